```python
import math
import jax, jax.numpy as jnp
from jax import lax
import numpy as np

D_MODEL = 2048
BATCH = 8
SEQ = 4096
DEPTH = 4

HEAD_DIM = 64
EPS = 1e-6
A_HEADS = 4
A_DK = 128
A_DV = 128
A_CHUNK = 64
A_OUT = A_HEADS * A_DV
B_HEADS = 8
B_WIDTH = B_HEADS * HEAD_DIM
SB_BLOCK = 128
C_PATTERNS = ((128, 1), (512, 4), (2048, 16))
C_GROUPS = 3
C_HPG = 4
C_HEADS = C_GROUPS * C_HPG
C_WIDTH = C_HEADS * HEAD_DIM
C_OUT = C_HPG * HEAD_DIM
C_BLOCK = 128
D_HEADS = 8
D_KV_HEADS = 2
D_GROUP = D_HEADS // D_KV_HEADS
D_WINDOW = 128
D_BLOCK = 128
D_Q = D_HEADS * HEAD_DIM
D_KV = D_KV_HEADS * HEAD_DIM
REL_BUCKETS = 32
REL_MAX_DIST = 2048
REL_HEADS = C_HEADS + D_HEADS
D_FF = 5632
MIX_SIZES = (A_HEADS * A_DK, A_HEADS * A_DK, A_OUT, A_OUT,
             B_WIDTH, B_WIDTH, B_WIDTH,
             C_WIDTH, C_WIDTH, C_WIDTH,
             D_Q, D_KV, D_KV)
MIX_IN = sum(MIX_SIZES)
N_BRANCH = 4
IN_TOTAL = MIX_IN + N_BRANCH * D_MODEL
BRANCH_WIDTHS = (A_OUT, B_WIDTH, C_OUT, D_Q)
BR_TOTAL = sum(BRANCH_WIDTHS)

kernel_name = 'hybrid_gated_parallel_mixers_trunk'


def _rms_norm(x, g):
    xf = x.astype(jnp.float32)
    y = xf * lax.rsqrt(jnp.mean(xf * xf, axis=-1, keepdims=True) + EPS)
    return (y * g.astype(jnp.float32)).astype(x.dtype)


def _swiglu(x, w13, w2):
    gate, up = jnp.split(x @ w13, 2, axis=-1)
    return (jax.nn.silu(gate) * up) @ w2


def _rel_bucket(dist):
    max_exact = REL_BUCKETS // 2
    d = jnp.maximum(dist, 1).astype(jnp.float32)
    large = max_exact + (jnp.log(d / max_exact) / math.log(REL_MAX_DIST / max_exact)
                         * (REL_BUCKETS - max_exact)).astype(jnp.int32)
    large = jnp.minimum(large, REL_BUCKETS - 1)
    return jnp.where(dist < max_exact, dist, large)


def _with_prev_block(t, nb_axis):
    pad = [(0, 0)] * t.ndim
    pad[nb_axis] = (1, 0)
    prev = lax.slice_in_dim(jnp.pad(t, pad), 0, t.shape[nb_axis], axis=nb_axis)
    return jnp.concatenate([prev, t], axis=nb_axis + 1)


def _hgrn2(q, f_logit, i, g, lb, norm_g):
    Bn, T, _ = q.shape
    nc = T // A_CHUNK
    f32 = jnp.float32
    f = lb + (1.0 - lb) * jax.nn.sigmoid(f_logit.astype(f32))

    def chunks(t):
        return t.astype(f32).reshape(Bn, nc, A_CHUNK, A_HEADS, -1).transpose(1, 0, 3, 2, 4)

    qc, kc, vc, gc = chunks(q), chunks(1.0 - f), chunks(i), chunks(jnp.log(f))
    causal = jnp.tril(jnp.ones((A_CHUNK, A_CHUNK), dtype=bool))[:, :, None]

    def step(S, inp):
        qb, kb, vb, gb = inp
        b = jnp.cumsum(gb, axis=2)
        b_last = b[:, :, -1:, :]
        o_inter = jnp.einsum('bhtk,bhkv->bhtv', qb * jnp.exp(b), S)
        decay = jnp.exp(jnp.where(causal, b[:, :, :, None, :] - b[:, :, None, :, :], -jnp.inf))
        scores = jnp.einsum('bhtk,bhsk,bhtsk->bhts', qb, kb, decay)
        o = o_inter + jnp.einsum('bhts,bhsv->bhtv', scores, vb)
        S = (jnp.exp(b_last[:, :, 0, :, None]) * S
             + jnp.einsum('bhsk,bhsv->bhkv', kb * jnp.exp(b_last - b), vb))
        return S, o

    S0 = jnp.zeros((Bn, A_HEADS, A_DK, A_DV), f32)
    _, o = lax.scan(step, S0, (qc, kc, vc, gc))
    o = o.transpose(1, 0, 3, 2, 4).reshape(Bn, T, A_HEADS, A_DV)
    o = _rms_norm(o, norm_g) * jax.nn.silu(g.astype(f32).reshape(Bn, T, A_HEADS, A_DV))
    return o.reshape(Bn, T, A_OUT).astype(q.dtype)


def _stick_breaking(q, k, v):
    Bn, T, _ = q.shape
    nb = T // SB_BLOCK
    scale = HEAD_DIM ** -0.5
    qh = q.reshape(Bn, nb, SB_BLOCK, B_HEADS, HEAD_DIM).transpose(1, 0, 3, 2, 4)
    kh = k.reshape(Bn, T, B_HEADS, HEAD_DIM)
    vh = v.reshape(Bn, T, B_HEADS, HEAD_DIM)
    key_pos = jnp.arange(T)

    def block(args):
        qb, blk = args
        z = jnp.einsum('bhqd,bshd->bhqs', qb, kh).astype(jnp.float32) * scale
        q_pos = blk * SB_BLOCK + jnp.arange(SB_BLOCK)
        past = key_pos[None, :] < q_pos[:, None]
        log_keep = jnp.where(past, -jax.nn.softplus(z), 0.0)
        between = lax.cumsum(log_keep, axis=3, reverse=True) - log_keep
        a = jnp.where(past, jnp.exp(jax.nn.log_sigmoid(z) + between), 0.0)
        return jnp.einsum('bhqs,bshd->bqhd', a.astype(v.dtype), vh)

    o = lax.map(block, (qh, jnp.arange(nb)))
    return o.transpose(1, 0, 2, 3, 4).reshape(Bn, T, B_WIDTH)


def _dilated_group(q, k, v, bias_tab, window, dilation):
    Bn, T, H, d = q.shape
    span = C_BLOCK * dilation
    Tp = -(-T // span) * span
    L = Tp // dilation
    nb = L // C_BLOCK

    def to_sub(t):
        t = jnp.pad(t, ((0, 0), (0, Tp - T), (0, 0), (0, 0)))
        t = t.reshape(Bn, L, dilation, H, d).transpose(0, 2, 1, 3, 4)
        return t.reshape(Bn, dilation, nb, C_BLOCK, H, d)

    def from_sub(t):
        rest = t.shape[4:]
        t = jnp.moveaxis(t.reshape((Bn, dilation, L) + rest), 1, 2)
        return t.reshape((Bn, Tp) + rest)[:, :T]

    qb = to_sub(q)
    kk = _with_prev_block(to_sub(k), 2)
    vv = _with_prev_block(to_sub(v), 2)
    qi = jnp.arange(C_BLOCK)[:, None] + C_BLOCK
    kj = jnp.arange(2 * C_BLOCK)[None, :]
    dist = qi - kj
    valid = (dist >= 0) & (dist <= window // dilation)
    valid = valid[None] & ((jnp.arange(nb)[:, None, None] > 0) | (kj[None] >= C_BLOCK))
    bias = bias_tab[_rel_bucket(jnp.maximum(dist, 0) * dilation)].transpose(2, 0, 1)
    logits = jnp.einsum('brnqhd,brnkhd->brnhqk', qb, kk).astype(jnp.float32) * HEAD_DIM ** -0.5
    logits = jnp.where(valid[None, None, :, None], logits + bias.astype(jnp.float32), -jnp.inf)
    m = jnp.max(logits, axis=-1, keepdims=True)
    p = jnp.exp(logits - m)
    den = jnp.sum(p, axis=-1)
    o = jnp.einsum('brnhqk,brnkhd->brnqhd', p, vv.astype(jnp.float32))
    o = o / jnp.swapaxes(den, -1, -2)[..., None]
    lse = jnp.swapaxes(m[..., 0] + jnp.log(den), -1, -2)
    return from_sub(o), from_sub(lse)


def _dilated_mixture(q, k, v, bias_tab):
    Bn, T, _ = q.shape
    q = q.reshape(Bn, T, C_GROUPS, C_HPG, HEAD_DIM)
    k = k.reshape(Bn, T, C_GROUPS, C_HPG, HEAD_DIM)
    v = v.reshape(Bn, T, C_GROUPS, C_HPG, HEAD_DIM)
    outs, lses = [], []
    for g, (window, dilation) in enumerate(C_PATTERNS):
        o, lse = _dilated_group(q[:, :, g], k[:, :, g], v[:, :, g],
                                bias_tab[:, g * C_HPG:(g + 1) * C_HPG], window, dilation)
        outs.append(o)
        lses.append(lse)
    w = jax.nn.softmax(jnp.stack(lses), axis=0)
    o = jnp.sum(w[..., None] * jnp.stack(outs), axis=0)
    return o.reshape(Bn, T, C_OUT).astype(q.dtype)


def _swa_sinks(q, k, v, sinks, bias_tab):
    Bn, T, _ = q.shape
    nb = T // D_BLOCK
    qb = q.reshape(Bn, nb, D_BLOCK, D_KV_HEADS, D_GROUP, HEAD_DIM)
    kk = _with_prev_block(k.reshape(Bn, nb, D_BLOCK, D_KV_HEADS, HEAD_DIM), 1)
    vv = _with_prev_block(v.reshape(Bn, nb, D_BLOCK, D_KV_HEADS, HEAD_DIM), 1)
    qi = jnp.arange(D_BLOCK)[:, None] + D_BLOCK
    kj = jnp.arange(2 * D_BLOCK)[None, :]
    dist = qi - kj
    valid = (dist >= 0) & (dist < D_WINDOW)
    valid = valid[None] & ((jnp.arange(nb)[:, None, None] > 0) | (kj[None] >= D_BLOCK))
    bias = bias_tab[_rel_bucket(jnp.maximum(dist, 0))].transpose(2, 0, 1)
    bias = bias.reshape(D_KV_HEADS, D_GROUP, D_BLOCK, 2 * D_BLOCK).astype(jnp.float32)
    logits = jnp.einsum('bnqkgd,bnskd->bnkgqs', qb, kk).astype(jnp.float32) * HEAD_DIM ** -0.5
    logits = jnp.where(valid[None, :, None, None], logits + bias, -jnp.inf)
    s = sinks.astype(jnp.float32).reshape(D_KV_HEADS, D_GROUP, 1, 1)
    m = jnp.maximum(jnp.max(logits, axis=-1, keepdims=True), s)
    p = jnp.exp(logits - m)
    den = jnp.sum(p, axis=-1, keepdims=True) + jnp.exp(s - m)
    o = jnp.einsum('bnkgqs,bnskd->bnqkgd', p / den, vv.astype(jnp.float32))
    return o.reshape(Bn, T, D_Q).astype(q.dtype)


def _token_mixing(u, w_in, lb, hgrn_norm_g, sinks, w_branch, w_out, rel_bias):
    proj = u @ w_in[:, :MIX_IN]
    (a_q, a_f, a_i, a_g, b_q, b_k, b_v, c_q, c_k, c_v, d_q, d_k, d_v) = jnp.split(
        proj, np.cumsum(MIX_SIZES)[:-1].tolist(), axis=-1)
    ys = (_hgrn2(a_q, a_f, a_i, a_g, lb, hgrn_norm_g),
          _stick_breaking(b_q, b_k, b_v),
          _dilated_mixture(c_q, c_k, c_v, rel_bias[:, :C_HEADS]),
          _swa_sinks(d_q, d_k, d_v, sinks, rel_bias[:, C_HEADS:]))
    merged = None
    row = 0
    for idx, (y, width) in enumerate(zip(ys, BRANCH_WIDTHS)):
        col = MIX_IN + idx * D_MODEL
        gate = jax.nn.sigmoid(u @ w_in[:, col:col + D_MODEL])
        term = gate * (y @ w_branch[row:row + width])
        merged = term if merged is None else merged + term
        row += width
    return merged @ w_out


def setup_inputs(seed: int = 0) -> dict:
    key = jax.random.key(seed)
    ks = jax.random.split(key, 16)
    D = D_MODEL

    def normal(k, shape, scale):
        return jax.random.normal(k, shape, jnp.float32) * scale

    br_scale = jnp.concatenate([jnp.full((w,), w ** -0.5, jnp.float32) for w in BRANCH_WIDTHS])
    return {
        'x': normal(ks[0], (BATCH, SEQ, D), 1.0),
        'ffn1_norm': 1.0 + normal(ks[1], (DEPTH, 2, D), 0.05),
        'ffn1_w13': normal(ks[2], (DEPTH, D, 2 * D_FF), D ** -0.5),
        'ffn1_w2': normal(ks[3], (DEPTH, D_FF, D), D_FF ** -0.5),
        'mix_norm': 1.0 + normal(ks[4], (DEPTH, 2, D), 0.05),
        'w_in': normal(ks[5], (DEPTH, D, IN_TOTAL), D ** -0.5),
        'hgrn_lb_logits': normal(ks[6], (DEPTH, A_HEADS * A_DK), 0.1),
        'hgrn_out_norm': 1.0 + normal(ks[7], (DEPTH, A_DV), 0.05),
        'attn_sinks': normal(ks[8], (DEPTH, D_HEADS), 0.5),
        'w_branch': normal(ks[9], (DEPTH, BR_TOTAL, D), 1.0) * br_scale[None, :, None],
        'w_out': normal(ks[10], (DEPTH, D, D), D ** -0.5),
        'ffn2_norm': 1.0 + normal(ks[11], (DEPTH, 2, D), 0.05),
        'ffn2_w13': normal(ks[12], (DEPTH, D, 2 * D_FF), D ** -0.5),
        'ffn2_w2': normal(ks[13], (DEPTH, D_FF, D), D_FF ** -0.5),
        'rel_bias': normal(ks[14], (REL_BUCKETS, REL_HEADS), 0.5),
    }


def reference(x, ffn1_norm, ffn1_w13, ffn1_w2, mix_norm, w_in, hgrn_lb_logits, hgrn_out_norm,
              attn_sinks, w_branch, w_out, ffn2_norm, ffn2_w13, ffn2_w2, rel_bias):
    lb_sm = jax.nn.softmax(hgrn_lb_logits.astype(jnp.float32), axis=0)
    lb_all = jnp.cumsum(lb_sm, axis=0) - lb_sm[0:1]
    h = x
    for l in range(DEPTH):
        f1 = _swiglu(_rms_norm(h, ffn1_norm[l, 0]), ffn1_w13[l], ffn1_w2[l])
        h = h + 0.5 * _rms_norm(f1, ffn1_norm[l, 1])
        u = _rms_norm(h, mix_norm[l, 0])
        mix = _token_mixing(u, w_in[l], lb_all[l], hgrn_out_norm[l], attn_sinks[l],
                            w_branch[l], w_out[l], rel_bias)
        h = h + _rms_norm(mix, mix_norm[l, 1])
        f2 = _swiglu(_rms_norm(h, ffn2_norm[l, 0]), ffn2_w13[l], ffn2_w2[l])
        h = h + 0.5 * _rms_norm(f2, ffn2_norm[l, 1])
    return h
```

```python
import functools
import math

import jax
import jax.numpy as jnp
from jax import lax
from jax.experimental import pallas as pl
from jax.experimental.pallas import tpu as pltpu

F32 = jnp.float32
BF16 = jnp.bfloat16

EPS = 1e-6
HEAD_DIM = 64
LANES = 128
A_HEADS = 4
A_DK = 128
B_HEADS = 8
C_PATTERNS = ((128, 1), (512, 4), (2048, 16))
C_HPG = 4
C_HEADS = len(C_PATTERNS) * C_HPG
D_HEADS = 8
D_KV_HEADS = 2
D_WINDOW = 128
WIN_BLOCK = 128
REL_BUCKETS = 32
REL_MAX_DIST = 2048
A_SUB = 16
VMEM_LIMIT = 56 * 1024 * 1024

_BQ, _BK, _BV = 0, 4, 8
_CQ, _CK, _CV = 12, 18, 24
_DQ, _DK, _DV = 30, 34, 35
BCD_BLOCKS = 36
D_HEAD_PERM = (0, 4, 1, 5, 2, 6, 3, 7)


def _rms(x, g):
    ms = jnp.mean(x * x, axis=-1, keepdims=True)
    return x * lax.rsqrt(ms + EPS) * g


def _sigmoid(x):
    return 1.0 / (1.0 + jnp.exp(-x))


def _dot(a, b):
    return jnp.dot(a, b, preferred_element_type=F32)


def _dot_nt(a, b):
    return lax.dot_general(a, b, (((1,), (1,)), ((), ())), preferred_element_type=F32)


def _dot_tn(a, b):
    return lax.dot_general(a, b, (((0,), (0,)), ((), ())), preferred_element_type=F32)


def _split_bf16(x):
    hi = x.astype(BF16)
    lo = (x - hi.astype(F32)).astype(BF16)
    return hi, lo


def _ffn_kernel(x_ref, g_ref, w1_ref, w3_ref, w2_ref, o_ref, xn_ref, acc_ref):
    j = pl.program_id(1)

    @pl.when(j == 0)
    def _():
        xn_ref[...] = _rms(x_ref[...], g_ref[0:1, :]).astype(BF16)
        acc_ref[...] = jnp.zeros_like(acc_ref)

    xn = xn_ref[...]
    gate = _dot(xn, w1_ref[...])
    up = _dot(xn, w3_ref[...])
    act = (gate * _sigmoid(gate) * up).astype(BF16)
    acc_ref[...] += _dot(act, w2_ref[...])

    @pl.when(j == pl.num_programs(1) - 1)
    def _():
        o_ref[...] = x_ref[...] + 0.5 * _rms(acc_ref[...], g_ref[1:2, :])


def _ffn(h, norm, w13, w2, layer, tm, tf):
    M, D = h.shape
    F = w2.shape[1]
    nf = F // tf
    return pl.pallas_call(
        _ffn_kernel,
        grid=(M // tm, nf),
        in_specs=[
            pl.BlockSpec((tm, D), lambda i, j: (i, 0)),
            pl.BlockSpec((None, 2, D), lambda i, j: (layer, 0, 0)),
            pl.BlockSpec((None, D, tf), lambda i, j: (layer, 0, j)),
            pl.BlockSpec((None, D, tf), lambda i, j: (layer, 0, j + nf)),
            pl.BlockSpec((None, tf, D), lambda i, j: (layer, j, 0)),
        ],
        out_specs=pl.BlockSpec((tm, D), lambda i, j: (i, 0)),
        out_shape=jax.ShapeDtypeStruct((M, D), F32),
        scratch_shapes=[pltpu.VMEM((tm, D), BF16), pltpu.VMEM((tm, D), F32)],
        compiler_params=pltpu.CompilerParams(
            dimension_semantics=("parallel", "arbitrary"), vmem_limit_bytes=VMEM_LIMIT),
        name="ffn",
    )(h, norm, w13, w13, w2)


def _proj_kernel(x_ref, g_ref, w_ref, o_ref, xn_ref):
    @pl.when(pl.program_id(1) == 0)
    def _():
        xn_ref[...] = _rms(x_ref[...], g_ref[0:1, :]).astype(BF16)

    o_ref[...] = _dot(xn_ref[...], w_ref[...]).astype(o_ref.dtype)


def _proj(h, norm, w_in, layer, col0, ncols, out_dtype, tm, tn):
    M, D = h.shape
    cb0 = col0 // tn
    return pl.pallas_call(
        _proj_kernel,
        grid=(M // tm, ncols // tn),
        in_specs=[
            pl.BlockSpec((tm, D), lambda i, j: (i, 0)),
            pl.BlockSpec((None, 2, D), lambda i, j: (layer, 0, 0)),
            pl.BlockSpec((None, D, tn), lambda i, j: (layer, 0, cb0 + j)),
        ],
        out_specs=pl.BlockSpec((tm, tn), lambda i, j: (i, j)),
        out_shape=jax.ShapeDtypeStruct((M, ncols), out_dtype),
        scratch_shapes=[pltpu.VMEM((tm, D), BF16)],
        compiler_params=pltpu.CompilerParams(
            dimension_semantics=("parallel", "arbitrary"), vmem_limit_bytes=VMEM_LIMIT),
        name="proj",
    )(h, norm, w_in)


def _hgrn_kernel(q_ref, f_ref, i_ref, g_ref, lb_ref, ng_ref, o_ref,
                 st_ref, bl_ref, k_ref, oacc_ref):
    tt = q_ref.shape[1]
    sub = A_SUB

    @pl.when(pl.program_id(2) == 0)
    def _():
        st_ref[...] = jnp.zeros_like(st_ref)

    lb = lb_ref[0]
    f = lb + (1.0 - lb) * _sigmoid(f_ref[0])
    k_ref[...] = 1.0 - f
    gl = jnp.log(f)
    r = lax.broadcasted_iota(jnp.int32, (LANES, LANES), 0)
    c = lax.broadcasted_iota(jnp.int32, (LANES, LANES), 1)
    tril = ((r // sub == c // sub) & (c <= r)).astype(BF16)
    for a in range(tt // LANES):
        hi, lo = _split_bf16(gl[a * LANES:(a + 1) * LANES])
        bl_ref[a * LANES:(a + 1) * LANES, :] = _dot(tril, hi) + _dot(tril, lo)

    ones = jnp.ones((LANES, LANES), BF16)
    rows = lax.broadcasted_iota(jnp.int32, (sub, LANES), 0)

    def body(blk, carry):
        sl = pl.ds(pl.multiple_of(blk * sub, sub), sub)
        qb = q_ref[0, sl, :]
        vb = i_ref[0, sl, :]
        bb = bl_ref[sl, :]
        kb = k_ref[sl, :]
        st = st_ref[...]
        o = _dot_nt((qb * jnp.exp(bb)).astype(BF16), st.astype(BF16))
        parts = []
        for s in range(sub):
            e = jnp.exp(bb - bb[s:s + 1, :])
            parts.append(jnp.where(rows >= s, qb * e * kb[s:s + 1, :], 0.0).astype(BF16))
        scores = _dot(jnp.concatenate(parts, axis=0), ones)
        for s in range(sub):
            o = o + scores[s * sub:(s + 1) * sub, :] * vb[s:s + 1, :]
        oacc_ref[sl, :] = o
        b_last = bb[sub - 1:sub, :]
        kd = kb * jnp.exp(b_last - bb)
        st_ref[...] = jnp.exp(b_last) * st + _dot_tn(vb.astype(BF16), kd.astype(BF16))
        return carry

    lax.fori_loop(0, tt // sub, body, 0)

    g = g_ref[0]
    o_ref[0] = (_rms(oacc_ref[...], ng_ref[0]) * (g * _sigmoid(g))).astype(o_ref.dtype)


def _hgrn(pa, lb, ng, tt):
    Bn, T, _ = pa.shape
    H = A_HEADS

    def col(off):
        return pl.BlockSpec((1, tt, A_DK), lambda b, h, t: (b, t, off * H + h))

    return pl.pallas_call(
        _hgrn_kernel,
        grid=(Bn, H, T // tt),
        in_specs=[col(0), col(1), col(2), col(3),
                  pl.BlockSpec((1, 1, A_DK), lambda b, h, t: (h, 0, 0)),
                  pl.BlockSpec((1, 1, A_DK), lambda b, h, t: (0, 0, 0))],
        out_specs=pl.BlockSpec((1, tt, A_DK), lambda b, h, t: (b, t, h)),
        out_shape=jax.ShapeDtypeStruct((Bn, T, H * A_DK), BF16),
        scratch_shapes=[pltpu.VMEM((A_DK, A_DK), F32), pltpu.VMEM((tt, A_DK), F32),
                        pltpu.VMEM((tt, A_DK), F32), pltpu.VMEM((tt, A_DK), F32)],
        compiler_params=pltpu.CompilerParams(
            dimension_semantics=("parallel", "parallel", "arbitrary"), vmem_limit_bytes=VMEM_LIMIT),
        name="hgrn2",
    )(pa, pa, pa, pa, lb, ng)


def _sb_kernel(q_ref, k_ref, v_ref, o_ref):
    tq = q_ref.shape[1]
    i = pl.program_id(2)
    q = q_ref[0]
    lane = lax.broadcasted_iota(jnp.int32, (tq, LANES), 1)
    r = lax.broadcasted_iota(jnp.int32, (tq, tq), 0)
    c = lax.broadcasted_iota(jnp.int32, (tq, tq), 1)
    suffix = (r >= c).astype(BF16)
    strict = c < r

    def tile(qm, kb, vb, run, diag):
        z = _dot_nt(qm, kb)
        sp = jnp.maximum(z, 0.0) + jnp.log(1.0 + jnp.exp(-jnp.abs(z)))
        if diag:
            sp = jnp.where(strict, sp, 0.0)
        hi, lo = _split_bf16(sp)
        cs = _dot(hi, suffix) + _dot(lo, suffix) + run
        a = jnp.exp(z - cs)
        if diag:
            a = jnp.where(strict, a, 0.0)
        return _dot(a.astype(BF16), vb), cs[:, 0:1]

    outs = []
    for hh in range(2):
        in_head = (lane >= hh * HEAD_DIM) & (lane < (hh + 1) * HEAD_DIM)
        qm = jnp.where(in_head, q, jnp.zeros_like(q)) * (HEAD_DIM ** -0.5)
        d0 = pl.ds(pl.multiple_of(i * tq, tq), tq)
        acc, run = tile(qm, k_ref[0, d0, :], v_ref[0, d0, :], jnp.zeros((tq, 1), F32), True)

        def body(s, carry, qm=qm):
            acc, run = carry
            sl = pl.ds(pl.multiple_of((i - 1 - s) * tq, tq), tq)
            pv, run = tile(qm, k_ref[0, sl, :], v_ref[0, sl, :], run, False)
            return acc + pv, run

        acc, run = lax.fori_loop(0, i, body, (acc, run))
        outs.append(acc)
    o_ref[0] = jnp.where(lane < HEAD_DIM, outs[0], outs[1]).astype(o_ref.dtype)


def _stick_breaking(pbcd, tq):
    Bn, T, _ = pbcd.shape
    npair = B_HEADS // 2
    return pl.pallas_call(
        _sb_kernel,
        grid=(Bn, npair, T // tq),
        in_specs=[pl.BlockSpec((1, tq, LANES), lambda b, p, i: (b, i, _BQ + p)),
                  pl.BlockSpec((1, T, LANES), lambda b, p, i: (b, 0, _BK + p)),
                  pl.BlockSpec((1, T, LANES), lambda b, p, i: (b, 0, _BV + p))],
        out_specs=pl.BlockSpec((1, tq, LANES), lambda b, p, i: (b, i, p)),
        out_shape=jax.ShapeDtypeStruct((Bn, T, npair * LANES), BF16),
        compiler_params=pltpu.CompilerParams(
            dimension_semantics=("parallel", "parallel", "arbitrary"), vmem_limit_bytes=VMEM_LIMIT),
        name="stick_breaking",
    )(pbcd, pbcd, pbcd)


def _win_kernel(q_ref, kc_ref, kp_ref, vc_ref, vp_ref, bias_ref, sink_ref, *out_refs):
    blk = q_ref.shape[1]
    i = pl.program_id(2)
    q = q_ref[0]
    kc, kp, vc, vp = kc_ref[0], kp_ref[0], vc_ref[0], vp_ref[0]
    lane = lax.broadcasted_iota(jnp.int32, (blk, LANES), 1)
    scale = HEAD_DIM ** -0.5
    os, ls = [], []
    for hh in range(2):
        in_head = (lane >= hh * HEAD_DIM) & (lane < (hh + 1) * HEAD_DIM)
        qm = jnp.where(in_head, q, jnp.zeros_like(q))
        bias = bias_ref[hh]
        lp = _dot_nt(qm, kp) * scale + bias[:, :blk]
        lp = jnp.where(i > 0, lp, -jnp.inf)
        lc = _dot_nt(qm, kc) * scale + bias[:, blk:]
        sink = sink_ref[hh:hh + 1, 0:1]
        m = jnp.maximum(jnp.max(lp, axis=-1, keepdims=True), jnp.max(lc, axis=-1, keepdims=True))
        m = jnp.maximum(m, sink)
        pp = jnp.exp(lp - m)
        pc = jnp.exp(lc - m)
        den = (jnp.sum(pp, axis=-1, keepdims=True) + jnp.sum(pc, axis=-1, keepdims=True)
               + jnp.exp(sink - m))
        o = (_dot(pp.astype(BF16), vp) + _dot(pc.astype(BF16), vc)) / den
        os.append(o)
        ls.append(m + jnp.log(den))
    out_refs[0][0] = jnp.where(lane < HEAD_DIM, os[0], os[1]).astype(out_refs[0].dtype)
    if len(out_refs) > 1:
        out_refs[1][0] = jnp.where(lane < HEAD_DIM, ls[0], ls[1])


def _window_attn(pbcd, bias, sinks, dil, qcol, kcol, vcol, npair, kv_shared, with_lse, out_dtype):
    Bn, T, W = pbcd.shape
    L = T // dil
    nb = L // WIN_BLOCK
    view = pbcd.reshape(Bn, L, dil * W)
    wb = W // LANES

    def qmap(b, j, i):
        return (b, i, (j // npair) * wb + qcol + j % npair)

    def kvmap(col, prev):
        def f(b, j, i):
            p = 0 if kv_shared else j % npair
            return (b, jnp.maximum(i - 1, 0) if prev else i, (j // npair) * wb + col + p)
        return f

    blk = (1, WIN_BLOCK, LANES)
    out_spec = pl.BlockSpec(blk, lambda b, j, i: (b, i, j))
    out_sds = jax.ShapeDtypeStruct((Bn, L, dil * npair * LANES), out_dtype)
    lse_sds = jax.ShapeDtypeStruct((Bn, L, dil * npair * LANES), F32)
    res = pl.pallas_call(
        _win_kernel,
        grid=(Bn, dil * npair, nb),
        in_specs=[pl.BlockSpec(blk, qmap),
                  pl.BlockSpec(blk, kvmap(kcol, False)), pl.BlockSpec(blk, kvmap(kcol, True)),
                  pl.BlockSpec(blk, kvmap(vcol, False)), pl.BlockSpec(blk, kvmap(vcol, True)),
                  pl.BlockSpec((None, 2, WIN_BLOCK, 2 * WIN_BLOCK), lambda b, j, i: (j % npair, 0, 0, 0)),
                  pl.BlockSpec((None, 2, LANES), lambda b, j, i: (j % npair, 0, 0))],
        out_specs=[out_spec, out_spec] if with_lse else [out_spec],
        out_shape=[out_sds, lse_sds] if with_lse else [out_sds],
        compiler_params=pltpu.CompilerParams(
            dimension_semantics=("parallel", "parallel", "arbitrary"), vmem_limit_bytes=VMEM_LIMIT),
        name="window_attn",
    )(view, view, view, view, view, bias, sinks)
    return [x.reshape(Bn, T, npair * LANES) for x in res]


def _combine_kernel(o0, o1, o2, l0, l1, l2, out_ref):
    a, b, c = l0[...], l1[...], l2[...]
    m = jnp.maximum(jnp.maximum(a, b), c)
    ea, eb, ec = jnp.exp(a - m), jnp.exp(b - m), jnp.exp(c - m)
    out_ref[...] = ((ea * o0[...] + eb * o1[...] + ec * o2[...]) / (ea + eb + ec)).astype(out_ref.dtype)


def _combine(outs, lses, tm):
    M, W = outs[0].shape
    spec = pl.BlockSpec((tm, W), lambda i: (i, 0))
    return pl.pallas_call(
        _combine_kernel,
        grid=(M // tm,),
        in_specs=[spec] * 6,
        out_specs=spec,
        out_shape=jax.ShapeDtypeStruct((M, W), BF16),
        compiler_params=pltpu.CompilerParams(dimension_semantics=("parallel",)),
        name="group_combine",
    )(*outs, *lses)


def _merge_kernel(x_ref, g_ref, ya_ref, yb_ref, yc_ref, yd_ref,
                  ga_ref, gb_ref, gc_ref, gd_ref, wa_ref, wb_ref, wc_ref, wd_ref, wo_ref,
                  o_ref, u_ref, acc_ref):
    j = pl.program_id(1)

    @pl.when(j == 0)
    def _():
        u_ref[...] = _rms(x_ref[...], g_ref[0:1, :]).astype(BF16)
        acc_ref[...] = jnp.zeros_like(acc_ref)

    u = u_ref[...]
    merged = None
    for y_ref, gate_ref, w_ref in ((ya_ref, ga_ref, wa_ref), (yb_ref, gb_ref, wb_ref),
                                   (yc_ref, gc_ref, wc_ref), (yd_ref, gd_ref, wd_ref)):
        term = _sigmoid(_dot(u, gate_ref[...])) * _dot(y_ref[...], w_ref[...])
        merged = term if merged is None else merged + term
    acc_ref[...] += _dot(merged.astype(BF16), wo_ref[...])

    @pl.when(j == pl.num_programs(1) - 1)
    def _():
        o_ref[...] = x_ref[...] + _rms(acc_ref[...], g_ref[1:2, :])


def _merge(h, norm, ys, w_in, gate_col0, w_branch, w_out, layer, tm, tn):
    M, D = h.shape
    gcb = gate_col0 // tn
    dcb = D // tn

    def yspec(w):
        return pl.BlockSpec((tm, w), lambda i, j: (i, 0))

    def gspec(br):
        return pl.BlockSpec((None, D, tn), lambda i, j: (layer, 0, gcb + br * dcb + j))

    def wspec(rows, rb):
        return pl.BlockSpec((None, rows, tn), lambda i, j: (layer, rb, j))

    wa, wb, wc, wd = (y.shape[1] for y in ys)
    return pl.pallas_call(
        _merge_kernel,
        grid=(M // tm, dcb),
        in_specs=[pl.BlockSpec((tm, D), lambda i, j: (i, 0)),
                  pl.BlockSpec((None, 2, D), lambda i, j: (layer, 0, 0)),
                  yspec(wa), yspec(wb), yspec(wc), yspec(wd),
                  gspec(0), gspec(1), gspec(2), gspec(3),
                  wspec(wa, 0), wspec(wb, wa // wb), wspec(wc, (wa + wb + wd) // wc),
                  wspec(wd, (wa + wb) // wd),
                  pl.BlockSpec((None, tn, D), lambda i, j: (layer, j, 0))],
        out_specs=pl.BlockSpec((tm, D), lambda i, j: (i, 0)),
        out_shape=jax.ShapeDtypeStruct((M, D), F32),
        scratch_shapes=[pltpu.VMEM((tm, D), BF16), pltpu.VMEM((tm, D), F32)],
        compiler_params=pltpu.CompilerParams(
            dimension_semantics=("parallel", "arbitrary"), vmem_limit_bytes=VMEM_LIMIT),
        name="gated_merge",
    )(h, norm, *ys, w_in, w_in, w_in, w_in, w_branch, w_branch, w_branch, w_branch, w_out)


def _rel_bucket(dist):
    max_exact = REL_BUCKETS // 2
    d = jnp.maximum(dist, 1).astype(F32)
    large = max_exact + (jnp.log(d / max_exact) / math.log(REL_MAX_DIST / max_exact)
                         * (REL_BUCKETS - max_exact)).astype(jnp.int32)
    large = jnp.minimum(large, REL_BUCKETS - 1)
    return jnp.where(dist < max_exact, dist, large)


def _window_bias(rel_bias, heads, dilation, max_dist):
    row = jnp.arange(WIN_BLOCK)[:, None]
    col = jnp.arange(2 * WIN_BLOCK)[None, :]
    dist = row + WIN_BLOCK - col
    valid = (dist >= 0) & (dist <= max_dist)
    tab = rel_bias[_rel_bucket(jnp.maximum(dist, 0) * dilation)]
    tab = jnp.transpose(tab[:, :, jnp.array(heads)], (2, 0, 1)).astype(F32)
    tab = jnp.where(valid[None], tab, -jnp.inf)
    return tab.reshape(len(heads) // 2, 2, WIN_BLOCK, 2 * WIN_BLOCK)


def _trunk(x, ffn1_norm, ffn1_w13, ffn1_w2, mix_norm, w_in, hgrn_lb_logits, hgrn_out_norm,
           attn_sinks, w_branch, w_out, ffn2_norm, ffn2_w13, ffn2_w2, rel_bias, cfg):
    Bn, T, D = x.shape
    depth = w_in.shape[0]
    M = Bn * T
    a_w = A_HEADS * A_DK
    a_cols = 4 * a_w
    bcd_cols = BCD_BLOCKS * LANES
    mix_in = a_cols + bcd_cols

    lb_sm = jax.nn.softmax(hgrn_lb_logits.astype(F32), axis=0)
    lb_all = (jnp.cumsum(lb_sm, axis=0) - lb_sm[0:1]).reshape(depth, A_HEADS, 1, A_DK)

    dq0 = a_cols + _DQ * LANES
    perm = jnp.array([h * HEAD_DIM + d for h in D_HEAD_PERM for d in range(HEAD_DIM)])
    w_in_b = jnp.concatenate(
        [w_in[:, :, :dq0], w_in[:, :, dq0:dq0 + D_HEADS * HEAD_DIM][:, :, perm],
         w_in[:, :, dq0 + D_HEADS * HEAD_DIM:]], axis=-1).astype(BF16)
    c_w = C_HPG * HEAD_DIM
    d0 = 2 * a_w + c_w
    w_br_b = jnp.concatenate(
        [w_branch[:, :2 * a_w], w_branch[:, d0:][:, perm], w_branch[:, 2 * a_w:d0]], axis=1).astype(BF16)
    w_out_b = w_out.astype(BF16)
    f1_w13, f1_w2 = ffn1_w13.astype(BF16), ffn1_w2.astype(BF16)
    f2_w13, f2_w2 = ffn2_w13.astype(BF16), ffn2_w2.astype(BF16)

    c_bias = [_window_bias(rel_bias, list(range(g * C_HPG, (g + 1) * C_HPG)), dil, win // dil)
              for g, (win, dil) in enumerate(C_PATTERNS)]
    d_bias = _window_bias(rel_bias, [C_HEADS + h for h in D_HEAD_PERM], 1, D_WINDOW - 1)
    no_sink = jnp.full((C_HPG // 2, 2, LANES), -jnp.inf, F32)

    h = x.reshape(M, D)
    for l in range(depth):
        h = _ffn(h, ffn1_norm, f1_w13, f1_w2, l, cfg["ffn_tm"], cfg["ffn_tf"])

        pa = _proj(h, mix_norm, w_in_b, l, 0, a_cols, F32, cfg["proj_tm"], cfg["proj_tn"])
        pbcd = _proj(h, mix_norm, w_in_b, l, a_cols, bcd_cols, BF16, cfg["proj_tm"], cfg["proj_tn"])
        pa = pa.reshape(Bn, T, a_cols)
        pbcd = pbcd.reshape(Bn, T, bcd_cols)

        ya = _hgrn(pa, lb_all[l], hgrn_out_norm[l].reshape(1, 1, A_DK).astype(F32), cfg["hgrn_tt"])
        yb = _stick_breaking(pbcd, cfg["sb_tq"])
        c_out, c_lse = [], []
        for g, (win, dil) in enumerate(C_PATTERNS):
            o, lse = _window_attn(pbcd, c_bias[g], no_sink, dil, _CQ + 2 * g, _CK + 2 * g, _CV + 2 * g,
                                  C_HPG // 2, False, True, F32)
            c_out.append(o.reshape(M, c_w))
            c_lse.append(lse.reshape(M, c_w))
        yc = _combine(c_out, c_lse, cfg["comb_tm"])
        sinks = attn_sinks[l].astype(F32)[jnp.array(D_HEAD_PERM)].reshape(D_HEADS // 2, 2, 1)
        sinks = jnp.broadcast_to(sinks, (D_HEADS // 2, 2, LANES))
        (yd,) = _window_attn(pbcd, d_bias, sinks, 1, _DQ, _DK, _DV, D_HEADS // 2, True, False, BF16)

        ys = (ya.reshape(M, a_w), yb.reshape(M, a_w), yc, yd.reshape(M, a_w))
        h = _merge(h, mix_norm, ys, w_in_b, mix_in, w_br_b, w_out_b, l, cfg["merge_tm"], cfg["merge_tn"])

        h = _ffn(h, ffn2_norm, f2_w13, f2_w2, l, cfg["ffn_tm"], cfg["ffn_tf"])
    return h.reshape(Bn, T, D)


_CFG = dict(ffn_tm=512, ffn_tf=512, proj_tm=1024, proj_tn=512, hgrn_tt=512, sb_tq=256,
            comb_tm=1024, merge_tm=512, merge_tn=256)


def kernel(x, ffn1_norm, ffn1_w13, ffn1_w2, mix_norm, w_in, hgrn_lb_logits, hgrn_out_norm, attn_sinks,
           w_branch, w_out, ffn2_norm, ffn2_w13, ffn2_w2, rel_bias):
    return _trunk(x, ffn1_norm, ffn1_w13, ffn1_w2, mix_norm, w_in, hgrn_lb_logits, hgrn_out_norm,
                  attn_sinks, w_branch, w_out, ffn2_norm, ffn2_w13, ffn2_w2, rel_bias, _CFG)
```

```python
import functools
import math

import jax
import jax.numpy as jnp
from jax import lax
from jax.experimental import pallas as pl
from jax.experimental.pallas import tpu as pltpu

F32 = jnp.float32
BF16 = jnp.bfloat16

EPS = 1e-6
HEAD_DIM = 64
LANES = 128
A_HEADS = 4
A_DK = 128
B_HEADS = 8
C_PATTERNS = ((128, 1), (512, 4), (2048, 16))
C_HPG = 4
C_HEADS = len(C_PATTERNS) * C_HPG
D_HEADS = 8
D_KV_HEADS = 2
D_WINDOW = 128
WIN_BLOCK = 128
REL_BUCKETS = 32
REL_MAX_DIST = 2048
A_SUB = 16
A_UNROLL = 4
VMEM_LIMIT = 56 * 1024 * 1024

_CQ, _CK, _CV = 16, 22, 28
AC_BLOCKS = 34
_BQ, _BK, _BV = 0, 4, 8
_DQ, _DK, _DV = 12, 16, 17
BD_BLOCKS = 18
D_HEAD_PERM = (0, 4, 1, 5, 2, 6, 3, 7)
SB_EXIT = 100.0


def _rms(x, g):
    ms = jnp.mean(x * x, axis=-1, keepdims=True)
    return x * lax.rsqrt(ms + EPS) * g


def _sigmoid(x):
    return 1.0 / (1.0 + jnp.exp(-x))


def _dot(a, b):
    return jnp.dot(a, b, preferred_element_type=F32)


def _dot_nt(a, b):
    return lax.dot_general(a, b, (((1,), (1,)), ((), ())), preferred_element_type=F32)


def _dot_tn(a, b):
    return lax.dot_general(a, b, (((0,), (0,)), ((), ())), preferred_element_type=F32)


def _split_bf16(x):
    hi = x.astype(BF16)
    lo = (x - hi.astype(F32)).astype(BF16)
    return hi, lo


def _ffn_kernel(x_ref, g_ref, w1_ref, w3_ref, w2_ref, o_ref, xn_ref, acc_ref):
    j = pl.program_id(1)

    @pl.when(j == 0)
    def _():
        xn_ref[...] = _rms(x_ref[...], g_ref[0:1, :]).astype(BF16)
        acc_ref[...] = jnp.zeros_like(acc_ref)

    xn = xn_ref[...]
    gate = _dot(xn, w1_ref[...])
    up = _dot(xn, w3_ref[...])
    act = (gate * _sigmoid(gate) * up).astype(BF16)
    acc_ref[...] += _dot(act, w2_ref[...])

    @pl.when(j == pl.num_programs(1) - 1)
    def _():
        o_ref[...] = x_ref[...] + 0.5 * _rms(acc_ref[...], g_ref[1:2, :])


def _ffn(h, norm, w13, w2, layer, tm, tf):
    M, D = h.shape
    F = w2.shape[1]
    nf = F // tf
    return pl.pallas_call(
        _ffn_kernel,
        grid=(M // tm, nf),
        in_specs=[
            pl.BlockSpec((tm, D), lambda i, j: (i, 0)),
            pl.BlockSpec((None, 2, D), lambda i, j: (layer, 0, 0)),
            pl.BlockSpec((None, D, tf), lambda i, j: (layer, 0, j)),
            pl.BlockSpec((None, D, tf), lambda i, j: (layer, 0, j + nf)),
            pl.BlockSpec((None, tf, D), lambda i, j: (layer, j, 0)),
        ],
        out_specs=pl.BlockSpec((tm, D), lambda i, j: (i, 0)),
        out_shape=jax.ShapeDtypeStruct((M, D), F32),
        scratch_shapes=[pltpu.VMEM((tm, D), BF16), pltpu.VMEM((tm, D), F32)],
        compiler_params=pltpu.CompilerParams(
            dimension_semantics=("parallel", "arbitrary"), vmem_limit_bytes=VMEM_LIMIT),
        name="ffn",
    )(h, norm, w13, w13, w2)


def _proj_kernel(x_ref, g_ref, w_ref, o_ref, xn_ref):
    @pl.when(pl.program_id(1) == 0)
    def _():
        xn_ref[...] = _rms(x_ref[...], g_ref[0:1, :]).astype(BF16)

    o_ref[...] = _dot(xn_ref[...], w_ref[...]).astype(o_ref.dtype)


def _proj(h, norm, w_in, layer, col0, ncols, out_dtype, tm, tn):
    M, D = h.shape
    cb0 = col0 // tn
    return pl.pallas_call(
        _proj_kernel,
        grid=(M // tm, ncols // tn),
        in_specs=[
            pl.BlockSpec((tm, D), lambda i, j: (i, 0)),
            pl.BlockSpec((None, 2, D), lambda i, j: (layer, 0, 0)),
            pl.BlockSpec((None, D, tn), lambda i, j: (layer, 0, cb0 + j)),
        ],
        out_specs=pl.BlockSpec((tm, tn), lambda i, j: (i, j)),
        out_shape=jax.ShapeDtypeStruct((M, ncols), out_dtype),
        scratch_shapes=[pltpu.VMEM((tm, D), BF16)],
        compiler_params=pltpu.CompilerParams(
            dimension_semantics=("parallel", "arbitrary"), vmem_limit_bytes=VMEM_LIMIT),
        name="proj",
    )(h, norm, w_in)


def _hgrn_kernel(q_ref, f_ref, i_ref, g_ref, lb_ref, ng_ref, o_ref,
                 st_ref, bl_ref, k_ref, oacc_ref):
    tt = q_ref.shape[1]
    sub = A_SUB

    @pl.when(pl.program_id(2) == 0)
    def _():
        st_ref[...] = jnp.zeros_like(st_ref)

    lb = lb_ref[0]
    f = lb + (1.0 - lb) * _sigmoid(f_ref[0])
    k_ref[...] = 1.0 - f
    gl = jnp.log(f)
    r = lax.broadcasted_iota(jnp.int32, (LANES, LANES), 0)
    c = lax.broadcasted_iota(jnp.int32, (LANES, LANES), 1)
    tril = ((r // sub == c // sub) & (c <= r)).astype(BF16)
    for a in range(tt // LANES):
        hi, lo = _split_bf16(gl[a * LANES:(a + 1) * LANES])
        bl_ref[a * LANES:(a + 1) * LANES, :] = _dot(tril, hi) + _dot(tril, lo)

    ones = jnp.ones((LANES, LANES), BF16)
    rows = lax.broadcasted_iota(jnp.int32, (sub, LANES), 0)

    def body(blk, st):
        sl = pl.ds(pl.multiple_of(blk * sub, sub), sub)
        qb = q_ref[0, sl, :]
        vb = i_ref[0, sl, :]
        bb = bl_ref[sl, :]
        kb = k_ref[sl, :]
        o = _dot_nt((qb * jnp.exp(bb)).astype(BF16), st.astype(BF16))
        parts = []
        for s in range(sub):
            e = jnp.exp(bb - bb[s:s + 1, :])
            parts.append(jnp.where(rows >= s, qb * e * kb[s:s + 1, :], 0.0).astype(BF16))
        scores = _dot(jnp.concatenate(parts, axis=0), ones)
        for s in range(sub):
            o = o + scores[s * sub:(s + 1) * sub, :] * vb[s:s + 1, :]
        oacc_ref[sl, :] = o
        b_last = bb[sub - 1:sub, :]
        kd = kb * jnp.exp(b_last - bb)
        return jnp.exp(b_last) * st + _dot_tn(vb.astype(BF16), kd.astype(BF16))

    st_ref[...] = lax.fori_loop(0, tt // sub, body, st_ref[...], unroll=A_UNROLL)

    g = g_ref[0]
    o_ref[0] = (_rms(oacc_ref[...], ng_ref[0]) * (g * _sigmoid(g))).astype(o_ref.dtype)


def _hgrn(pa, lb, ng, tt):
    Bn, T, _ = pa.shape
    H = A_HEADS

    def col(off):
        return pl.BlockSpec((1, tt, A_DK), lambda b, h, t: (b, t, off * H + h))

    return pl.pallas_call(
        _hgrn_kernel,
        grid=(Bn, H, T // tt),
        in_specs=[col(0), col(1), col(2), col(3),
                  pl.BlockSpec((1, 1, A_DK), lambda b, h, t: (h, 0, 0)),
                  pl.BlockSpec((1, 1, A_DK), lambda b, h, t: (0, 0, 0))],
        out_specs=pl.BlockSpec((1, tt, A_DK), lambda b, h, t: (b, t, h)),
        out_shape=jax.ShapeDtypeStruct((Bn, T, H * A_DK), BF16),
        scratch_shapes=[pltpu.VMEM((A_DK, A_DK), F32), pltpu.VMEM((tt, A_DK), F32),
                        pltpu.VMEM((tt, A_DK), F32), pltpu.VMEM((tt, A_DK), F32)],
        compiler_params=pltpu.CompilerParams(
            dimension_semantics=("parallel", "parallel", "arbitrary"), vmem_limit_bytes=VMEM_LIMIT),
        name="hgrn2",
    )(pa, pa, pa, pa, lb, ng)


def _sb_kernel(q_ref, k_ref, v_ref, o_ref):
    tq = q_ref.shape[1]
    i = pl.program_id(2)
    q = q_ref[0]
    lane = lax.broadcasted_iota(jnp.int32, (tq, LANES), 1)
    r = lax.broadcasted_iota(jnp.int32, (tq, tq), 0)
    c = lax.broadcasted_iota(jnp.int32, (tq, tq), 1)
    suffix = (r >= c).astype(BF16)
    strict = c < r

    def tile(qm, kb, vb, run, diag):
        z = _dot_nt(qm, kb)
        sp = jnp.maximum(z, 0.0) + jnp.log(1.0 + jnp.exp(-jnp.abs(z)))
        if diag:
            sp = jnp.where(strict, sp, 0.0)
        hi, lo = _split_bf16(sp)
        cs = _dot(hi, suffix) + _dot(lo, suffix) + run
        a = jnp.exp(z - cs)
        if diag:
            a = jnp.where(strict, a, 0.0)
        return _dot(a.astype(BF16), vb), cs[:, 0:1]

    qms = []
    for hh in range(2):
        in_head = (lane >= hh * HEAD_DIM) & (lane < (hh + 1) * HEAD_DIM)
        qms.append(jnp.where(in_head, q, jnp.zeros_like(q)) * (HEAD_DIM ** -0.5))
    d0 = pl.ds(pl.multiple_of(i * tq, tq), tq)
    kb, vb = k_ref[0, d0, :], v_ref[0, d0, :]
    zero = jnp.zeros((tq, 1), F32)
    acc0, run0 = tile(qms[0], kb, vb, zero, True)
    acc1, run1 = tile(qms[1], kb, vb, zero, True)

    def cond(carry):
        s, _, run0, _, run1 = carry
        return (s < i) & (jnp.minimum(jnp.min(run0), jnp.min(run1)) <= SB_EXIT)

    def body(carry):
        s, acc0, run0, acc1, run1 = carry
        sl = pl.ds(pl.multiple_of((i - 1 - s) * tq, tq), tq)
        kb, vb = k_ref[0, sl, :], v_ref[0, sl, :]
        pv0, run0 = tile(qms[0], kb, vb, run0, False)
        pv1, run1 = tile(qms[1], kb, vb, run1, False)
        return s + 1, acc0 + pv0, run0, acc1 + pv1, run1

    _, acc0, _, acc1, _ = lax.while_loop(cond, body, (jnp.int32(0), acc0, run0, acc1, run1))
    o_ref[0] = jnp.where(lane < HEAD_DIM, acc0, acc1).astype(o_ref.dtype)


def _stick_breaking(pbcd, tq):
    Bn, T, _ = pbcd.shape
    npair = B_HEADS // 2
    return pl.pallas_call(
        _sb_kernel,
        grid=(Bn, npair, T // tq),
        in_specs=[pl.BlockSpec((1, tq, LANES), lambda b, p, i: (b, i, _BQ + p)),
                  pl.BlockSpec((1, T, LANES), lambda b, p, i: (b, 0, _BK + p)),
                  pl.BlockSpec((1, T, LANES), lambda b, p, i: (b, 0, _BV + p))],
        out_specs=pl.BlockSpec((1, tq, LANES), lambda b, p, i: (b, i, p)),
        out_shape=jax.ShapeDtypeStruct((Bn, T, npair * LANES), BF16),
        compiler_params=pltpu.CompilerParams(
            dimension_semantics=("parallel", "parallel", "arbitrary"), vmem_limit_bytes=VMEM_LIMIT),
        name="stick_breaking",
    )(pbcd, pbcd, pbcd)


def _win_kernel(q_ref, kc_ref, kp_ref, vc_ref, vp_ref, bias_ref, sink_ref, *out_refs, dil):
    blk = WIN_BLOCK
    i = pl.program_id(2)
    lane = lax.broadcasted_iota(jnp.int32, (blk, LANES), 1)
    scale = HEAD_DIM ** -0.5

    def residue_class(c):
        rows = pl.ds(c, blk, stride=dil) if dil > 1 else pl.ds(0, blk)
        q, kc, kp, vc, vp = (r[0, rows, :].astype(BF16) for r in (q_ref, kc_ref, kp_ref, vc_ref, vp_ref))
        os, ls = [], []
        for hh in range(2):
            in_head = (lane >= hh * HEAD_DIM) & (lane < (hh + 1) * HEAD_DIM)
            qm = jnp.where(in_head, q, jnp.zeros_like(q))
            bias = bias_ref[hh]
            lp = _dot_nt(qm, kp) * scale + bias[:, :blk]
            lp = jnp.where(i > 0, lp, -jnp.inf)
            lc = _dot_nt(qm, kc) * scale + bias[:, blk:]
            sink = sink_ref[hh:hh + 1, 0:1]
            m = jnp.maximum(jnp.max(lp, axis=-1, keepdims=True), jnp.max(lc, axis=-1, keepdims=True))
            m = jnp.maximum(m, sink)
            pp = jnp.exp(lp - m)
            pc = jnp.exp(lc - m)
            den = (jnp.sum(pp, axis=-1, keepdims=True) + jnp.sum(pc, axis=-1, keepdims=True)
                   + jnp.exp(sink - m))
            os.append((_dot(pp.astype(BF16), vp) + _dot(pc.astype(BF16), vc)) / den)
            ls.append(m + jnp.log(den))
        out_refs[0][0, rows, :] = jnp.where(lane < HEAD_DIM, os[0], os[1]).astype(out_refs[0].dtype)
        if len(out_refs) > 1:
            out_refs[1][0, rows, :] = jnp.where(lane < HEAD_DIM, ls[0], ls[1])

    if dil == 1:
        residue_class(0)
    else:
        def body(c, carry):
            residue_class(c)
            return carry
        lax.fori_loop(0, dil, body, 0)


def _window_attn(slab, bias, sinks, dil, qcol, kcol, vcol, npair, kv_shared, with_lse, out_dtype):
    Bn, T, _ = slab.shape
    tb = WIN_BLOCK * dil

    def qmap(b, p, i):
        return (b, i, qcol + p)

    def kvmap(col, prev):
        def f(b, p, i):
            return (b, jnp.maximum(i - 1, 0) if prev else i, col + (0 if kv_shared else p))
        return f

    blk = (1, tb, LANES)
    out_spec = pl.BlockSpec(blk, lambda b, p, i: (b, i, p))
    out_sds = jax.ShapeDtypeStruct((Bn, T, npair * LANES), out_dtype)
    lse_sds = jax.ShapeDtypeStruct((Bn, T, npair * LANES), F32)
    return pl.pallas_call(
        functools.partial(_win_kernel, dil=dil),
        grid=(Bn, npair, T // tb),
        in_specs=[pl.BlockSpec(blk, qmap),
                  pl.BlockSpec(blk, kvmap(kcol, False)), pl.BlockSpec(blk, kvmap(kcol, True)),
                  pl.BlockSpec(blk, kvmap(vcol, False)), pl.BlockSpec(blk, kvmap(vcol, True)),
                  pl.BlockSpec((None, 2, WIN_BLOCK, 2 * WIN_BLOCK), lambda b, p, i: (p, 0, 0, 0)),
                  pl.BlockSpec((None, 2, LANES), lambda b, p, i: (p, 0, 0))],
        out_specs=[out_spec, out_spec] if with_lse else [out_spec],
        out_shape=[out_sds, lse_sds] if with_lse else [out_sds],
        compiler_params=pltpu.CompilerParams(
            dimension_semantics=("parallel", "parallel", "arbitrary"), vmem_limit_bytes=VMEM_LIMIT),
        name="window_attn",
    )(slab, slab, slab, slab, slab, bias, sinks)


def _combine_kernel(o0, o1, o2, l0, l1, l2, out_ref):
    a, b, c = l0[...], l1[...], l2[...]
    m = jnp.maximum(jnp.maximum(a, b), c)
    ea, eb, ec = jnp.exp(a - m), jnp.exp(b - m), jnp.exp(c - m)
    out_ref[...] = ((ea * o0[...] + eb * o1[...] + ec * o2[...]) / (ea + eb + ec)).astype(out_ref.dtype)


def _combine(outs, lses, tm):
    M, W = outs[0].shape
    spec = pl.BlockSpec((tm, W), lambda i: (i, 0))
    return pl.pallas_call(
        _combine_kernel,
        grid=(M // tm,),
        in_specs=[spec] * 6,
        out_specs=spec,
        out_shape=jax.ShapeDtypeStruct((M, W), BF16),
        compiler_params=pltpu.CompilerParams(dimension_semantics=("parallel",)),
        name="group_combine",
    )(*outs, *lses)


def _merge_kernel(x_ref, g_ref, ya_ref, yb_ref, yc_ref, yd_ref,
                  ga_ref, gb_ref, gc_ref, gd_ref, wa_ref, wb_ref, wc_ref, wd_ref, wo_ref,
                  o_ref, u_ref, acc_ref):
    j = pl.program_id(1)

    @pl.when(j == 0)
    def _():
        u_ref[...] = _rms(x_ref[...], g_ref[0:1, :]).astype(BF16)
        acc_ref[...] = jnp.zeros_like(acc_ref)

    u = u_ref[...]
    merged = None
    for y_ref, gate_ref, w_ref in ((ya_ref, ga_ref, wa_ref), (yb_ref, gb_ref, wb_ref),
                                   (yc_ref, gc_ref, wc_ref), (yd_ref, gd_ref, wd_ref)):
        term = _sigmoid(_dot(u, gate_ref[...])) * _dot(y_ref[...], w_ref[...])
        merged = term if merged is None else merged + term
    acc_ref[...] += _dot(merged.astype(BF16), wo_ref[...])

    @pl.when(j == pl.num_programs(1) - 1)
    def _():
        o_ref[...] = x_ref[...] + _rms(acc_ref[...], g_ref[1:2, :])


def _merge(h, norm, ys, w_in, gate_col0, w_branch, w_out, layer, tm, tn):
    M, D = h.shape
    gcb = gate_col0 // tn
    dcb = D // tn

    def yspec(w):
        return pl.BlockSpec((tm, w), lambda i, j: (i, 0))

    def gspec(br):
        return pl.BlockSpec((None, D, tn), lambda i, j: (layer, 0, gcb + br * dcb + j))

    def wspec(rows, rb):
        return pl.BlockSpec((None, rows, tn), lambda i, j: (layer, rb, j))

    wa, wb, wc, wd = (y.shape[1] for y in ys)
    return pl.pallas_call(
        _merge_kernel,
        grid=(M // tm, dcb),
        in_specs=[pl.BlockSpec((tm, D), lambda i, j: (i, 0)),
                  pl.BlockSpec((None, 2, D), lambda i, j: (layer, 0, 0)),
                  yspec(wa), yspec(wb), yspec(wc), yspec(wd),
                  gspec(0), gspec(1), gspec(2), gspec(3),
                  wspec(wa, 0), wspec(wb, wa // wb), wspec(wc, (wa + wb + wd) // wc),
                  wspec(wd, (wa + wb) // wd),
                  pl.BlockSpec((None, tn, D), lambda i, j: (layer, j, 0))],
        out_specs=pl.BlockSpec((tm, D), lambda i, j: (i, 0)),
        out_shape=jax.ShapeDtypeStruct((M, D), F32),
        scratch_shapes=[pltpu.VMEM((tm, D), BF16), pltpu.VMEM((tm, D), F32)],
        compiler_params=pltpu.CompilerParams(
            dimension_semantics=("parallel", "arbitrary"), vmem_limit_bytes=VMEM_LIMIT),
        name="gated_merge",
    )(h, norm, *ys, w_in, w_in, w_in, w_in, w_branch, w_branch, w_branch, w_branch, w_out)


def _rel_bucket(dist):
    max_exact = REL_BUCKETS // 2
    d = jnp.maximum(dist, 1).astype(F32)
    large = max_exact + (jnp.log(d / max_exact) / math.log(REL_MAX_DIST / max_exact)
                         * (REL_BUCKETS - max_exact)).astype(jnp.int32)
    large = jnp.minimum(large, REL_BUCKETS - 1)
    return jnp.where(dist < max_exact, dist, large)


def _window_bias(rel_bias, heads, dilation, max_dist):
    row = jnp.arange(WIN_BLOCK)[:, None]
    col = jnp.arange(2 * WIN_BLOCK)[None, :]
    dist = row + WIN_BLOCK - col
    valid = (dist >= 0) & (dist <= max_dist)
    tab = rel_bias[_rel_bucket(jnp.maximum(dist, 0) * dilation)]
    tab = jnp.transpose(tab[:, :, jnp.array(heads)], (2, 0, 1)).astype(F32)
    tab = jnp.where(valid[None], tab, -jnp.inf)
    return tab.reshape(len(heads) // 2, 2, WIN_BLOCK, 2 * WIN_BLOCK)


def _trunk(x, ffn1_norm, ffn1_w13, ffn1_w2, mix_norm, w_in, hgrn_lb_logits, hgrn_out_norm,
           attn_sinks, w_branch, w_out, ffn2_norm, ffn2_w13, ffn2_w2, rel_bias, cfg):
    Bn, T, D = x.shape
    depth = w_in.shape[0]
    M = Bn * T
    a_w = A_HEADS * A_DK
    a_cols = 4 * a_w
    b_cols = 3 * B_HEADS * HEAD_DIM
    c_cols = 3 * C_HEADS * HEAD_DIM
    ac_cols = AC_BLOCKS * LANES
    bd_cols = BD_BLOCKS * LANES
    mix_in = ac_cols + bd_cols

    lb_sm = jax.nn.softmax(hgrn_lb_logits.astype(F32), axis=0)
    lb_all = (jnp.cumsum(lb_sm, axis=0) - lb_sm[0:1]).reshape(depth, A_HEADS, 1, A_DK)

    b0, c0 = a_cols, a_cols + b_cols
    dq0 = c0 + c_cols
    dk0 = dq0 + D_HEADS * HEAD_DIM
    perm = jnp.array([h * HEAD_DIM + d for h in D_HEAD_PERM for d in range(HEAD_DIM)])
    w_in_b = jnp.concatenate(
        [w_in[:, :, :a_cols], w_in[:, :, c0:dq0], w_in[:, :, b0:c0], w_in[:, :, dq0:dk0][:, :, perm],
         w_in[:, :, dk0:]], axis=-1).astype(BF16)
    c_w = C_HPG * HEAD_DIM
    d0 = 2 * a_w + c_w
    w_br_b = jnp.concatenate(
        [w_branch[:, :2 * a_w], w_branch[:, d0:][:, perm], w_branch[:, 2 * a_w:d0]], axis=1).astype(BF16)
    w_out_b = w_out.astype(BF16)
    f1_w13, f1_w2 = ffn1_w13.astype(BF16), ffn1_w2.astype(BF16)
    f2_w13, f2_w2 = ffn2_w13.astype(BF16), ffn2_w2.astype(BF16)

    c_bias = [_window_bias(rel_bias, list(range(g * C_HPG, (g + 1) * C_HPG)), dil, win // dil)
              for g, (win, dil) in enumerate(C_PATTERNS)]
    d_bias = _window_bias(rel_bias, [C_HEADS + h for h in D_HEAD_PERM], 1, D_WINDOW - 1)
    no_sink = jnp.full((C_HPG // 2, 2, LANES), -jnp.inf, F32)

    h = x.reshape(M, D)
    for l in range(depth):
        h = _ffn(h, ffn1_norm, f1_w13, f1_w2, l, cfg["ffn_tm"], cfg["ffn_tf"])

        pac = _proj(h, mix_norm, w_in_b, l, 0, ac_cols, F32, cfg["proj_tm"], cfg["proj_tn"])
        pbd = _proj(h, mix_norm, w_in_b, l, ac_cols, bd_cols, BF16, cfg["proj_tm"], cfg["proj_tn"])
        pac = pac.reshape(Bn, T, ac_cols)
        pbd = pbd.reshape(Bn, T, bd_cols)

        ya = _hgrn(pac, lb_all[l], hgrn_out_norm[l].reshape(1, 1, A_DK).astype(F32), cfg["hgrn_tt"])
        yb = _stick_breaking(pbd, cfg["sb_tq"])
        c_out, c_lse = [], []
        for g, (win, dil) in enumerate(C_PATTERNS):
            o, lse = _window_attn(pac, c_bias[g], no_sink, dil, _CQ + 2 * g, _CK + 2 * g, _CV + 2 * g,
                                  C_HPG // 2, False, True, F32)
            c_out.append(o.reshape(M, c_w))
            c_lse.append(lse.reshape(M, c_w))
        yc = _combine(c_out, c_lse, cfg["comb_tm"])
        sinks = attn_sinks[l].astype(F32)[jnp.array(D_HEAD_PERM)].reshape(D_HEADS // 2, 2, 1)
        sinks = jnp.broadcast_to(sinks, (D_HEADS // 2, 2, LANES))
        (yd,) = _window_attn(pbd, d_bias, sinks, 1, _DQ, _DK, _DV, D_HEADS // 2, True, False, BF16)

        ys = (ya.reshape(M, a_w), yb.reshape(M, a_w), yc, yd.reshape(M, a_w))
        h = _merge(h, mix_norm, ys, w_in_b, mix_in, w_br_b, w_out_b, l, cfg["merge_tm"], cfg["merge_tn"])

        h = _ffn(h, ffn2_norm, f2_w13, f2_w2, l, cfg["ffn_tm"], cfg["ffn_tf"])
    return h.reshape(Bn, T, D)


_CFG = dict(ffn_tm=512, ffn_tf=512, proj_tm=1024, proj_tn=256, hgrn_tt=512, sb_tq=256,
            comb_tm=1024, merge_tm=512, merge_tn=256)


def kernel(x, ffn1_norm, ffn1_w13, ffn1_w2, mix_norm, w_in, hgrn_lb_logits, hgrn_out_norm, attn_sinks,
           w_branch, w_out, ffn2_norm, ffn2_w13, ffn2_w2, rel_bias):
    return _trunk(x, ffn1_norm, ffn1_w13, ffn1_w2, mix_norm, w_in, hgrn_lb_logits, hgrn_out_norm,
                  attn_sinks, w_branch, w_out, ffn2_norm, ffn2_w13, ffn2_w2, rel_bias, _CFG)
```

```python
import functools
import math

import jax
import jax.numpy as jnp
from jax import lax
from jax.experimental import pallas as pl
from jax.experimental.pallas import tpu as pltpu

F32 = jnp.float32
BF16 = jnp.bfloat16

EPS = 1e-6
HEAD_DIM = 64
LANES = 128
A_HEADS = 4
A_DK = 128
B_HEADS = 8
C_PATTERNS = ((128, 1), (512, 4), (2048, 16))
C_HPG = 4
C_HEADS = len(C_PATTERNS) * C_HPG
D_HEADS = 8
D_KV_HEADS = 2
D_WINDOW = 128
WIN_BLOCK = 128
REL_BUCKETS = 32
REL_MAX_DIST = 2048
A_SUB = 16
VMEM_LIMIT = 56 * 1024 * 1024

_CQ, _CK, _CV = 16, 20, 24
AC_BLOCKS = 28
_BQ, _BK, _BV = 0, 4, 8
_C0Q, _C0K, _C0V = 12, 14, 16
_DQ, _DK, _DV = 18, 22, 23
BD_BLOCKS = 24
WIN_UNROLL = 4
D_HEAD_PERM = (0, 4, 1, 5, 2, 6, 3, 7)
SB_EXIT = 100.0


def _rms(x, g):
    ms = jnp.mean(x * x, axis=-1, keepdims=True)
    return x * lax.rsqrt(ms + EPS) * g


def _sigmoid(x):
    return 1.0 / (1.0 + jnp.exp(-x))


def _dot(a, b):
    return jnp.dot(a, b, preferred_element_type=F32)


def _dot_nt(a, b):
    return lax.dot_general(a, b, (((1,), (1,)), ((), ())), preferred_element_type=F32)


def _dot_tn(a, b):
    return lax.dot_general(a, b, (((0,), (0,)), ((), ())), preferred_element_type=F32)


def _split_bf16(x):
    hi = x.astype(BF16)
    lo = (x - hi.astype(F32)).astype(BF16)
    return hi, lo


def _ffn_kernel(x_ref, g_ref, w1_ref, w3_ref, w2_ref, o_ref, xn_ref, acc_ref):
    j = pl.program_id(1)

    @pl.when(j == 0)
    def _():
        xn_ref[...] = _rms(x_ref[...], g_ref[0:1, :]).astype(BF16)
        acc_ref[...] = jnp.zeros_like(acc_ref)

    xn = xn_ref[...]
    gate = _dot(xn, w1_ref[...])
    up = _dot(xn, w3_ref[...])
    act = (gate * _sigmoid(gate) * up).astype(BF16)
    acc_ref[...] += _dot(act, w2_ref[...])

    @pl.when(j == pl.num_programs(1) - 1)
    def _():
        o_ref[...] = x_ref[...] + 0.5 * _rms(acc_ref[...], g_ref[1:2, :])


def _ffn(h, norm, w13, w2, layer, tm, tf):
    M, D = h.shape
    F = w2.shape[1]
    nf = F // tf
    return pl.pallas_call(
        _ffn_kernel,
        grid=(M // tm, nf),
        in_specs=[
            pl.BlockSpec((tm, D), lambda i, j: (i, 0)),
            pl.BlockSpec((None, 2, D), lambda i, j: (layer, 0, 0)),
            pl.BlockSpec((None, D, tf), lambda i, j: (layer, 0, j)),
            pl.BlockSpec((None, D, tf), lambda i, j: (layer, 0, j + nf)),
            pl.BlockSpec((None, tf, D), lambda i, j: (layer, j, 0)),
        ],
        out_specs=pl.BlockSpec((tm, D), lambda i, j: (i, 0)),
        out_shape=jax.ShapeDtypeStruct((M, D), F32),
        scratch_shapes=[pltpu.VMEM((tm, D), BF16), pltpu.VMEM((tm, D), F32)],
        compiler_params=pltpu.CompilerParams(
            dimension_semantics=("parallel", "arbitrary"), vmem_limit_bytes=VMEM_LIMIT),
        name="ffn",
    )(h, norm, w13, w13, w2)


def _proj_kernel(x_ref, g_ref, w_ref, o_ref, xn_ref):
    @pl.when(pl.program_id(1) == 0)
    def _():
        xn_ref[...] = _rms(x_ref[...], g_ref[0:1, :]).astype(BF16)

    o_ref[...] = _dot(xn_ref[...], w_ref[...]).astype(o_ref.dtype)


def _proj(h, norm, w_in, layer, col0, ncols, out_dtype, tm, tn):
    M, D = h.shape
    cb0 = col0 // tn
    return pl.pallas_call(
        _proj_kernel,
        grid=(M // tm, ncols // tn),
        in_specs=[
            pl.BlockSpec((tm, D), lambda i, j: (i, 0)),
            pl.BlockSpec((None, 2, D), lambda i, j: (layer, 0, 0)),
            pl.BlockSpec((None, D, tn), lambda i, j: (layer, 0, cb0 + j)),
        ],
        out_specs=pl.BlockSpec((tm, tn), lambda i, j: (i, j)),
        out_shape=jax.ShapeDtypeStruct((M, ncols), out_dtype),
        scratch_shapes=[pltpu.VMEM((tm, D), BF16)],
        compiler_params=pltpu.CompilerParams(
            dimension_semantics=("parallel", "arbitrary"), vmem_limit_bytes=VMEM_LIMIT),
        name="proj",
    )(h, norm, w_in)


def _hgrn_kernel(q_ref, f_ref, i_ref, g_ref, lb_ref, ng_ref, o_ref, st_ref, kv_ref, sall_ref):
    tt = q_ref.shape[1]
    sub = A_SUB
    nb = tt // sub
    per = LANES // sub
    shape3 = (nb, sub, LANES)

    @pl.when(pl.program_id(2) == 0)
    def _():
        st_ref[...] = jnp.zeros_like(st_ref)

    lb = lb_ref[0]
    f = lb + (1.0 - lb) * _sigmoid(f_ref[0])
    k = 1.0 - f
    gl = jnp.log(f)
    r = lax.broadcasted_iota(jnp.int32, (LANES, LANES), 0)
    c = lax.broadcasted_iota(jnp.int32, (LANES, LANES), 1)
    tril = ((r // sub == c // sub) & (c <= r)).astype(BF16)
    cums = []
    for a in range(tt // LANES):
        hi, lo = _split_bf16(gl[a * LANES:(a + 1) * LANES])
        cums.append(_dot(tril, hi) + _dot(tril, lo))
    b = jnp.concatenate(cums, axis=0)
    q = q_ref[0]
    v = i_ref[0]
    b3, q3, k3, v3 = (x.reshape(shape3) for x in (b, q, k, v))

    ones = jnp.ones((LANES, LANES), BF16)
    rows = lax.broadcasted_iota(jnp.int32, (1, sub, LANES), 1)
    o3 = jnp.zeros(shape3, F32)
    for s in range(sub):
        e = jnp.exp(b3 - b3[:, s:s + 1, :])
        w = jnp.where(rows >= s, q3 * e * k3[:, s:s + 1, :], 0.0).astype(BF16)
        o3 = o3 + _dot(w.reshape(tt, LANES), ones).reshape(shape3) * v3[:, s:s + 1, :]

    b_last = b3[:, sub - 1:sub, :]
    kd = (k3 * jnp.exp(b_last - b3)).reshape(tt, LANES)
    dec = jnp.exp(b_last)
    for a in range(tt // LANES):
        v_t = v[a * LANES:(a + 1) * LANES].T.astype(BF16)
        kd_a = kd[a * LANES:(a + 1) * LANES]
        rhs = jnp.concatenate([jnp.where(r // sub == j, kd_a, 0.0).astype(BF16) for j in range(per)], axis=1)
        kv = _dot(v_t, rhs)
        for j in range(per):
            kv_ref[a * per + j] = kv[:, j * LANES:(j + 1) * LANES]

    st = st_ref[...]
    for n in range(nb):
        sall_ref[n] = st.astype(BF16)
        st = dec[n] * st + kv_ref[n]
    st_ref[...] = st

    qe = (q * jnp.exp(b)).astype(BF16)
    inter = []
    for a in range(tt // LANES):
        stack = sall_ref[a * per:(a + 1) * per].reshape(per * LANES, LANES)
        p = _dot_nt(qe[a * LANES:(a + 1) * LANES], stack)
        inter += [p[j * sub:(j + 1) * sub, j * LANES:(j + 1) * LANES] for j in range(per)]
    o = o3.reshape(tt, LANES) + jnp.concatenate(inter, axis=0)

    g = g_ref[0]
    o_ref[0] = (_rms(o, ng_ref[0]) * (g * _sigmoid(g))).astype(o_ref.dtype)


def _hgrn(pa, lb, ng, tt):
    Bn, T, _ = pa.shape
    H = A_HEADS

    def col(off):
        return pl.BlockSpec((1, tt, A_DK), lambda b, h, t: (b, t, off * H + h))

    return pl.pallas_call(
        _hgrn_kernel,
        grid=(Bn, H, T // tt),
        in_specs=[col(0), col(1), col(2), col(3),
                  pl.BlockSpec((1, 1, A_DK), lambda b, h, t: (h, 0, 0)),
                  pl.BlockSpec((1, 1, A_DK), lambda b, h, t: (0, 0, 0))],
        out_specs=pl.BlockSpec((1, tt, A_DK), lambda b, h, t: (b, t, h)),
        out_shape=jax.ShapeDtypeStruct((Bn, T, H * A_DK), BF16),
        scratch_shapes=[pltpu.VMEM((A_DK, A_DK), F32), pltpu.VMEM((tt // A_SUB, A_DK, A_DK), F32),
                        pltpu.VMEM((tt // A_SUB, A_DK, A_DK), BF16)],
        compiler_params=pltpu.CompilerParams(
            dimension_semantics=("parallel", "parallel", "arbitrary"), vmem_limit_bytes=VMEM_LIMIT),
        name="hgrn2",
    )(pa, pa, pa, pa, lb, ng)


def _sb_kernel(q_ref, k_ref, v_ref, o_ref):
    tq = q_ref.shape[1]
    i = pl.program_id(2)
    q = q_ref[0]
    lane = lax.broadcasted_iota(jnp.int32, (tq, LANES), 1)
    r = lax.broadcasted_iota(jnp.int32, (tq, tq), 0)
    c = lax.broadcasted_iota(jnp.int32, (tq, tq), 1)
    suffix = (r >= c).astype(BF16)
    strict = c < r

    def tile(qm, kb, vb, run, diag):
        z = _dot_nt(qm, kb)
        sp = jnp.maximum(z, 0.0) + jnp.log(1.0 + jnp.exp(-jnp.abs(z)))
        if diag:
            sp = jnp.where(strict, sp, 0.0)
        hi, lo = _split_bf16(sp)
        cs = _dot(hi, suffix) + _dot(lo, suffix) + run
        a = jnp.exp(z - cs)
        if diag:
            a = jnp.where(strict, a, 0.0)
        return _dot(a.astype(BF16), vb), cs[:, 0:1]

    qms = []
    for hh in range(2):
        in_head = (lane >= hh * HEAD_DIM) & (lane < (hh + 1) * HEAD_DIM)
        qms.append(jnp.where(in_head, q, jnp.zeros_like(q)) * (HEAD_DIM ** -0.5))
    d0 = pl.ds(pl.multiple_of(i * tq, tq), tq)
    kb, vb = k_ref[0, d0, :], v_ref[0, d0, :]
    zero = jnp.zeros((tq, 1), F32)
    acc0, run0 = tile(qms[0], kb, vb, zero, True)
    acc1, run1 = tile(qms[1], kb, vb, zero, True)

    def cond(carry):
        s, _, run0, _, run1 = carry
        return (s < i) & (jnp.minimum(jnp.min(run0), jnp.min(run1)) <= SB_EXIT)

    def body(carry):
        s, acc0, run0, acc1, run1 = carry
        sl = pl.ds(pl.multiple_of((i - 1 - s) * tq, tq), tq)
        kb, vb = k_ref[0, sl, :], v_ref[0, sl, :]
        pv0, run0 = tile(qms[0], kb, vb, run0, False)
        pv1, run1 = tile(qms[1], kb, vb, run1, False)
        return s + 1, acc0 + pv0, run0, acc1 + pv1, run1

    _, acc0, _, acc1, _ = lax.while_loop(cond, body, (jnp.int32(0), acc0, run0, acc1, run1))
    o_ref[0] = jnp.where(lane < HEAD_DIM, acc0, acc1).astype(o_ref.dtype)


def _stick_breaking(pbcd, tq):
    Bn, T, _ = pbcd.shape
    npair = B_HEADS // 2
    return pl.pallas_call(
        _sb_kernel,
        grid=(Bn, npair, T // tq),
        in_specs=[pl.BlockSpec((1, tq, LANES), lambda b, p, i: (b, i, _BQ + p)),
                  pl.BlockSpec((1, T, LANES), lambda b, p, i: (b, 0, _BK + p)),
                  pl.BlockSpec((1, T, LANES), lambda b, p, i: (b, 0, _BV + p))],
        out_specs=pl.BlockSpec((1, tq, LANES), lambda b, p, i: (b, i, p)),
        out_shape=jax.ShapeDtypeStruct((Bn, T, npair * LANES), BF16),
        compiler_params=pltpu.CompilerParams(
            dimension_semantics=("parallel", "parallel", "arbitrary"), vmem_limit_bytes=VMEM_LIMIT),
        name="stick_breaking",
    )(pbcd, pbcd, pbcd)


def _win_kernel(q_ref, k_ref, kp_ref, v_ref, vp_ref, bias_ref, sink_ref, *out_refs, dil, nsub):
    blk = WIN_BLOCK
    span = blk * dil
    i = pl.program_id(2)
    lane = lax.broadcasted_iota(jnp.int32, (blk, LANES), 1)
    in_h0 = lane < HEAD_DIM
    prev_col = lax.broadcasted_iota(jnp.int32, (2 * blk, 2 * blk), 1) < blk
    ones = jnp.ones((2 * blk, LANES), BF16)
    sink2 = jnp.concatenate([jnp.broadcast_to(sink_ref[hh:hh + 1, 0:1], (blk, 1)) for hh in range(2)], axis=0)
    bias = bias_ref[...]

    def rows_of(start):
        return pl.ds(start, blk, stride=dil) if dil > 1 else pl.ds(start, blk)

    def residue_class(s, c):
        cur = rows_of(s * span + c)
        q = q_ref[0, cur, :].astype(BF16)
        if s == 0:
            kp, vp = kp_ref[0, rows_of(c), :], vp_ref[0, rows_of(c), :]
        else:
            prev = rows_of((s - 1) * span + c)
            kp, vp = k_ref[0, prev, :], v_ref[0, prev, :]
        k2 = jnp.concatenate([kp.astype(BF16), k_ref[0, cur, :].astype(BF16)], axis=0)
        v2 = jnp.concatenate([vp.astype(BF16), v_ref[0, cur, :].astype(BF16)], axis=0)
        zero = jnp.zeros_like(q)
        q2 = jnp.concatenate([jnp.where(in_h0, q, zero), jnp.where(in_h0, zero, q)], axis=0)
        logits = _dot_nt(q2 * (HEAD_DIM ** -0.5), k2) + bias
        if s == 0:
            logits = jnp.where(jnp.logical_and(i == 0, prev_col), -jnp.inf, logits)
        m = jnp.max(jnp.maximum(logits[:, :blk], logits[:, blk:]), axis=-1, keepdims=True)
        m = jnp.maximum(m, sink2)
        p = jnp.exp(logits - m).astype(BF16)
        den = _dot(p, ones) + jnp.exp(sink2 - m)
        o2 = _dot(p, v2) / den
        out_refs[0][0, cur, :] = jnp.where(in_h0, o2[:blk], o2[blk:]).astype(out_refs[0].dtype)
        if len(out_refs) > 1:
            lse2 = m + jnp.log(den)
            out_refs[1][0, cur, :] = jnp.where(in_h0, lse2[:blk], lse2[blk:])

    for s in range(nsub):
        if dil <= WIN_UNROLL:
            for c in range(dil):
                residue_class(s, c)
        else:
            def body(c, carry, s=s):
                residue_class(s, c)
                return carry
            lax.fori_loop(0, dil, body, 0, unroll=WIN_UNROLL)


def _window_attn(slab, bias, sinks, dil, qcol, kcol, vcol, npair, kv_shared, with_lse, out_dtype):
    Bn, T, _ = slab.shape
    span = WIN_BLOCK * dil
    nsub = max(1, WIN_UNROLL // dil)
    tb = span * nsub

    def qmap(b, p, i):
        return (b, i, qcol + p)

    def kvmap(col, prev):
        def f(b, p, i):
            return (b, jnp.maximum(i * nsub - 1, 0) if prev else i, col + (0 if kv_shared else p))
        return f

    blk = (1, tb, LANES)
    pblk = (1, span, LANES)
    out_spec = pl.BlockSpec(blk, lambda b, p, i: (b, i, p))
    out_sds = jax.ShapeDtypeStruct((Bn, T, npair * LANES), out_dtype)
    lse_sds = jax.ShapeDtypeStruct((Bn, T, npair * LANES), F32)
    return pl.pallas_call(
        functools.partial(_win_kernel, dil=dil, nsub=nsub),
        grid=(Bn, npair, T // tb),
        in_specs=[pl.BlockSpec(blk, qmap),
                  pl.BlockSpec(blk, kvmap(kcol, False)), pl.BlockSpec(pblk, kvmap(kcol, True)),
                  pl.BlockSpec(blk, kvmap(vcol, False)), pl.BlockSpec(pblk, kvmap(vcol, True)),
                  pl.BlockSpec((None, 2 * WIN_BLOCK, 2 * WIN_BLOCK), lambda b, p, i: (p, 0, 0)),
                  pl.BlockSpec((None, 2, LANES), lambda b, p, i: (p, 0, 0))],
        out_specs=[out_spec, out_spec] if with_lse else [out_spec],
        out_shape=[out_sds, lse_sds] if with_lse else [out_sds],
        compiler_params=pltpu.CompilerParams(
            dimension_semantics=("parallel", "parallel", "arbitrary"), vmem_limit_bytes=VMEM_LIMIT),
        name="window_attn",
    )(slab, slab, slab, slab, slab, bias, sinks)


def _combine_kernel(o0, o1, o2, l0, l1, l2, out_ref):
    a, b, c = l0[...], l1[...], l2[...]
    m = jnp.maximum(jnp.maximum(a, b), c)
    ea, eb, ec = jnp.exp(a - m), jnp.exp(b - m), jnp.exp(c - m)
    out_ref[...] = ((ea * o0[...] + eb * o1[...] + ec * o2[...]) / (ea + eb + ec)).astype(out_ref.dtype)


def _combine(outs, lses, tm):
    M, W = outs[0].shape
    spec = pl.BlockSpec((tm, W), lambda i: (i, 0))
    return pl.pallas_call(
        _combine_kernel,
        grid=(M // tm,),
        in_specs=[spec] * 6,
        out_specs=spec,
        out_shape=jax.ShapeDtypeStruct((M, W), BF16),
        compiler_params=pltpu.CompilerParams(dimension_semantics=("parallel",)),
        name="group_combine",
    )(*outs, *lses)


def _merge_kernel(x_ref, g_ref, ya_ref, yb_ref, yc_ref, yd_ref,
                  ga_ref, gb_ref, gc_ref, gd_ref, wa_ref, wb_ref, wc_ref, wd_ref, wo_ref,
                  o_ref, u_ref, acc_ref):
    j = pl.program_id(1)

    @pl.when(j == 0)
    def _():
        u_ref[...] = _rms(x_ref[...], g_ref[0:1, :]).astype(BF16)
        acc_ref[...] = jnp.zeros_like(acc_ref)

    u = u_ref[...]
    merged = None
    for y_ref, gate_ref, w_ref in ((ya_ref, ga_ref, wa_ref), (yb_ref, gb_ref, wb_ref),
                                   (yc_ref, gc_ref, wc_ref), (yd_ref, gd_ref, wd_ref)):
        term = _sigmoid(_dot(u, gate_ref[...])) * _dot(y_ref[...], w_ref[...])
        merged = term if merged is None else merged + term
    acc_ref[...] += _dot(merged.astype(BF16), wo_ref[...])

    @pl.when(j == pl.num_programs(1) - 1)
    def _():
        o_ref[...] = x_ref[...] + _rms(acc_ref[...], g_ref[1:2, :])


def _merge(h, norm, ys, w_in, gate_col0, w_branch, w_out, layer, tm, tn):
    M, D = h.shape
    gcb = gate_col0 // tn
    dcb = D // tn

    def yspec(w):
        return pl.BlockSpec((tm, w), lambda i, j: (i, 0))

    def gspec(br):
        return pl.BlockSpec((None, D, tn), lambda i, j: (layer, 0, gcb + br * dcb + j))

    def wspec(rows, rb):
        return pl.BlockSpec((None, rows, tn), lambda i, j: (layer, rb, j))

    wa, wb, wc, wd = (y.shape[1] for y in ys)
    return pl.pallas_call(
        _merge_kernel,
        grid=(M // tm, dcb),
        in_specs=[pl.BlockSpec((tm, D), lambda i, j: (i, 0)),
                  pl.BlockSpec((None, 2, D), lambda i, j: (layer, 0, 0)),
                  yspec(wa), yspec(wb), yspec(wc), yspec(wd),
                  gspec(0), gspec(1), gspec(2), gspec(3),
                  wspec(wa, 0), wspec(wb, wa // wb), wspec(wc, (wa + wb + wd) // wc),
                  wspec(wd, (wa + wb) // wd),
                  pl.BlockSpec((None, tn, D), lambda i, j: (layer, j, 0))],
        out_specs=pl.BlockSpec((tm, D), lambda i, j: (i, 0)),
        out_shape=jax.ShapeDtypeStruct((M, D), F32),
        scratch_shapes=[pltpu.VMEM((tm, D), BF16), pltpu.VMEM((tm, D), F32)],
        compiler_params=pltpu.CompilerParams(
            dimension_semantics=("parallel", "arbitrary"), vmem_limit_bytes=VMEM_LIMIT),
        name="gated_merge",
    )(h, norm, *ys, w_in, w_in, w_in, w_in, w_branch, w_branch, w_branch, w_branch, w_out)


def _rel_bucket(dist):
    max_exact = REL_BUCKETS // 2
    d = jnp.maximum(dist, 1).astype(F32)
    large = max_exact + (jnp.log(d / max_exact) / math.log(REL_MAX_DIST / max_exact)
                         * (REL_BUCKETS - max_exact)).astype(jnp.int32)
    large = jnp.minimum(large, REL_BUCKETS - 1)
    return jnp.where(dist < max_exact, dist, large)


def _window_bias(rel_bias, heads, dilation, max_dist):
    row = jnp.arange(WIN_BLOCK)[:, None]
    col = jnp.arange(2 * WIN_BLOCK)[None, :]
    dist = row + WIN_BLOCK - col
    valid = (dist >= 0) & (dist <= max_dist)
    tab = rel_bias[_rel_bucket(jnp.maximum(dist, 0) * dilation)]
    tab = jnp.transpose(tab[:, :, jnp.array(heads)], (2, 0, 1)).astype(F32)
    tab = jnp.where(valid[None], tab, -jnp.inf)
    return tab.reshape(len(heads) // 2, 2 * WIN_BLOCK, 2 * WIN_BLOCK)


def _trunk(x, ffn1_norm, ffn1_w13, ffn1_w2, mix_norm, w_in, hgrn_lb_logits, hgrn_out_norm,
           attn_sinks, w_branch, w_out, ffn2_norm, ffn2_w13, ffn2_w2, rel_bias, cfg):
    Bn, T, D = x.shape
    depth = w_in.shape[0]
    M = Bn * T
    a_w = A_HEADS * A_DK
    a_cols = 4 * a_w
    b_cols = 3 * B_HEADS * HEAD_DIM
    c_cols = 3 * C_HEADS * HEAD_DIM
    ac_cols = AC_BLOCKS * LANES
    bd_cols = BD_BLOCKS * LANES
    mix_in = ac_cols + bd_cols

    lb_sm = jax.nn.softmax(hgrn_lb_logits.astype(F32), axis=0)
    lb_all = (jnp.cumsum(lb_sm, axis=0) - lb_sm[0:1]).reshape(depth, A_HEADS, 1, A_DK)

    b0, c0 = a_cols, a_cols + b_cols
    dq0 = c0 + c_cols
    dk0 = dq0 + D_HEADS * HEAD_DIM
    perm = jnp.array([h * HEAD_DIM + d for h in D_HEAD_PERM for d in range(HEAD_DIM)])
    c_w = C_HPG * HEAD_DIM

    def c_part(part, g0, g1):
        base = c0 + part * C_HEADS * HEAD_DIM
        return w_in[:, :, base + g0 * c_w:base + g1 * c_w]

    n_grp = len(C_PATTERNS)
    w_in_b = jnp.concatenate(
        [w_in[:, :, :a_cols]] + [c_part(p, 1, n_grp) for p in range(3)]
        + [w_in[:, :, b0:c0]] + [c_part(p, 0, 1) for p in range(3)]
        + [w_in[:, :, dq0:dk0][:, :, perm], w_in[:, :, dk0:]], axis=-1).astype(BF16)
    d0 = 2 * a_w + c_w
    w_br_b = jnp.concatenate(
        [w_branch[:, :2 * a_w], w_branch[:, d0:][:, perm], w_branch[:, 2 * a_w:d0]], axis=1).astype(BF16)
    w_out_b = w_out.astype(BF16)
    f1_w13, f1_w2 = ffn1_w13.astype(BF16), ffn1_w2.astype(BF16)
    f2_w13, f2_w2 = ffn2_w13.astype(BF16), ffn2_w2.astype(BF16)

    c_bias = [_window_bias(rel_bias, list(range(g * C_HPG, (g + 1) * C_HPG)), dil, win // dil)
              for g, (win, dil) in enumerate(C_PATTERNS)]
    d_bias = _window_bias(rel_bias, [C_HEADS + h for h in D_HEAD_PERM], 1, D_WINDOW - 1)
    no_sink = jnp.full((C_HPG // 2, 2, LANES), -jnp.inf, F32)

    h = x.reshape(M, D)
    for l in range(depth):
        h = _ffn(h, ffn1_norm, f1_w13, f1_w2, l, cfg["ffn_tm"], cfg["ffn_tf"])

        pac = _proj(h, mix_norm, w_in_b, l, 0, ac_cols, F32, cfg["proj_tm"], cfg["proj_tn"])
        pbd = _proj(h, mix_norm, w_in_b, l, ac_cols, bd_cols, BF16, cfg["proj_tm"], cfg["proj_tn"])
        pac = pac.reshape(Bn, T, ac_cols)
        pbd = pbd.reshape(Bn, T, bd_cols)

        ya = _hgrn(pac, lb_all[l], hgrn_out_norm[l].reshape(1, 1, A_DK).astype(F32), cfg["hgrn_tt"])
        yb = _stick_breaking(pbd, cfg["sb_tq"])
        c_out, c_lse = [], []
        for g, (win, dil) in enumerate(C_PATTERNS):
            if dil == 1:
                slab, cols = pbd, (_C0Q, _C0K, _C0V)
            else:
                slab, cols = pac, tuple(c + 2 * (g - 1) for c in (_CQ, _CK, _CV))
            o, lse = _window_attn(slab, c_bias[g], no_sink, dil, *cols, C_HPG // 2, False, True, F32)
            c_out.append(o.reshape(M, c_w))
            c_lse.append(lse.reshape(M, c_w))
        yc = _combine(c_out, c_lse, cfg["comb_tm"])
        sinks = attn_sinks[l].astype(F32)[jnp.array(D_HEAD_PERM)].reshape(D_HEADS // 2, 2, 1)
        sinks = jnp.broadcast_to(sinks, (D_HEADS // 2, 2, LANES))
        (yd,) = _window_attn(pbd, d_bias, sinks, 1, _DQ, _DK, _DV, D_HEADS // 2, True, False, BF16)

        ys = (ya.reshape(M, a_w), yb.reshape(M, a_w), yc, yd.reshape(M, a_w))
        h = _merge(h, mix_norm, ys, w_in_b, mix_in, w_br_b, w_out_b, l, cfg["merge_tm"], cfg["merge_tn"])

        h = _ffn(h, ffn2_norm, f2_w13, f2_w2, l, cfg["ffn_tm"], cfg["ffn_tf"])
    return h.reshape(Bn, T, D)


_CFG = dict(ffn_tm=512, ffn_tf=512, proj_tm=1024, proj_tn=512, hgrn_tt=512, sb_tq=256,
            comb_tm=1024, merge_tm=512, merge_tn=256)


def kernel(x, ffn1_norm, ffn1_w13, ffn1_w2, mix_norm, w_in, hgrn_lb_logits, hgrn_out_norm, attn_sinks,
           w_branch, w_out, ffn2_norm, ffn2_w13, ffn2_w2, rel_bias):
    return _trunk(x, ffn1_norm, ffn1_w13, ffn1_w2, mix_norm, w_in, hgrn_lb_logits, hgrn_out_norm,
                  attn_sinks, w_branch, w_out, ffn2_norm, ffn2_w13, ffn2_w2, rel_bias, _CFG)
```

```python
import functools
import math

import jax
import jax.numpy as jnp
from jax import lax
from jax.experimental import pallas as pl
from jax.experimental.pallas import tpu as pltpu

F32 = jnp.float32
BF16 = jnp.bfloat16

EPS = 1e-6
HEAD_DIM = 64
LANES = 128
A_HEADS = 4
A_DK = 128
B_HEADS = 8
C_PATTERNS = ((128, 1), (512, 4), (2048, 16))
C_HPG = 4
C_HEADS = len(C_PATTERNS) * C_HPG
D_HEADS = 8
D_KV_HEADS = 2
D_WINDOW = 128
WIN_BLOCK = 128
REL_BUCKETS = 32
REL_MAX_DIST = 2048
A_SUB = 16
VMEM_LIMIT = 60 * 1024 * 1024

_CQ, _CK, _CV = 16, 20, 24
AC_BLOCKS = 28
_BQ, _BK, _BV = 0, 4, 8
_C0Q, _C0K, _C0V = 12, 14, 16
_DQ, _DK, _DV = 18, 22, 23
BD_BLOCKS = 24
WIN_UNROLL = 4
D_HEAD_PERM = (0, 4, 1, 5, 2, 6, 3, 7)
SB_EXIT = 100.0


def _rms(x, g):
    ms = jnp.mean(x * x, axis=-1, keepdims=True)
    return x * lax.rsqrt(ms + EPS) * g


def _sigmoid(x):
    return 1.0 / (1.0 + jnp.exp(-x))


def _dot(a, b):
    return jnp.dot(a, b, preferred_element_type=F32)


def _dot_nt(a, b):
    return lax.dot_general(a, b, (((1,), (1,)), ((), ())), preferred_element_type=F32)


def _dot_tn(a, b):
    return lax.dot_general(a, b, (((0,), (0,)), ((), ())), preferred_element_type=F32)


def _split_bf16(x):
    hi = x.astype(BF16)
    lo = (x - hi.astype(F32)).astype(BF16)
    return hi, lo


def _ffn_kernel(x_ref, g_ref, w1_ref, w3_ref, w2_ref, o_ref, xn_ref):
    j = pl.program_id(1)

    @pl.when(j == 0)
    def _():
        xn_ref[...] = _rms(x_ref[...], g_ref[0:1, :]).astype(BF16)
        o_ref[...] = jnp.zeros_like(o_ref)

    xn = xn_ref[...]
    gate = _dot(xn, w1_ref[...])
    up = _dot(xn, w3_ref[...])
    act = (gate * _sigmoid(gate) * up).astype(BF16)
    o_ref[...] += _dot(act, w2_ref[...])

    @pl.when(j == pl.num_programs(1) - 1)
    def _():
        o_ref[...] = x_ref[...] + 0.5 * _rms(o_ref[...], g_ref[1:2, :])


def _ffn(h, norm, w13, w2, layer, tm, tf):
    M, D = h.shape
    F = w2.shape[1]
    nf = F // tf
    return pl.pallas_call(
        _ffn_kernel,
        grid=(M // tm, nf),
        in_specs=[
            pl.BlockSpec((tm, D), lambda i, j: (i, 0)),
            pl.BlockSpec((None, 2, D), lambda i, j: (layer, 0, 0)),
            pl.BlockSpec((None, D, tf), lambda i, j: (layer, 0, j)),
            pl.BlockSpec((None, D, tf), lambda i, j: (layer, 0, j + nf)),
            pl.BlockSpec((None, tf, D), lambda i, j: (layer, j, 0)),
        ],
        out_specs=pl.BlockSpec((tm, D), lambda i, j: (i, 0)),
        out_shape=jax.ShapeDtypeStruct((M, D), F32),
        scratch_shapes=[pltpu.VMEM((tm, D), BF16)],
        compiler_params=pltpu.CompilerParams(
            dimension_semantics=("parallel", "arbitrary"), vmem_limit_bytes=VMEM_LIMIT),
        name="ffn",
    )(h, norm, w13, w13, w2)


def _proj_kernel(x_ref, g_ref, w_ref, o32_ref, o16_ref, xn_ref, *, n32):
    j = pl.program_id(1)

    @pl.when(j == 0)
    def _():
        xn_ref[...] = _rms(x_ref[...], g_ref[0:1, :]).astype(BF16)

    @pl.when(j < n32)
    def _():
        o32_ref[...] = _dot(xn_ref[...], w_ref[...])

    @pl.when(j >= n32)
    def _():
        o16_ref[...] = _dot(xn_ref[...], w_ref[...]).astype(BF16)


def _proj(h, norm, w_in, layer, cols32, cols16, tm, tn):
    M, D = h.shape
    n32, n16 = cols32 // tn, cols16 // tn
    return pl.pallas_call(
        functools.partial(_proj_kernel, n32=n32),
        grid=(M // tm, n32 + n16),
        in_specs=[
            pl.BlockSpec((tm, D), lambda i, j: (i, 0)),
            pl.BlockSpec((None, 2, D), lambda i, j: (layer, 0, 0)),
            pl.BlockSpec((None, D, tn), lambda i, j: (layer, 0, j)),
        ],
        out_specs=[pl.BlockSpec((tm, tn), lambda i, j: (i, jnp.minimum(j, n32 - 1))),
                   pl.BlockSpec((tm, tn), lambda i, j: (i, jnp.maximum(j - n32, 0)))],
        out_shape=[jax.ShapeDtypeStruct((M, cols32), F32), jax.ShapeDtypeStruct((M, cols16), BF16)],
        scratch_shapes=[pltpu.VMEM((tm, D), BF16)],
        compiler_params=pltpu.CompilerParams(
            dimension_semantics=("parallel", "arbitrary"), vmem_limit_bytes=VMEM_LIMIT),
        name="proj",
    )(h, norm, w_in)


def _hgrn_kernel(q_ref, f_ref, i_ref, g_ref, lb_ref, ng_ref, o_ref, st_ref, kv_ref, sall_ref):
    tt = q_ref.shape[1]
    sub = A_SUB
    nb = tt // sub
    per = LANES // sub
    shape3 = (nb, sub, LANES)

    @pl.when(pl.program_id(2) == 0)
    def _():
        st_ref[...] = jnp.zeros_like(st_ref)

    lb = lb_ref[0]
    f = lb + (1.0 - lb) * _sigmoid(f_ref[0])
    k = 1.0 - f
    gl = jnp.log(f)
    r = lax.broadcasted_iota(jnp.int32, (LANES, LANES), 0)
    c = lax.broadcasted_iota(jnp.int32, (LANES, LANES), 1)
    tril = ((r // sub == c // sub) & (c <= r)).astype(BF16)
    cums = []
    for a in range(tt // LANES):
        hi, lo = _split_bf16(gl[a * LANES:(a + 1) * LANES])
        cums.append(_dot(tril, hi) + _dot(tril, lo))
    b = jnp.concatenate(cums, axis=0)
    q = q_ref[0]
    v = i_ref[0]
    b3, q3, k3, v3 = (x.reshape(shape3) for x in (b, q, k, v))

    ones = jnp.ones((LANES, LANES), BF16)
    rows = lax.broadcasted_iota(jnp.int32, (1, sub, LANES), 1)
    o3 = jnp.zeros(shape3, F32)
    for s in range(sub):
        e = jnp.exp(b3 - b3[:, s:s + 1, :])
        w = jnp.where(rows >= s, q3 * e * k3[:, s:s + 1, :], 0.0).astype(BF16)
        o3 = o3 + _dot(w.reshape(tt, LANES), ones).reshape(shape3) * v3[:, s:s + 1, :]

    b_last = b3[:, sub - 1:sub, :]
    kd = (k3 * jnp.exp(b_last - b3)).reshape(tt, LANES)
    dec = jnp.exp(b_last)
    for a in range(tt // LANES):
        v_t = v[a * LANES:(a + 1) * LANES].T.astype(BF16)
        kd_a = kd[a * LANES:(a + 1) * LANES]
        rhs = jnp.concatenate([jnp.where(r // sub == j, kd_a, 0.0).astype(BF16) for j in range(per)], axis=1)
        kv = _dot(v_t, rhs)
        for j in range(per):
            kv_ref[a * per + j] = kv[:, j * LANES:(j + 1) * LANES]

    st = st_ref[...]
    for n in range(nb):
        sall_ref[n] = st.astype(BF16)
        st = dec[n] * st + kv_ref[n]
    st_ref[...] = st

    qe = (q * jnp.exp(b)).astype(BF16)
    inter = []
    for a in range(tt // LANES):
        stack = sall_ref[a * per:(a + 1) * per].reshape(per * LANES, LANES)
        p = _dot_nt(qe[a * LANES:(a + 1) * LANES], stack)
        inter += [p[j * sub:(j + 1) * sub, j * LANES:(j + 1) * LANES] for j in range(per)]
    o = o3.reshape(tt, LANES) + jnp.concatenate(inter, axis=0)

    g = g_ref[0]
    o_ref[0] = (_rms(o, ng_ref[0]) * (g * _sigmoid(g))).astype(o_ref.dtype)


def _hgrn(pa, lb, ng, tt):
    Bn, T, _ = pa.shape
    H = A_HEADS

    def col(off):
        return pl.BlockSpec((1, tt, A_DK), lambda b, h, t: (b, t, off * H + h))

    return pl.pallas_call(
        _hgrn_kernel,
        grid=(Bn, H, T // tt),
        in_specs=[col(0), col(1), col(2), col(3),
                  pl.BlockSpec((1, 1, A_DK), lambda b, h, t: (h, 0, 0)),
                  pl.BlockSpec((1, 1, A_DK), lambda b, h, t: (0, 0, 0))],
        out_specs=pl.BlockSpec((1, tt, A_DK), lambda b, h, t: (b, t, h)),
        out_shape=jax.ShapeDtypeStruct((Bn, T, H * A_DK), BF16),
        scratch_shapes=[pltpu.VMEM((A_DK, A_DK), F32), pltpu.VMEM((tt // A_SUB, A_DK, A_DK), F32),
                        pltpu.VMEM((tt // A_SUB, A_DK, A_DK), BF16)],
        compiler_params=pltpu.CompilerParams(
            dimension_semantics=("parallel", "parallel", "arbitrary"), vmem_limit_bytes=VMEM_LIMIT),
        name="hgrn2",
    )(pa, pa, pa, pa, lb, ng)


def _sb_kernel(q_ref, k_ref, v_ref, o_ref, suf_ref, acc_ref, run_ref):
    tq = q_ref.shape[1]
    i = pl.program_id(2)
    q = q_ref[0]
    in_h0 = lax.broadcasted_iota(jnp.int32, (tq, LANES), 1) < HEAD_DIM
    zero = jnp.zeros_like(q)
    q2 = jnp.concatenate([jnp.where(in_h0, q, zero), jnp.where(in_h0, zero, q)], axis=0) * (HEAD_DIM ** -0.5)
    r = lax.broadcasted_iota(jnp.int32, (tq, tq), 0)
    c = lax.broadcasted_iota(jnp.int32, (tq, tq), 1)
    suf_ref[...] = (r >= c).astype(BF16)
    strict = jnp.concatenate([c < r, c < r], axis=0)

    def tile(kb, vb, run, diag):
        z = _dot_nt(q2, kb)
        sp = jnp.maximum(z, 0.0) + jnp.log(1.0 + jnp.exp(-jnp.abs(z)))
        if diag:
            sp = jnp.where(strict, sp, 0.0)
        hi, lo = _split_bf16(sp)
        suffix = suf_ref[...]
        cs = _dot(hi, suffix) + _dot(lo, suffix) + run
        a = jnp.exp(z - cs)
        if diag:
            a = jnp.where(strict, a, 0.0)
        return _dot(a.astype(BF16), vb), cs[:, 0:1]

    d0 = pl.ds(pl.multiple_of(i * tq, tq), tq)
    pv, run = tile(k_ref[0, d0, :], v_ref[0, d0, :], jnp.zeros((2 * tq, 1), F32), True)
    acc_ref[...] = pv
    run_ref[...] = run

    def cond(carry):
        s, lowest = carry
        return (s < i) & (lowest <= SB_EXIT)

    def body(carry):
        s, _ = carry
        sl = pl.ds(pl.multiple_of((i - 1 - s) * tq, tq), tq)
        pv, run = tile(k_ref[0, sl, :], v_ref[0, sl, :], run_ref[...], False)
        acc_ref[...] += pv
        run_ref[...] = run
        return s + 1, jnp.min(run)

    lax.while_loop(cond, body, (jnp.int32(0), jnp.min(run)))
    acc = acc_ref[...]
    o_ref[0] = jnp.where(in_h0, acc[:tq], acc[tq:]).astype(o_ref.dtype)


def _stick_breaking(pbcd, tq):
    Bn, T, _ = pbcd.shape
    npair = B_HEADS // 2
    return pl.pallas_call(
        _sb_kernel,
        grid=(Bn, npair, T // tq),
        in_specs=[pl.BlockSpec((1, tq, LANES), lambda b, p, i: (b, i, _BQ + p)),
                  pl.BlockSpec((1, T, LANES), lambda b, p, i: (b, 0, _BK + p)),
                  pl.BlockSpec((1, T, LANES), lambda b, p, i: (b, 0, _BV + p))],
        out_specs=pl.BlockSpec((1, tq, LANES), lambda b, p, i: (b, i, p)),
        out_shape=jax.ShapeDtypeStruct((Bn, T, npair * LANES), BF16),
        scratch_shapes=[pltpu.VMEM((tq, tq), BF16), pltpu.VMEM((2 * tq, LANES), F32),
                        pltpu.VMEM((2 * tq, 1), F32)],
        compiler_params=pltpu.CompilerParams(
            dimension_semantics=("parallel", "parallel", "arbitrary"), vmem_limit_bytes=VMEM_LIMIT),
        name="stick_breaking",
    )(pbcd, pbcd, pbcd)


def _win_kernel(q_ref, k_ref, kp_ref, v_ref, vp_ref, bias_ref, sink_ref, *out_refs, dil, nsub):
    blk = WIN_BLOCK
    span = blk * dil
    i = pl.program_id(2)
    lane = lax.broadcasted_iota(jnp.int32, (blk, LANES), 1)
    in_h0 = lane < HEAD_DIM
    prev_col = lax.broadcasted_iota(jnp.int32, (2 * blk, 2 * blk), 1) < blk
    ones = jnp.ones((2 * blk, LANES), BF16)
    sink2 = jnp.concatenate([jnp.broadcast_to(sink_ref[hh:hh + 1, 0:1], (blk, 1)) for hh in range(2)], axis=0)
    bias = bias_ref[...]

    def rows_of(start):
        return pl.ds(start, blk, stride=dil) if dil > 1 else pl.ds(start, blk)

    def residue_class(s, c):
        cur = rows_of(s * span + c)
        q = q_ref[0, cur, :].astype(BF16)
        if s == 0:
            kp, vp = kp_ref[0, rows_of(c), :], vp_ref[0, rows_of(c), :]
        else:
            prev = rows_of((s - 1) * span + c)
            kp, vp = k_ref[0, prev, :], v_ref[0, prev, :]
        k2 = jnp.concatenate([kp.astype(BF16), k_ref[0, cur, :].astype(BF16)], axis=0)
        v2 = jnp.concatenate([vp.astype(BF16), v_ref[0, cur, :].astype(BF16)], axis=0)
        zero = jnp.zeros_like(q)
        q2 = jnp.concatenate([jnp.where(in_h0, q, zero), jnp.where(in_h0, zero, q)], axis=0)
        logits = _dot_nt(q2 * (HEAD_DIM ** -0.5), k2) + bias
        if s == 0:
            logits = jnp.where(jnp.logical_and(i == 0, prev_col), -jnp.inf, logits)
        m = jnp.max(jnp.maximum(logits[:, :blk], logits[:, blk:]), axis=-1, keepdims=True)
        m = jnp.maximum(m, sink2)
        p = jnp.exp(logits - m).astype(BF16)
        den = _dot(p, ones) + jnp.exp(sink2 - m)
        o2 = _dot(p, v2) / den
        out_refs[0][0, cur, :] = jnp.where(in_h0, o2[:blk], o2[blk:]).astype(out_refs[0].dtype)
        if len(out_refs) > 1:
            lse2 = m + jnp.log(den)
            out_refs[1][0, cur, :] = jnp.where(in_h0, lse2[:blk], lse2[blk:])

    for s in range(nsub):
        if dil <= WIN_UNROLL:
            for c in range(dil):
                residue_class(s, c)
        else:
            def body(c, carry, s=s):
                residue_class(s, c)
                return carry
            lax.fori_loop(0, dil, body, 0, unroll=WIN_UNROLL)


def _window_attn(slab, bias, sinks, dil, qcol, kcol, vcol, npair, kv_shared, with_lse, out_dtype):
    Bn, T, _ = slab.shape
    span = WIN_BLOCK * dil
    nsub = max(1, WIN_UNROLL // dil)
    tb = span * nsub

    def qmap(b, p, i):
        return (b, i, qcol + p)

    def kvmap(col, prev):
        def f(b, p, i):
            return (b, jnp.maximum(i * nsub - 1, 0) if prev else i, col + (0 if kv_shared else p))
        return f

    blk = (1, tb, LANES)
    pblk = (1, span, LANES)
    out_spec = pl.BlockSpec(blk, lambda b, p, i: (b, i, p))
    out_sds = jax.ShapeDtypeStruct((Bn, T, npair * LANES), out_dtype)
    lse_sds = jax.ShapeDtypeStruct((Bn, T, npair * LANES), F32)
    return pl.pallas_call(
        functools.partial(_win_kernel, dil=dil, nsub=nsub),
        grid=(Bn, npair, T // tb),
        in_specs=[pl.BlockSpec(blk, qmap),
                  pl.BlockSpec(blk, kvmap(kcol, False)), pl.BlockSpec(pblk, kvmap(kcol, True)),
                  pl.BlockSpec(blk, kvmap(vcol, False)), pl.BlockSpec(pblk, kvmap(vcol, True)),
                  pl.BlockSpec((None, 2 * WIN_BLOCK, 2 * WIN_BLOCK), lambda b, p, i: (p, 0, 0)),
                  pl.BlockSpec((None, 2, LANES), lambda b, p, i: (p, 0, 0))],
        out_specs=[out_spec, out_spec] if with_lse else [out_spec],
        out_shape=[out_sds, lse_sds] if with_lse else [out_sds],
        compiler_params=pltpu.CompilerParams(
            dimension_semantics=("parallel", "parallel", "arbitrary"), vmem_limit_bytes=VMEM_LIMIT),
        name="window_attn",
    )(slab, slab, slab, slab, slab, bias, sinks)


def _combine_kernel(o0, o1, o2, l0, l1, l2, out_ref):
    a, b, c = l0[...], l1[...], l2[...]
    m = jnp.maximum(jnp.maximum(a, b), c)
    ea, eb, ec = jnp.exp(a - m), jnp.exp(b - m), jnp.exp(c - m)
    out_ref[...] = ((ea * o0[...] + eb * o1[...] + ec * o2[...]) / (ea + eb + ec)).astype(out_ref.dtype)


def _combine(outs, lses, tm):
    M, W = outs[0].shape
    spec = pl.BlockSpec((tm, W), lambda i: (i, 0))
    return pl.pallas_call(
        _combine_kernel,
        grid=(M // tm,),
        in_specs=[spec] * 6,
        out_specs=spec,
        out_shape=jax.ShapeDtypeStruct((M, W), BF16),
        compiler_params=pltpu.CompilerParams(dimension_semantics=("parallel",)),
        name="group_combine",
    )(*outs, *lses)


def _merge_kernel(x_ref, g_ref, ya_ref, yb_ref, yc_ref, yd_ref,
                  ga_ref, gb_ref, gc_ref, gd_ref, wa_ref, wb_ref, wc_ref, wd_ref, wo_ref,
                  o_ref, u_ref):
    j = pl.program_id(1)

    @pl.when(j == 0)
    def _():
        u_ref[...] = _rms(x_ref[...], g_ref[0:1, :]).astype(BF16)
        o_ref[...] = jnp.zeros_like(o_ref)

    u = u_ref[...]
    merged = None
    for y_ref, gate_ref, w_ref in ((ya_ref, ga_ref, wa_ref), (yb_ref, gb_ref, wb_ref),
                                   (yc_ref, gc_ref, wc_ref), (yd_ref, gd_ref, wd_ref)):
        term = _sigmoid(_dot(u, gate_ref[...])) * _dot(y_ref[...], w_ref[...])
        merged = term if merged is None else merged + term
    o_ref[...] += _dot(merged.astype(BF16), wo_ref[...])

    @pl.when(j == pl.num_programs(1) - 1)
    def _():
        o_ref[...] = x_ref[...] + _rms(o_ref[...], g_ref[1:2, :])


def _merge(h, norm, ys, w_in, gate_col0, w_branch, w_out, layer, tm, tn):
    M, D = h.shape
    gcb = gate_col0 // tn
    dcb = D // tn

    def yspec(w):
        return pl.BlockSpec((tm, w), lambda i, j: (i, 0))

    def gspec(br):
        return pl.BlockSpec((None, D, tn), lambda i, j: (layer, 0, gcb + br * dcb + j))

    def wspec(rows, rb):
        return pl.BlockSpec((None, rows, tn), lambda i, j: (layer, rb, j))

    wa, wb, wc, wd = (y.shape[1] for y in ys)
    return pl.pallas_call(
        _merge_kernel,
        grid=(M // tm, dcb),
        in_specs=[pl.BlockSpec((tm, D), lambda i, j: (i, 0)),
                  pl.BlockSpec((None, 2, D), lambda i, j: (layer, 0, 0)),
                  yspec(wa), yspec(wb), yspec(wc), yspec(wd),
                  gspec(0), gspec(1), gspec(2), gspec(3),
                  wspec(wa, 0), wspec(wb, wa // wb), wspec(wc, (wa + wb + wd) // wc),
                  wspec(wd, (wa + wb) // wd),
                  pl.BlockSpec((None, tn, D), lambda i, j: (layer, j, 0))],
        out_specs=pl.BlockSpec((tm, D), lambda i, j: (i, 0)),
        out_shape=jax.ShapeDtypeStruct((M, D), F32),
        scratch_shapes=[pltpu.VMEM((tm, D), BF16)],
        compiler_params=pltpu.CompilerParams(
            dimension_semantics=("parallel", "arbitrary"), vmem_limit_bytes=VMEM_LIMIT),
        name="gated_merge",
    )(h, norm, *ys, w_in, w_in, w_in, w_in, w_branch, w_branch, w_branch, w_branch, w_out)


def _rel_bucket(dist):
    max_exact = REL_BUCKETS // 2
    d = jnp.maximum(dist, 1).astype(F32)
    large = max_exact + (jnp.log(d / max_exact) / math.log(REL_MAX_DIST / max_exact)
                         * (REL_BUCKETS - max_exact)).astype(jnp.int32)
    large = jnp.minimum(large, REL_BUCKETS - 1)
    return jnp.where(dist < max_exact, dist, large)


def _window_bias(rel_bias, heads, dilation, max_dist):
    row = jnp.arange(WIN_BLOCK)[:, None]
    col = jnp.arange(2 * WIN_BLOCK)[None, :]
    dist = row + WIN_BLOCK - col
    valid = (dist >= 0) & (dist <= max_dist)
    tab = rel_bias[_rel_bucket(jnp.maximum(dist, 0) * dilation)]
    tab = jnp.transpose(tab[:, :, jnp.array(heads)], (2, 0, 1)).astype(F32)
    tab = jnp.where(valid[None], tab, -jnp.inf)
    return tab.reshape(len(heads) // 2, 2 * WIN_BLOCK, 2 * WIN_BLOCK)


def _trunk(x, ffn1_norm, ffn1_w13, ffn1_w2, mix_norm, w_in, hgrn_lb_logits, hgrn_out_norm,
           attn_sinks, w_branch, w_out, ffn2_norm, ffn2_w13, ffn2_w2, rel_bias, cfg):
    Bn, T, D = x.shape
    depth = w_in.shape[0]
    M = Bn * T
    a_w = A_HEADS * A_DK
    a_cols = 4 * a_w
    b_cols = 3 * B_HEADS * HEAD_DIM
    c_cols = 3 * C_HEADS * HEAD_DIM
    ac_cols = AC_BLOCKS * LANES
    bd_cols = BD_BLOCKS * LANES
    mix_in = ac_cols + bd_cols

    lb_sm = jax.nn.softmax(hgrn_lb_logits.astype(F32), axis=0)
    lb_all = (jnp.cumsum(lb_sm, axis=0) - lb_sm[0:1]).reshape(depth, A_HEADS, 1, A_DK)

    b0, c0 = a_cols, a_cols + b_cols
    dq0 = c0 + c_cols
    dk0 = dq0 + D_HEADS * HEAD_DIM
    perm = jnp.array([h * HEAD_DIM + d for h in D_HEAD_PERM for d in range(HEAD_DIM)])
    c_w = C_HPG * HEAD_DIM

    def c_part(part, g0, g1):
        base = c0 + part * C_HEADS * HEAD_DIM
        return w_in[:, :, base + g0 * c_w:base + g1 * c_w]

    n_grp = len(C_PATTERNS)
    w_in_b = jnp.concatenate(
        [w_in[:, :, :a_cols]] + [c_part(p, 1, n_grp) for p in range(3)]
        + [w_in[:, :, b0:c0]] + [c_part(p, 0, 1) for p in range(3)]
        + [w_in[:, :, dq0:dk0][:, :, perm], w_in[:, :, dk0:]], axis=-1).astype(BF16)
    d0 = 2 * a_w + c_w
    w_br_b = jnp.concatenate(
        [w_branch[:, :2 * a_w], w_branch[:, d0:][:, perm], w_branch[:, 2 * a_w:d0]], axis=1).astype(BF16)
    w_out_b = w_out.astype(BF16)
    f1_w13, f1_w2 = ffn1_w13.astype(BF16), ffn1_w2.astype(BF16)
    f2_w13, f2_w2 = ffn2_w13.astype(BF16), ffn2_w2.astype(BF16)

    c_bias = [_window_bias(rel_bias, list(range(g * C_HPG, (g + 1) * C_HPG)), dil, win // dil)
              for g, (win, dil) in enumerate(C_PATTERNS)]
    d_bias = _window_bias(rel_bias, [C_HEADS + h for h in D_HEAD_PERM], 1, D_WINDOW - 1)
    no_sink = jnp.full((C_HPG // 2, 2, LANES), -jnp.inf, F32)

    h = x.reshape(M, D)
    for l in range(depth):
        h = _ffn(h, ffn1_norm, f1_w13, f1_w2, l, cfg["ffn_tm"], cfg["ffn_tf"])

        pac, pbd = _proj(h, mix_norm, w_in_b, l, ac_cols, bd_cols, cfg["proj_tm"], cfg["proj_tn"])
        pac = pac.reshape(Bn, T, ac_cols)
        pbd = pbd.reshape(Bn, T, bd_cols)

        ya = _hgrn(pac, lb_all[l], hgrn_out_norm[l].reshape(1, 1, A_DK).astype(F32), cfg["hgrn_tt"])
        yb = _stick_breaking(pbd, cfg["sb_tq"])
        c_out, c_lse = [], []
        for g, (win, dil) in enumerate(C_PATTERNS):
            if dil == 1:
                slab, cols = pbd, (_C0Q, _C0K, _C0V)
            else:
                slab, cols = pac, tuple(c + 2 * (g - 1) for c in (_CQ, _CK, _CV))
            o, lse = _window_attn(slab, c_bias[g], no_sink, dil, *cols, C_HPG // 2, False, True, F32)
            c_out.append(o.reshape(M, c_w))
            c_lse.append(lse.reshape(M, c_w))
        yc = _combine(c_out, c_lse, cfg["comb_tm"])
        sinks = attn_sinks[l].astype(F32)[jnp.array(D_HEAD_PERM)].reshape(D_HEADS // 2, 2, 1)
        sinks = jnp.broadcast_to(sinks, (D_HEADS // 2, 2, LANES))
        (yd,) = _window_attn(pbd, d_bias, sinks, 1, _DQ, _DK, _DV, D_HEADS // 2, True, False, BF16)

        ys = (ya.reshape(M, a_w), yb.reshape(M, a_w), yc, yd.reshape(M, a_w))
        h = _merge(h, mix_norm, ys, w_in_b, mix_in, w_br_b, w_out_b, l, cfg["merge_tm"], cfg["merge_tn"])

        h = _ffn(h, ffn2_norm, f2_w13, f2_w2, l, cfg["ffn_tm"], cfg["ffn_tf"])
    return h.reshape(Bn, T, D)


_CFG = dict(ffn_tm=512, ffn_tf=512, proj_tm=1024, proj_tn=512, hgrn_tt=512, sb_tq=256,
            comb_tm=1024, merge_tm=512, merge_tn=512)


def kernel(x, ffn1_norm, ffn1_w13, ffn1_w2, mix_norm, w_in, hgrn_lb_logits, hgrn_out_norm, attn_sinks,
           w_branch, w_out, ffn2_norm, ffn2_w13, ffn2_w2, rel_bias):
    return _trunk(x, ffn1_norm, ffn1_w13, ffn1_w2, mix_norm, w_in, hgrn_lb_logits, hgrn_out_norm,
                  attn_sinks, w_branch, w_out, ffn2_norm, ffn2_w13, ffn2_w2, rel_bias, _CFG)
```

```python
import functools
import math

import jax
import jax.numpy as jnp
from jax import lax
from jax.experimental import pallas as pl
from jax.experimental.pallas import tpu as pltpu

F32 = jnp.float32
BF16 = jnp.bfloat16

EPS = 1e-6
HEAD_DIM = 64
LANES = 128
A_HEADS = 4
A_DK = 128
B_HEADS = 8
C_PATTERNS = ((128, 1), (512, 4), (2048, 16))
C_HPG = 4
C_HEADS = len(C_PATTERNS) * C_HPG
D_HEADS = 8
D_KV_HEADS = 2
D_WINDOW = 128
WIN_BLOCK = 128
REL_BUCKETS = 32
REL_MAX_DIST = 2048
A_SUB = 16
VMEM_LIMIT = 60 * 1024 * 1024

_CQ, _CK, _CV = 16, 20, 24
AC_BLOCKS = 28
_BQ, _BK, _BV = 0, 4, 8
_DQ, _DK, _DV = 12, 16, 17
_C0Q, _C0K, _C0V = 18, 20, 22
BD_BLOCKS = 24
WIN_UNROLL = 4
D_HEAD_PERM = (0, 4, 1, 5, 2, 6, 3, 7)
SB_CHAINS = 2
SB_EXIT = 100.0


def _rms(x, g):
    ms = jnp.mean(x * x, axis=-1, keepdims=True)
    return x * lax.rsqrt(ms + EPS) * g


def _sigmoid(x):
    return 1.0 / (1.0 + jnp.exp(-x))


def _dot(a, b):
    return jnp.dot(a, b, preferred_element_type=F32)


def _dot_nt(a, b):
    return lax.dot_general(a, b, (((1,), (1,)), ((), ())), preferred_element_type=F32)


def _dot_tn(a, b):
    return lax.dot_general(a, b, (((0,), (0,)), ((), ())), preferred_element_type=F32)


def _split_bf16(x):
    hi = x.astype(BF16)
    lo = (x - hi.astype(F32)).astype(BF16)
    return hi, lo


def _ffn_kernel(x_ref, g_ref, w1_ref, w3_ref, w2_ref, o_ref, xn_ref):
    j = pl.program_id(1)

    @pl.when(j == 0)
    def _():
        xn_ref[...] = _rms(x_ref[...], g_ref[0:1, :]).astype(BF16)
        o_ref[...] = jnp.zeros_like(o_ref)

    xn = xn_ref[...]
    gate = _dot(xn, w1_ref[...])
    up = _dot(xn, w3_ref[...])
    act = (gate * _sigmoid(gate) * up).astype(BF16)
    o_ref[...] += _dot(act, w2_ref[...])

    @pl.when(j == pl.num_programs(1) - 1)
    def _():
        o_ref[...] = x_ref[...] + 0.5 * _rms(o_ref[...], g_ref[1:2, :])


def _ffn(h, norm, w13, w2, layer, tm, tf):
    M, D = h.shape
    F = w2.shape[1]
    nf = F // tf
    return pl.pallas_call(
        _ffn_kernel,
        grid=(M // tm, nf),
        in_specs=[
            pl.BlockSpec((tm, D), lambda i, j: (i, 0)),
            pl.BlockSpec((None, 2, D), lambda i, j: (layer, 0, 0)),
            pl.BlockSpec((None, D, tf), lambda i, j: (layer, 0, j)),
            pl.BlockSpec((None, D, tf), lambda i, j: (layer, 0, j + nf)),
            pl.BlockSpec((None, tf, D), lambda i, j: (layer, j, 0)),
        ],
        out_specs=pl.BlockSpec((tm, D), lambda i, j: (i, 0)),
        out_shape=jax.ShapeDtypeStruct((M, D), F32),
        scratch_shapes=[pltpu.VMEM((tm, D), BF16)],
        compiler_params=pltpu.CompilerParams(
            dimension_semantics=("parallel", "arbitrary"), vmem_limit_bytes=VMEM_LIMIT),
        name="ffn",
    )(h, norm, w13, w13, w2)


def _proj_kernel(x_ref, g_ref, w_ref, o32_ref, o16_ref, xn_ref, *, n32):
    j = pl.program_id(1)

    @pl.when(j == 0)
    def _():
        xn_ref[...] = _rms(x_ref[...], g_ref[0:1, :]).astype(BF16)

    @pl.when(j < n32)
    def _():
        o32_ref[...] = _dot(xn_ref[...], w_ref[...])

    @pl.when(j >= n32)
    def _():
        o16_ref[...] = _dot(xn_ref[...], w_ref[...]).astype(BF16)


def _proj(h, norm, w_in, layer, cols32, cols16, tm, tn):
    M, D = h.shape
    n32, n16 = cols32 // tn, cols16 // tn
    return pl.pallas_call(
        functools.partial(_proj_kernel, n32=n32),
        grid=(M // tm, n32 + n16),
        in_specs=[
            pl.BlockSpec((tm, D), lambda i, j: (i, 0)),
            pl.BlockSpec((None, 2, D), lambda i, j: (layer, 0, 0)),
            pl.BlockSpec((None, D, tn), lambda i, j: (layer, 0, j)),
        ],
        out_specs=[pl.BlockSpec((tm, tn), lambda i, j: (i, jnp.minimum(j, n32 - 1))),
                   pl.BlockSpec((tm, tn), lambda i, j: (i, jnp.maximum(j - n32, 0)))],
        out_shape=[jax.ShapeDtypeStruct((M, cols32), F32), jax.ShapeDtypeStruct((M, cols16), BF16)],
        scratch_shapes=[pltpu.VMEM((tm, D), BF16)],
        compiler_params=pltpu.CompilerParams(
            dimension_semantics=("parallel", "arbitrary"), vmem_limit_bytes=VMEM_LIMIT),
        name="proj",
    )(h, norm, w_in)


def _hgrn_kernel(q_ref, f_ref, i_ref, g_ref, lb_ref, ng_ref, o_ref, st_ref, kv_ref, sall_ref):
    tt = q_ref.shape[1]
    sub = A_SUB
    nb = tt // sub
    per = LANES // sub
    shape3 = (nb, sub, LANES)

    @pl.when(pl.program_id(2) == 0)
    def _():
        st_ref[...] = jnp.zeros_like(st_ref)

    lb = lb_ref[0]
    f = lb + (1.0 - lb) * _sigmoid(f_ref[0])
    k = 1.0 - f
    gl = jnp.log(f)
    r = lax.broadcasted_iota(jnp.int32, (LANES, LANES), 0)
    c = lax.broadcasted_iota(jnp.int32, (LANES, LANES), 1)
    tril = ((r // sub == c // sub) & (c <= r)).astype(BF16)
    cums = []
    for a in range(tt // LANES):
        hi, lo = _split_bf16(gl[a * LANES:(a + 1) * LANES])
        cums.append(_dot(tril, hi) + _dot(tril, lo))
    b = jnp.concatenate(cums, axis=0)
    q = q_ref[0]
    v = i_ref[0]
    b3, q3, k3, v3 = (x.reshape(shape3) for x in (b, q, k, v))

    ones = jnp.ones((LANES, LANES), BF16)
    rows = lax.broadcasted_iota(jnp.int32, (1, sub, LANES), 1)
    o3 = jnp.zeros(shape3, F32)
    for s in range(sub):
        e = jnp.exp(b3 - b3[:, s:s + 1, :])
        w = jnp.where(rows >= s, q3 * e * k3[:, s:s + 1, :], 0.0).astype(BF16)
        o3 = o3 + _dot(w.reshape(tt, LANES), ones).reshape(shape3) * v3[:, s:s + 1, :]

    b_last = b3[:, sub - 1:sub, :]
    kd = (k3 * jnp.exp(b_last - b3)).reshape(tt, LANES)
    dec = jnp.exp(b_last)
    for a in range(tt // LANES):
        v_t = v[a * LANES:(a + 1) * LANES].T.astype(BF16)
        kd_a = kd[a * LANES:(a + 1) * LANES]
        rhs = jnp.concatenate([jnp.where(r // sub == j, kd_a, 0.0).astype(BF16) for j in range(per)], axis=1)
        kv = _dot(v_t, rhs)
        for j in range(per):
            kv_ref[a * per + j] = kv[:, j * LANES:(j + 1) * LANES]

    st = st_ref[...]
    for n in range(nb):
        sall_ref[n] = st.astype(BF16)
        st = dec[n] * st + kv_ref[n]
    st_ref[...] = st

    qe = (q * jnp.exp(b)).astype(BF16)
    inter = []
    for a in range(tt // LANES):
        stack = sall_ref[a * per:(a + 1) * per].reshape(per * LANES, LANES)
        p = _dot_nt(qe[a * LANES:(a + 1) * LANES], stack)
        inter += [p[j * sub:(j + 1) * sub, j * LANES:(j + 1) * LANES] for j in range(per)]
    o = o3.reshape(tt, LANES) + jnp.concatenate(inter, axis=0)

    g = g_ref[0]
    o_ref[0] = (_rms(o, ng_ref[0]) * (g * _sigmoid(g))).astype(o_ref.dtype)


def _hgrn(pa, lb, ng, tt):
    Bn, T, _ = pa.shape
    H = A_HEADS

    def col(off):
        return pl.BlockSpec((1, tt, A_DK), lambda b, h, t: (b, t, off * H + h))

    return pl.pallas_call(
        _hgrn_kernel,
        grid=(Bn, H, T // tt),
        in_specs=[col(0), col(1), col(2), col(3),
                  pl.BlockSpec((1, 1, A_DK), lambda b, h, t: (h, 0, 0)),
                  pl.BlockSpec((1, 1, A_DK), lambda b, h, t: (0, 0, 0))],
        out_specs=pl.BlockSpec((1, tt, A_DK), lambda b, h, t: (b, t, h)),
        out_shape=jax.ShapeDtypeStruct((Bn, T, H * A_DK), BF16),
        scratch_shapes=[pltpu.VMEM((A_DK, A_DK), F32), pltpu.VMEM((tt // A_SUB, A_DK, A_DK), F32),
                        pltpu.VMEM((tt // A_SUB, A_DK, A_DK), BF16)],
        compiler_params=pltpu.CompilerParams(
            dimension_semantics=("parallel", "parallel", "arbitrary"), vmem_limit_bytes=VMEM_LIMIT),
        name="hgrn2",
    )(pa, pa, pa, pa, lb, ng)


def _sb_kernel(q_ref, k_ref, v_ref, o_ref, suf_ref, acc_ref, run_ref):
    tq = q_ref.shape[1]
    i = pl.program_id(2)
    in_h0 = lax.broadcasted_iota(jnp.int32, (tq, LANES), 1) < HEAD_DIM
    r = lax.broadcasted_iota(jnp.int32, (tq, tq), 0)
    c = lax.broadcasted_iota(jnp.int32, (tq, tq), 1)
    suf_ref[...] = (r >= c).astype(BF16)
    strict = jnp.concatenate([c < r, c < r], axis=0)

    def lanes(n):
        return slice(n * LANES, (n + 1) * LANES)

    q2s = []
    for n in range(SB_CHAINS):
        q = q_ref[0, :, lanes(n)]
        zero = jnp.zeros_like(q)
        q2s.append(jnp.concatenate([jnp.where(in_h0, q, zero), jnp.where(in_h0, zero, q)], axis=0)
                   * (HEAD_DIM ** -0.5))

    def tile(q2, kb, vb, run, diag):
        z = _dot_nt(q2, kb)
        sp = jnp.maximum(z, 0.0) + jnp.log(1.0 + jnp.exp(-jnp.abs(z)))
        if diag:
            sp = jnp.where(strict, sp, 0.0)
        hi, lo = _split_bf16(sp)
        suffix = suf_ref[...]
        cs = _dot(hi, suffix) + _dot(lo, suffix) + run
        a = jnp.exp(z - cs)
        if diag:
            a = jnp.where(strict, a, 0.0)
        return _dot(a.astype(BF16), vb), cs[:, 0:1]

    def step(rows, first):
        lowest = None
        for n in range(SB_CHAINS):
            run = jnp.zeros((2 * tq, 1), F32) if first else run_ref[n]
            pv, run = tile(q2s[n], k_ref[0, rows, lanes(n)], v_ref[0, rows, lanes(n)], run, first)
            acc_ref[n] = pv if first else acc_ref[n] + pv
            run_ref[n] = run
            low = jnp.min(run)
            lowest = low if lowest is None else jnp.minimum(lowest, low)
        return lowest

    lowest = step(pl.ds(pl.multiple_of(i * tq, tq), tq), True)

    def cond(carry):
        s, lowest = carry
        return (s < i) & (lowest <= SB_EXIT)

    def body(carry):
        s, _ = carry
        return s + 1, step(pl.ds(pl.multiple_of((i - 1 - s) * tq, tq), tq), False)

    lax.while_loop(cond, body, (jnp.int32(0), lowest))
    for n in range(SB_CHAINS):
        acc = acc_ref[n]
        o_ref[0, :, lanes(n)] = jnp.where(in_h0, acc[:tq], acc[tq:]).astype(o_ref.dtype)


def _stick_breaking(pbd, tq):
    Bn, T, _ = pbd.shape
    ngrp = B_HEADS // 2 // SB_CHAINS
    w = SB_CHAINS * LANES
    return pl.pallas_call(
        _sb_kernel,
        grid=(Bn, ngrp, T // tq),
        in_specs=[pl.BlockSpec((1, tq, w), lambda b, p, i: (b, i, _BQ // SB_CHAINS + p)),
                  pl.BlockSpec((1, T, w), lambda b, p, i: (b, 0, _BK // SB_CHAINS + p)),
                  pl.BlockSpec((1, T, w), lambda b, p, i: (b, 0, _BV // SB_CHAINS + p))],
        out_specs=pl.BlockSpec((1, tq, w), lambda b, p, i: (b, i, p)),
        out_shape=jax.ShapeDtypeStruct((Bn, T, B_HEADS // 2 * LANES), BF16),
        scratch_shapes=[pltpu.VMEM((tq, tq), BF16), pltpu.VMEM((SB_CHAINS, 2 * tq, LANES), F32),
                        pltpu.VMEM((SB_CHAINS, 2 * tq, 1), F32)],
        compiler_params=pltpu.CompilerParams(
            dimension_semantics=("parallel", "parallel", "arbitrary"), vmem_limit_bytes=VMEM_LIMIT),
        name="stick_breaking",
    )(pbd, pbd, pbd)


def _win_kernel(q_ref, k_ref, kp_ref, v_ref, vp_ref, bias_ref, sink_ref, *out_refs, dil, nsub, nq):
    blk = WIN_BLOCK
    span = blk * dil
    nrow = 2 * nq * blk
    i = pl.program_id(2)
    lane = lax.broadcasted_iota(jnp.int32, (blk, LANES), 1)
    in_h0 = lane < HEAD_DIM
    prev_col = lax.broadcasted_iota(jnp.int32, (nrow, 2 * blk), 1) < blk
    ones = jnp.ones((2 * blk, LANES), BF16)
    sinks = jnp.concatenate([jnp.broadcast_to(sink_ref[hh:hh + 1, 0:1], (blk, 1)) for hh in range(2 * nq)],
                            axis=0)
    bias = bias_ref[...]

    def rows_of(start):
        return pl.ds(start, blk, stride=dil) if dil > 1 else pl.ds(start, blk)

    def residue_class(s, c):
        cur = rows_of(s * span + c)
        q = q_ref[0, cur, :].astype(BF16)
        if s == 0:
            kp, vp = kp_ref[0, rows_of(c), :], vp_ref[0, rows_of(c), :]
        else:
            prev = rows_of((s - 1) * span + c)
            kp, vp = k_ref[0, prev, :], v_ref[0, prev, :]
        k2 = jnp.concatenate([kp.astype(BF16), k_ref[0, cur, :].astype(BF16)], axis=0)
        v2 = jnp.concatenate([vp.astype(BF16), v_ref[0, cur, :].astype(BF16)], axis=0)
        qs = [q[:, p * LANES:(p + 1) * LANES] for p in range(nq)]
        zero = jnp.zeros_like(qs[0])
        q2 = jnp.concatenate([jnp.where(in_h0, x, zero) for x in qs]
                             + [jnp.where(in_h0, zero, x) for x in qs], axis=0)
        logits = _dot_nt(q2 * (HEAD_DIM ** -0.5), k2) + bias
        if s == 0:
            logits = jnp.where(jnp.logical_and(i == 0, prev_col), -jnp.inf, logits)
        m = jnp.max(jnp.maximum(logits[:, :blk], logits[:, blk:]), axis=-1, keepdims=True)
        m = jnp.maximum(m, sinks)
        p = jnp.exp(logits - m).astype(BF16)
        if nq > 1:
            pv = _dot(p, jnp.concatenate([v2, ones], axis=1))
            num, den = pv[:, :LANES], pv[:, LANES:]
        else:
            num, den = _dot(p, v2), _dot(p, ones)
        den = den + jnp.exp(sinks - m)
        o2 = num / den

        def lane_blocks(x):
            return jnp.concatenate([jnp.where(in_h0, x[b * blk:(b + 1) * blk], x[(nq + b) * blk:(nq + b + 1) * blk])
                                    for b in range(nq)], axis=1)

        out_refs[0][0, cur, :] = lane_blocks(o2).astype(out_refs[0].dtype)
        if len(out_refs) > 1:
            out_refs[1][0, cur, :] = lane_blocks(m + jnp.log(den))

    for s in range(nsub):
        if dil <= WIN_UNROLL:
            for c in range(dil):
                residue_class(s, c)
        else:
            def body(c, carry, s=s):
                residue_class(s, c)
                return carry
            lax.fori_loop(0, dil, body, 0, unroll=WIN_UNROLL)


def _window_attn(slab, bias, sinks, dil, qcol, kcol, vcol, ngrp, nq, with_lse, out_dtype):
    Bn, T, _ = slab.shape
    span = WIN_BLOCK * dil
    nsub = max(1, WIN_UNROLL // dil)
    tb = span * nsub
    assert qcol % nq == 0

    def kvmap(col, prev):
        def f(b, g, i):
            return (b, jnp.maximum(i * nsub - 1, 0) if prev else i, col + g)
        return f

    qblk = (1, tb, nq * LANES)
    blk = (1, tb, LANES)
    pblk = (1, span, LANES)
    out_spec = pl.BlockSpec(qblk, lambda b, g, i: (b, i, g))
    out_sds = jax.ShapeDtypeStruct((Bn, T, ngrp * nq * LANES), out_dtype)
    lse_sds = jax.ShapeDtypeStruct((Bn, T, ngrp * nq * LANES), F32)
    return pl.pallas_call(
        functools.partial(_win_kernel, dil=dil, nsub=nsub, nq=nq),
        grid=(Bn, ngrp, T // tb),
        in_specs=[pl.BlockSpec(qblk, lambda b, g, i: (b, i, qcol // nq + g)),
                  pl.BlockSpec(blk, kvmap(kcol, False)), pl.BlockSpec(pblk, kvmap(kcol, True)),
                  pl.BlockSpec(blk, kvmap(vcol, False)), pl.BlockSpec(pblk, kvmap(vcol, True)),
                  pl.BlockSpec((None, 2 * nq * WIN_BLOCK, 2 * WIN_BLOCK), lambda b, g, i: (g, 0, 0)),
                  pl.BlockSpec((None, 2 * nq, LANES), lambda b, g, i: (g, 0, 0))],
        out_specs=[out_spec, out_spec] if with_lse else [out_spec],
        out_shape=[out_sds, lse_sds] if with_lse else [out_sds],
        compiler_params=pltpu.CompilerParams(
            dimension_semantics=("parallel", "parallel", "arbitrary"), vmem_limit_bytes=VMEM_LIMIT),
        name="window_attn",
    )(slab, slab, slab, slab, slab, bias, sinks)


def _combine_kernel(o0, o1, o2, l0, l1, l2, out_ref):
    a, b, c = l0[...], l1[...], l2[...]
    m = jnp.maximum(jnp.maximum(a, b), c)
    ea, eb, ec = jnp.exp(a - m), jnp.exp(b - m), jnp.exp(c - m)
    out_ref[...] = ((ea * o0[...] + eb * o1[...] + ec * o2[...]) / (ea + eb + ec)).astype(out_ref.dtype)


def _combine(outs, lses, tm):
    M, W = outs[0].shape
    spec = pl.BlockSpec((tm, W), lambda i: (i, 0))
    return pl.pallas_call(
        _combine_kernel,
        grid=(M // tm,),
        in_specs=[spec] * 6,
        out_specs=spec,
        out_shape=jax.ShapeDtypeStruct((M, W), BF16),
        compiler_params=pltpu.CompilerParams(dimension_semantics=("parallel",)),
        name="group_combine",
    )(*outs, *lses)


def _merge_kernel(x_ref, g_ref, ya_ref, yb_ref, yc_ref, yd_ref,
                  ga_ref, gb_ref, gc_ref, gd_ref, wa_ref, wb_ref, wc_ref, wd_ref, wo_ref,
                  o_ref, u_ref):
    j = pl.program_id(1)

    @pl.when(j == 0)
    def _():
        u_ref[...] = _rms(x_ref[...], g_ref[0:1, :]).astype(BF16)
        o_ref[...] = jnp.zeros_like(o_ref)

    u = u_ref[...]
    merged = None
    for y_ref, gate_ref, w_ref in ((ya_ref, ga_ref, wa_ref), (yb_ref, gb_ref, wb_ref),
                                   (yc_ref, gc_ref, wc_ref), (yd_ref, gd_ref, wd_ref)):
        term = _sigmoid(_dot(u, gate_ref[...])) * _dot(y_ref[...], w_ref[...])
        merged = term if merged is None else merged + term
    o_ref[...] += _dot(merged.astype(BF16), wo_ref[...])

    @pl.when(j == pl.num_programs(1) - 1)
    def _():
        o_ref[...] = x_ref[...] + _rms(o_ref[...], g_ref[1:2, :])


def _merge(h, norm, ys, w_in, gate_col0, w_branch, w_out, layer, tm, tn):
    M, D = h.shape
    gcb = gate_col0 // tn
    dcb = D // tn

    def yspec(w):
        return pl.BlockSpec((tm, w), lambda i, j: (i, 0))

    def gspec(br):
        return pl.BlockSpec((None, D, tn), lambda i, j: (layer, 0, gcb + br * dcb + j))

    def wspec(rows, rb):
        return pl.BlockSpec((None, rows, tn), lambda i, j: (layer, rb, j))

    wa, wb, wc, wd = (y.shape[1] for y in ys)
    return pl.pallas_call(
        _merge_kernel,
        grid=(M // tm, dcb),
        in_specs=[pl.BlockSpec((tm, D), lambda i, j: (i, 0)),
                  pl.BlockSpec((None, 2, D), lambda i, j: (layer, 0, 0)),
                  yspec(wa), yspec(wb), yspec(wc), yspec(wd),
                  gspec(0), gspec(1), gspec(2), gspec(3),
                  wspec(wa, 0), wspec(wb, wa // wb), wspec(wc, (wa + wb + wd) // wc),
                  wspec(wd, (wa + wb) // wd),
                  pl.BlockSpec((None, tn, D), lambda i, j: (layer, j, 0))],
        out_specs=pl.BlockSpec((tm, D), lambda i, j: (i, 0)),
        out_shape=jax.ShapeDtypeStruct((M, D), F32),
        scratch_shapes=[pltpu.VMEM((tm, D), BF16)],
        compiler_params=pltpu.CompilerParams(
            dimension_semantics=("parallel", "arbitrary"), vmem_limit_bytes=VMEM_LIMIT),
        name="gated_merge",
    )(h, norm, *ys, w_in, w_in, w_in, w_in, w_branch, w_branch, w_branch, w_branch, w_out)


def _rel_bucket(dist):
    max_exact = REL_BUCKETS // 2
    d = jnp.maximum(dist, 1).astype(F32)
    large = max_exact + (jnp.log(d / max_exact) / math.log(REL_MAX_DIST / max_exact)
                         * (REL_BUCKETS - max_exact)).astype(jnp.int32)
    large = jnp.minimum(large, REL_BUCKETS - 1)
    return jnp.where(dist < max_exact, dist, large)


def _window_bias(rel_bias, heads, dilation, max_dist, rows_heads=2):
    row = jnp.arange(WIN_BLOCK)[:, None]
    col = jnp.arange(2 * WIN_BLOCK)[None, :]
    dist = row + WIN_BLOCK - col
    valid = (dist >= 0) & (dist <= max_dist)
    tab = rel_bias[_rel_bucket(jnp.maximum(dist, 0) * dilation)]
    tab = jnp.transpose(tab[:, :, jnp.array(heads)], (2, 0, 1)).astype(F32)
    tab = jnp.where(valid[None], tab, -jnp.inf)
    return tab.reshape(len(heads) // rows_heads, rows_heads * WIN_BLOCK, 2 * WIN_BLOCK)


def _trunk(x, ffn1_norm, ffn1_w13, ffn1_w2, mix_norm, w_in, hgrn_lb_logits, hgrn_out_norm,
           attn_sinks, w_branch, w_out, ffn2_norm, ffn2_w13, ffn2_w2, rel_bias, cfg):
    Bn, T, D = x.shape
    depth = w_in.shape[0]
    M = Bn * T
    a_w = A_HEADS * A_DK
    a_cols = 4 * a_w
    b_cols = 3 * B_HEADS * HEAD_DIM
    c_cols = 3 * C_HEADS * HEAD_DIM
    ac_cols = AC_BLOCKS * LANES
    bd_cols = BD_BLOCKS * LANES
    mix_in = ac_cols + bd_cols

    lb_sm = jax.nn.softmax(hgrn_lb_logits.astype(F32), axis=0)
    lb_all = (jnp.cumsum(lb_sm, axis=0) - lb_sm[0:1]).reshape(depth, A_HEADS, 1, A_DK)

    b0, c0 = a_cols, a_cols + b_cols
    dq0 = c0 + c_cols
    dk0 = dq0 + D_HEADS * HEAD_DIM
    c_w = C_HPG * HEAD_DIM

    def d_heads(w, axis, start):
        return [lax.slice_in_dim(w, start + h * HEAD_DIM, start + (h + 1) * HEAD_DIM, axis=axis)
                for h in D_HEAD_PERM]

    def c_part(part, g0, g1):
        base = c0 + part * C_HEADS * HEAD_DIM
        return w_in[:, :, base + g0 * c_w:base + g1 * c_w]

    n_grp = len(C_PATTERNS)
    w_in_b = jnp.concatenate(
        [w_in[:, :, :a_cols]] + [c_part(p, 1, n_grp) for p in range(3)]
        + [w_in[:, :, b0:c0]] + d_heads(w_in, 2, dq0) + [w_in[:, :, dk0:mix_in]]
        + [c_part(p, 0, 1) for p in range(3)] + [w_in[:, :, mix_in:]], axis=-1).astype(BF16)
    d0 = 2 * a_w + c_w
    w_br_b = jnp.concatenate(
        [w_branch[:, :2 * a_w]] + d_heads(w_branch, 1, d0) + [w_branch[:, 2 * a_w:d0]], axis=1).astype(BF16)
    w_out_b = w_out.astype(BF16)
    f1_w13, f1_w2 = ffn1_w13.astype(BF16), ffn1_w2.astype(BF16)
    f2_w13, f2_w2 = ffn2_w13.astype(BF16), ffn2_w2.astype(BF16)

    c_bias = [_window_bias(rel_bias, list(range(g * C_HPG, (g + 1) * C_HPG)), dil, win // dil)
              for g, (win, dil) in enumerate(C_PATTERNS)]
    d_bias = _window_bias(rel_bias, [C_HEADS + h for h in range(D_HEADS)], 1, D_WINDOW - 1, D_HEADS)
    no_sink = jnp.full((C_HPG // 2, 2, LANES), -jnp.inf, F32)

    h = x.reshape(M, D)
    for l in range(depth):
        h = _ffn(h, ffn1_norm, f1_w13, f1_w2, l, cfg["ffn_tm"], cfg["ffn_tf"])

        pac, pbd = _proj(h, mix_norm, w_in_b, l, ac_cols, bd_cols, cfg["proj_tm"], cfg["proj_tn"])
        pac = pac.reshape(Bn, T, ac_cols)
        pbd = pbd.reshape(Bn, T, bd_cols)

        ya = _hgrn(pac, lb_all[l], hgrn_out_norm[l].reshape(1, 1, A_DK).astype(F32), cfg["hgrn_tt"])
        yb = _stick_breaking(pbd, cfg["sb_tq"])
        c_out, c_lse = [], []
        for g, (win, dil) in enumerate(C_PATTERNS):
            if dil == 1:
                slab, cols = pbd, (_C0Q, _C0K, _C0V)
            else:
                slab, cols = pac, tuple(c + 2 * (g - 1) for c in (_CQ, _CK, _CV))
            o, lse = _window_attn(slab, c_bias[g], no_sink, dil, *cols, C_HPG // 2, 1, True, F32)
            c_out.append(o.reshape(M, c_w))
            c_lse.append(lse.reshape(M, c_w))
        yc = _combine(c_out, c_lse, cfg["comb_tm"])
        sinks = jnp.broadcast_to(attn_sinks[l].astype(F32).reshape(1, D_HEADS, 1), (1, D_HEADS, LANES))
        (yd,) = _window_attn(pbd, d_bias, sinks, 1, _DQ, _DK, _DV, 1, D_HEADS // 2, False, BF16)

        ys = (ya.reshape(M, a_w), yb.reshape(M, a_w), yc, yd.reshape(M, a_w))
        h = _merge(h, mix_norm, ys, w_in_b, mix_in, w_br_b, w_out_b, l, cfg["merge_tm"], cfg["merge_tn"])

        h = _ffn(h, ffn2_norm, f2_w13, f2_w2, l, cfg["ffn_tm"], cfg["ffn_tf"])
    return h.reshape(Bn, T, D)


_CFG = dict(ffn_tm=512, ffn_tf=512, proj_tm=1024, proj_tn=512, hgrn_tt=512, sb_tq=256,
            comb_tm=1024, merge_tm=512, merge_tn=512)


def kernel(x, ffn1_norm, ffn1_w13, ffn1_w2, mix_norm, w_in, hgrn_lb_logits, hgrn_out_norm, attn_sinks,
           w_branch, w_out, ffn2_norm, ffn2_w13, ffn2_w2, rel_bias):
    return _trunk(x, ffn1_norm, ffn1_w13, ffn1_w2, mix_norm, w_in, hgrn_lb_logits, hgrn_out_norm,
                  attn_sinks, w_branch, w_out, ffn2_norm, ffn2_w13, ffn2_w2, rel_bias, _CFG)
```

```python
import functools
import math

import jax
import jax.numpy as jnp
from jax import lax
from jax.experimental import pallas as pl
from jax.experimental.pallas import tpu as pltpu

F32 = jnp.float32
BF16 = jnp.bfloat16

EPS = 1e-6
HEAD_DIM = 64
LANES = 128
A_HEADS = 4
A_DK = 128
B_HEADS = 8
C_PATTERNS = ((128, 1), (512, 4), (2048, 16))
C_HPG = 4
C_HEADS = len(C_PATTERNS) * C_HPG
D_HEADS = 8
D_KV_HEADS = 2
D_WINDOW = 128
WIN_BLOCK = 128
REL_BUCKETS = 32
REL_MAX_DIST = 2048
A_SUB = 16
VMEM_LIMIT = 60 * 1024 * 1024

_CQ, _CK, _CV = 16, 20, 24
AC_BLOCKS = 28
_BQ, _BK, _BV = 0, 4, 8
_DQ, _DK, _DV = 12, 16, 17
_C0Q, _C0K, _C0V = 18, 20, 22
BD_BLOCKS = 24
WIN_UNROLL = 4
D_HEAD_PERM = (0, 4, 1, 5, 2, 6, 3, 7)
SB_CHAINS = 2
SB_EXIT = 100.0


def _rms(x, g):
    ms = jnp.mean(x * x, axis=-1, keepdims=True)
    return x * lax.rsqrt(ms + EPS) * g


def _sigmoid(x):
    return 1.0 / (1.0 + jnp.exp(-x))


def _dot(a, b):
    return jnp.dot(a, b, preferred_element_type=F32)


def _dot_nt(a, b):
    return lax.dot_general(a, b, (((1,), (1,)), ((), ())), preferred_element_type=F32)


def _dot_tn(a, b):
    return lax.dot_general(a, b, (((0,), (0,)), ((), ())), preferred_element_type=F32)


def _split_bf16(x):
    hi = x.astype(BF16)
    lo = (x - hi.astype(F32)).astype(BF16)
    return hi, lo


def _ffn_kernel(x_ref, g_ref, w1_ref, w3_ref, w2_ref, o_ref, xn_ref):
    j = pl.program_id(1)

    @pl.when(j == 0)
    def _():
        xn_ref[...] = _rms(x_ref[...], g_ref[0:1, :]).astype(BF16)
        o_ref[...] = jnp.zeros_like(o_ref)

    xn = xn_ref[...]
    gate = _dot(xn, w1_ref[...])
    up = _dot(xn, w3_ref[...])
    act = (gate * _sigmoid(gate) * up).astype(BF16)
    o_ref[...] += _dot(act, w2_ref[...])

    @pl.when(j == pl.num_programs(1) - 1)
    def _():
        o_ref[...] = x_ref[...] + 0.5 * _rms(o_ref[...], g_ref[1:2, :])


def _ffn(h, norm, w13, w2, layer, tm, tf):
    M, D = h.shape
    F = w2.shape[1]
    nf = F // tf
    return pl.pallas_call(
        _ffn_kernel,
        grid=(M // tm, nf),
        in_specs=[
            pl.BlockSpec((tm, D), lambda i, j: (i, 0)),
            pl.BlockSpec((None, 2, D), lambda i, j: (layer, 0, 0)),
            pl.BlockSpec((None, D, tf), lambda i, j: (layer, 0, j)),
            pl.BlockSpec((None, D, tf), lambda i, j: (layer, 0, j + nf)),
            pl.BlockSpec((None, tf, D), lambda i, j: (layer, j, 0)),
        ],
        out_specs=pl.BlockSpec((tm, D), lambda i, j: (i, 0)),
        out_shape=jax.ShapeDtypeStruct((M, D), F32),
        scratch_shapes=[pltpu.VMEM((tm, D), BF16)],
        compiler_params=pltpu.CompilerParams(
            dimension_semantics=("parallel", "arbitrary"), vmem_limit_bytes=VMEM_LIMIT),
        name="ffn",
    )(h, norm, w13, w13, w2)


def _proj_kernel(x_ref, g_ref, w_ref, o32_ref, o16_ref, xn_ref, *, n32):
    j = pl.program_id(1)

    @pl.when(j == 0)
    def _():
        xn_ref[...] = _rms(x_ref[...], g_ref[0:1, :]).astype(BF16)

    @pl.when(j < n32)
    def _():
        o32_ref[...] = _dot(xn_ref[...], w_ref[...])

    @pl.when(j >= n32)
    def _():
        o16_ref[...] = _dot(xn_ref[...], w_ref[...]).astype(BF16)


def _proj(h, norm, w_in, layer, cols32, cols16, tm, tn):
    M, D = h.shape
    n32, n16 = cols32 // tn, cols16 // tn
    return pl.pallas_call(
        functools.partial(_proj_kernel, n32=n32),
        grid=(M // tm, n32 + n16),
        in_specs=[
            pl.BlockSpec((tm, D), lambda i, j: (i, 0)),
            pl.BlockSpec((None, 2, D), lambda i, j: (layer, 0, 0)),
            pl.BlockSpec((None, D, tn), lambda i, j: (layer, 0, j)),
        ],
        out_specs=[pl.BlockSpec((tm, tn), lambda i, j: (i, jnp.minimum(j, n32 - 1))),
                   pl.BlockSpec((tm, tn), lambda i, j: (i, jnp.maximum(j - n32, 0)))],
        out_shape=[jax.ShapeDtypeStruct((M, cols32), F32), jax.ShapeDtypeStruct((M, cols16), BF16)],
        scratch_shapes=[pltpu.VMEM((tm, D), BF16)],
        compiler_params=pltpu.CompilerParams(
            dimension_semantics=("parallel", "arbitrary"), vmem_limit_bytes=VMEM_LIMIT),
        name="proj",
    )(h, norm, w_in)


def _hgrn_kernel(q_ref, f_ref, i_ref, g_ref, lb_ref, ng_ref, o_ref, st_ref, kv_ref, sall_ref):
    tt = q_ref.shape[1]
    sub = A_SUB
    nb = tt // sub
    per = LANES // sub
    shape3 = (nb, sub, LANES)

    @pl.when(pl.program_id(2) == 0)
    def _():
        st_ref[...] = jnp.zeros_like(st_ref)

    lb = lb_ref[0]
    f = lb + (1.0 - lb) * _sigmoid(f_ref[0])
    k = 1.0 - f
    gl = jnp.log(f)
    r = lax.broadcasted_iota(jnp.int32, (LANES, LANES), 0)
    c = lax.broadcasted_iota(jnp.int32, (LANES, LANES), 1)
    tril = ((r // sub == c // sub) & (c <= r)).astype(BF16)
    cums = []
    for a in range(tt // LANES):
        hi, lo = _split_bf16(gl[a * LANES:(a + 1) * LANES])
        cums.append(_dot(tril, hi) + _dot(tril, lo))
    b = jnp.concatenate(cums, axis=0)
    q = q_ref[0]
    v = i_ref[0]
    b3, q3, k3, v3 = (x.reshape(shape3) for x in (b, q, k, v))

    ones = jnp.ones((LANES, LANES), BF16)
    rows = lax.broadcasted_iota(jnp.int32, (1, sub, LANES), 1)
    o3 = jnp.zeros(shape3, F32)
    for s in range(sub):
        e = jnp.exp(b3 - b3[:, s:s + 1, :])
        w = jnp.where(rows >= s, q3 * e * k3[:, s:s + 1, :], 0.0).astype(BF16)
        o3 = o3 + _dot(w.reshape(tt, LANES), ones).reshape(shape3) * v3[:, s:s + 1, :]

    b_last = b3[:, sub - 1:sub, :]
    kd = (k3 * jnp.exp(b_last - b3)).reshape(tt, LANES)
    dec = jnp.exp(b_last)
    for a in range(tt // LANES):
        v_t = v[a * LANES:(a + 1) * LANES].T.astype(BF16)
        kd_a = kd[a * LANES:(a + 1) * LANES]
        rhs = jnp.concatenate([jnp.where(r // sub == j, kd_a, 0.0).astype(BF16) for j in range(per)], axis=1)
        kv = _dot(v_t, rhs)
        for j in range(per):
            kv_ref[a * per + j] = kv[:, j * LANES:(j + 1) * LANES]

    st = st_ref[...]
    for n in range(nb):
        sall_ref[n] = st.astype(BF16)
        st = dec[n] * st + kv_ref[n]
    st_ref[...] = st

    qe = (q * jnp.exp(b)).astype(BF16)
    inter = []
    for a in range(tt // LANES):
        stack = sall_ref[a * per:(a + 1) * per].reshape(per * LANES, LANES)
        p = _dot_nt(qe[a * LANES:(a + 1) * LANES], stack)
        inter += [p[j * sub:(j + 1) * sub, j * LANES:(j + 1) * LANES] for j in range(per)]
    o = o3.reshape(tt, LANES) + jnp.concatenate(inter, axis=0)

    g = g_ref[0]
    o_ref[0] = (_rms(o, ng_ref[0]) * (g * _sigmoid(g))).astype(o_ref.dtype)


def _hgrn(pa, lb, ng, tt):
    Bn, T, _ = pa.shape
    H = A_HEADS

    def col(off):
        return pl.BlockSpec((1, tt, A_DK), lambda b, h, t: (b, t, off * H + h))

    return pl.pallas_call(
        _hgrn_kernel,
        grid=(Bn, H, T // tt),
        in_specs=[col(0), col(1), col(2), col(3),
                  pl.BlockSpec((1, 1, A_DK), lambda b, h, t: (h, 0, 0)),
                  pl.BlockSpec((1, 1, A_DK), lambda b, h, t: (0, 0, 0))],
        out_specs=pl.BlockSpec((1, tt, A_DK), lambda b, h, t: (b, t, h)),
        out_shape=jax.ShapeDtypeStruct((Bn, T, H * A_DK), BF16),
        scratch_shapes=[pltpu.VMEM((A_DK, A_DK), F32), pltpu.VMEM((tt // A_SUB, A_DK, A_DK), F32),
                        pltpu.VMEM((tt // A_SUB, A_DK, A_DK), BF16)],
        compiler_params=pltpu.CompilerParams(
            dimension_semantics=("parallel", "parallel", "arbitrary"), vmem_limit_bytes=VMEM_LIMIT),
        name="hgrn2",
    )(pa, pa, pa, pa, lb, ng)


def _sb_kernel(q_ref, k_ref, v_ref, o_ref, suf_ref, acc_ref, run_ref):
    tq = q_ref.shape[1]
    i = pl.program_id(2)
    in_h0 = lax.broadcasted_iota(jnp.int32, (tq, LANES), 1) < HEAD_DIM
    r = lax.broadcasted_iota(jnp.int32, (tq, tq), 0)
    c = lax.broadcasted_iota(jnp.int32, (tq, tq), 1)
    suf_ref[...] = (r >= c).astype(BF16)
    strict = jnp.concatenate([c < r, c < r], axis=0)

    def lanes(n):
        return slice(n * LANES, (n + 1) * LANES)

    q2s = []
    for n in range(SB_CHAINS):
        q = q_ref[0, :, lanes(n)]
        zero = jnp.zeros_like(q)
        q2s.append(jnp.concatenate([jnp.where(in_h0, q, zero), jnp.where(in_h0, zero, q)], axis=0)
                   * (HEAD_DIM ** -0.5))

    def tile(q2, kb, vb, run, diag):
        z = _dot_nt(q2, kb)
        sp = jnp.maximum(z, 0.0) + jnp.log(1.0 + jnp.exp(-jnp.abs(z)))
        if diag:
            sp = jnp.where(strict, sp, 0.0)
        hi, lo = _split_bf16(sp)
        suffix = suf_ref[...]
        cs = _dot(hi, suffix) + _dot(lo, suffix) + run
        a = jnp.exp(z - cs)
        if diag:
            a = jnp.where(strict, a, 0.0)
        return _dot(a.astype(BF16), vb), cs[:, 0:1]

    def step(rows, first):
        lowest = None
        for n in range(SB_CHAINS):
            run = jnp.zeros((2 * tq, 1), F32) if first else run_ref[n]
            pv, run = tile(q2s[n], k_ref[0, rows, lanes(n)], v_ref[0, rows, lanes(n)], run, first)
            acc_ref[n] = pv if first else acc_ref[n] + pv
            run_ref[n] = run
            low = jnp.min(run)
            lowest = low if lowest is None else jnp.minimum(lowest, low)
        return lowest

    lowest = step(pl.ds(pl.multiple_of(i * tq, tq), tq), True)

    def cond(carry):
        s, lowest = carry
        return (s < i) & (lowest <= SB_EXIT)

    def body(carry):
        s, _ = carry
        return s + 1, step(pl.ds(pl.multiple_of((i - 1 - s) * tq, tq), tq), False)

    lax.while_loop(cond, body, (jnp.int32(0), lowest))
    for n in range(SB_CHAINS):
        acc = acc_ref[n]
        o_ref[0, :, lanes(n)] = jnp.where(in_h0, acc[:tq], acc[tq:]).astype(o_ref.dtype)


def _stick_breaking(pbd, tq):
    Bn, T, _ = pbd.shape
    ngrp = B_HEADS // 2 // SB_CHAINS
    w = SB_CHAINS * LANES
    return pl.pallas_call(
        _sb_kernel,
        grid=(Bn, ngrp, T // tq),
        in_specs=[pl.BlockSpec((1, tq, w), lambda b, p, i: (b, i, _BQ // SB_CHAINS + p)),
                  pl.BlockSpec((1, T, w), lambda b, p, i: (b, 0, _BK // SB_CHAINS + p)),
                  pl.BlockSpec((1, T, w), lambda b, p, i: (b, 0, _BV // SB_CHAINS + p))],
        out_specs=pl.BlockSpec((1, tq, w), lambda b, p, i: (b, i, p)),
        out_shape=jax.ShapeDtypeStruct((Bn, T, B_HEADS // 2 * LANES), BF16),
        scratch_shapes=[pltpu.VMEM((tq, tq), BF16), pltpu.VMEM((SB_CHAINS, 2 * tq, LANES), F32),
                        pltpu.VMEM((SB_CHAINS, 2 * tq, 1), F32)],
        compiler_params=pltpu.CompilerParams(
            dimension_semantics=("parallel", "parallel", "arbitrary"), vmem_limit_bytes=VMEM_LIMIT),
        name="stick_breaking",
    )(pbd, pbd, pbd)


def _win_kernel(q_ref, k_ref, kp_ref, v_ref, vp_ref, bias_ref, sink_ref, *out_refs, dil, nsub, nq):
    blk = WIN_BLOCK
    span = blk * dil
    nrow = 2 * nq * blk
    i = pl.program_id(2)
    lane = lax.broadcasted_iota(jnp.int32, (blk, LANES), 1)
    in_h0 = lane < HEAD_DIM
    prev_col = lax.broadcasted_iota(jnp.int32, (nrow, 2 * blk), 1) < blk
    ones = jnp.ones((2 * blk, LANES), BF16)
    sinks = jnp.concatenate([jnp.broadcast_to(sink_ref[hh:hh + 1, 0:1], (blk, 1)) for hh in range(2 * nq)],
                            axis=0)
    bias = bias_ref[...]

    def rows_of(start):
        return pl.ds(start, blk, stride=dil) if dil > 1 else pl.ds(start, blk)

    def residue_class(s, c):
        cur = rows_of(s * span + c)
        q = q_ref[0, cur, :].astype(BF16)
        if s == 0:
            kp, vp = kp_ref[0, rows_of(c), :], vp_ref[0, rows_of(c), :]
        else:
            prev = rows_of((s - 1) * span + c)
            kp, vp = k_ref[0, prev, :], v_ref[0, prev, :]
        k2 = jnp.concatenate([kp.astype(BF16), k_ref[0, cur, :].astype(BF16)], axis=0)
        v2 = jnp.concatenate([vp.astype(BF16), v_ref[0, cur, :].astype(BF16)], axis=0)
        qs = [q[:, p * LANES:(p + 1) * LANES] for p in range(nq)]
        zero = jnp.zeros_like(qs[0])
        q2 = jnp.concatenate([jnp.where(in_h0, x, zero) for x in qs]
                             + [jnp.where(in_h0, zero, x) for x in qs], axis=0)
        logits = _dot_nt(q2 * (HEAD_DIM ** -0.5), k2) + bias
        if s == 0:
            logits = jnp.where(jnp.logical_and(i == 0, prev_col), -jnp.inf, logits)
        m = jnp.max(jnp.maximum(logits[:, :blk], logits[:, blk:]), axis=-1, keepdims=True)
        m = jnp.maximum(m, sinks)
        p = jnp.exp(logits - m).astype(BF16)
        if nq > 1:
            pv = _dot(p, jnp.concatenate([v2, ones], axis=1))
            num, den = pv[:, :LANES], pv[:, LANES:]
        else:
            num, den = _dot(p, v2), _dot(p, ones)
        den = den + jnp.exp(sinks - m)
        o2 = num / den

        def lane_blocks(x):
            return jnp.concatenate([jnp.where(in_h0, x[b * blk:(b + 1) * blk], x[(nq + b) * blk:(nq + b + 1) * blk])
                                    for b in range(nq)], axis=1)

        out_refs[0][0, cur, :] = lane_blocks(o2).astype(out_refs[0].dtype)
        if len(out_refs) > 1:
            out_refs[1][0, cur, :] = lane_blocks(m + jnp.log(den))

    for s in range(nsub):
        if dil <= WIN_UNROLL:
            for c in range(dil):
                residue_class(s, c)
        else:
            def body(c, carry, s=s):
                residue_class(s, c)
                return carry
            lax.fori_loop(0, dil, body, 0, unroll=WIN_UNROLL)


def _window_attn(slab, bias, sinks, dil, qcol, kcol, vcol, ngrp, nq, with_lse, out_dtype):
    Bn, T, _ = slab.shape
    span = WIN_BLOCK * dil
    nsub = max(1, WIN_UNROLL // dil)
    tb = span * nsub
    assert qcol % nq == 0

    def kvmap(col, prev):
        def f(b, g, i):
            return (b, jnp.maximum(i * nsub - 1, 0) if prev else i, col + g)
        return f

    qblk = (1, tb, nq * LANES)
    blk = (1, tb, LANES)
    pblk = (1, span, LANES)
    out_spec = pl.BlockSpec(qblk, lambda b, g, i: (b, i, g))
    out_sds = jax.ShapeDtypeStruct((Bn, T, ngrp * nq * LANES), out_dtype)
    lse_sds = jax.ShapeDtypeStruct((Bn, T, ngrp * nq * LANES), F32)
    return pl.pallas_call(
        functools.partial(_win_kernel, dil=dil, nsub=nsub, nq=nq),
        grid=(Bn, ngrp, T // tb),
        in_specs=[pl.BlockSpec(qblk, lambda b, g, i: (b, i, qcol // nq + g)),
                  pl.BlockSpec(blk, kvmap(kcol, False)), pl.BlockSpec(pblk, kvmap(kcol, True)),
                  pl.BlockSpec(blk, kvmap(vcol, False)), pl.BlockSpec(pblk, kvmap(vcol, True)),
                  pl.BlockSpec((None, 2 * nq * WIN_BLOCK, 2 * WIN_BLOCK), lambda b, g, i: (g, 0, 0)),
                  pl.BlockSpec((None, 2 * nq, LANES), lambda b, g, i: (g, 0, 0))],
        out_specs=[out_spec, out_spec] if with_lse else [out_spec],
        out_shape=[out_sds, lse_sds] if with_lse else [out_sds],
        compiler_params=pltpu.CompilerParams(
            dimension_semantics=("parallel", "parallel", "arbitrary"), vmem_limit_bytes=VMEM_LIMIT),
        name="window_attn",
    )(slab, slab, slab, slab, slab, bias, sinks)


def _combine_kernel(o0, o1, o2, l0, l1, l2, out_ref):
    a, b, c = l0[...], l1[...], l2[...]
    m = jnp.maximum(jnp.maximum(a, b), c)
    ea, eb, ec = jnp.exp(a - m), jnp.exp(b - m), jnp.exp(c - m)
    out_ref[...] = ((ea * o0[...] + eb * o1[...] + ec * o2[...]) / (ea + eb + ec)).astype(out_ref.dtype)


def _combine(outs, lses, tm):
    M, W = outs[0].shape
    spec = pl.BlockSpec((tm, W), lambda i: (i, 0))
    return pl.pallas_call(
        _combine_kernel,
        grid=(M // tm,),
        in_specs=[spec] * 6,
        out_specs=spec,
        out_shape=jax.ShapeDtypeStruct((M, W), BF16),
        compiler_params=pltpu.CompilerParams(dimension_semantics=("parallel",)),
        name="group_combine",
    )(*outs, *lses)


def _merge_kernel(x_ref, g_ref, ya_ref, yb_ref, yc_ref, yd_ref,
                  ga_ref, gb_ref, gc_ref, gd_ref, wa_ref, wb_ref, wc_ref, wd_ref, wo_ref,
                  o_ref, u_ref):
    j = pl.program_id(1)

    @pl.when(j == 0)
    def _():
        u_ref[...] = _rms(x_ref[...], g_ref[0:1, :]).astype(BF16)
        o_ref[...] = jnp.zeros_like(o_ref)

    u = u_ref[...]
    merged = None
    for y_ref, gate_ref, w_ref in ((ya_ref, ga_ref, wa_ref), (yb_ref, gb_ref, wb_ref),
                                   (yc_ref, gc_ref, wc_ref), (yd_ref, gd_ref, wd_ref)):
        term = _sigmoid(_dot(u, gate_ref[...])) * _dot(y_ref[...], w_ref[...])
        merged = term if merged is None else merged + term
    o_ref[...] += _dot(merged.astype(BF16), wo_ref[...])

    @pl.when(j == pl.num_programs(1) - 1)
    def _():
        o_ref[...] = x_ref[...] + _rms(o_ref[...], g_ref[1:2, :])


def _merge(h, norm, ys, w_in, gate_col0, w_branch, w_out, layer, tm, tn):
    M, D = h.shape
    gcb = gate_col0 // tn
    dcb = D // tn

    def yspec(w):
        return pl.BlockSpec((tm, w), lambda i, j: (i, 0))

    def gspec(br):
        return pl.BlockSpec((None, D, tn), lambda i, j: (layer, 0, gcb + br * dcb + j))

    def wspec(rows, rb):
        return pl.BlockSpec((None, rows, tn), lambda i, j: (layer, rb, j))

    wa, wb, wc, wd = (y.shape[1] for y in ys)
    return pl.pallas_call(
        _merge_kernel,
        grid=(M // tm, dcb),
        in_specs=[pl.BlockSpec((tm, D), lambda i, j: (i, 0)),
                  pl.BlockSpec((None, 2, D), lambda i, j: (layer, 0, 0)),
                  yspec(wa), yspec(wb), yspec(wc), yspec(wd),
                  gspec(0), gspec(1), gspec(2), gspec(3),
                  wspec(wa, 0), wspec(wb, wa // wb), wspec(wc, (wa + wb + wd) // wc),
                  wspec(wd, (wa + wb) // wd),
                  pl.BlockSpec((None, tn, D), lambda i, j: (layer, j, 0))],
        out_specs=pl.BlockSpec((tm, D), lambda i, j: (i, 0)),
        out_shape=jax.ShapeDtypeStruct((M, D), F32),
        scratch_shapes=[pltpu.VMEM((tm, D), BF16)],
        compiler_params=pltpu.CompilerParams(
            dimension_semantics=("parallel", "arbitrary"), vmem_limit_bytes=VMEM_LIMIT),
        name="gated_merge",
    )(h, norm, *ys, w_in, w_in, w_in, w_in, w_branch, w_branch, w_branch, w_branch, w_out)


def _rel_bucket(dist):
    max_exact = REL_BUCKETS // 2
    d = jnp.maximum(dist, 1).astype(F32)
    large = max_exact + (jnp.log(d / max_exact) / math.log(REL_MAX_DIST / max_exact)
                         * (REL_BUCKETS - max_exact)).astype(jnp.int32)
    large = jnp.minimum(large, REL_BUCKETS - 1)
    return jnp.where(dist < max_exact, dist, large)


def _window_bias(rel_bias, heads, dilation, max_dist, rows_heads=2):
    row = jnp.arange(WIN_BLOCK)[:, None]
    col = jnp.arange(2 * WIN_BLOCK)[None, :]
    dist = row + WIN_BLOCK - col
    valid = (dist >= 0) & (dist <= max_dist)
    assert list(heads) == list(range(heads[0], heads[-1] + 1))
    bucket = _rel_bucket(jnp.maximum(dist, 0) * dilation)
    per_head = rel_bias.astype(F32)[:, heads[0]:heads[-1] + 1]
    hit = bucket[None, None] == jnp.arange(REL_BUCKETS)[:, None, None, None]
    tab = jnp.sum(jnp.where(hit, per_head[:, :, None, None], 0.0), axis=0)
    tab = jnp.where(valid[None], tab, -jnp.inf)
    return tab.reshape(len(heads) // rows_heads, rows_heads * WIN_BLOCK, 2 * WIN_BLOCK)


def _trunk(x, ffn1_norm, ffn1_w13, ffn1_w2, mix_norm, w_in, hgrn_lb_logits, hgrn_out_norm,
           attn_sinks, w_branch, w_out, ffn2_norm, ffn2_w13, ffn2_w2, rel_bias, cfg):
    Bn, T, D = x.shape
    depth = w_in.shape[0]
    M = Bn * T
    a_w = A_HEADS * A_DK
    a_cols = 4 * a_w
    b_cols = 3 * B_HEADS * HEAD_DIM
    c_cols = 3 * C_HEADS * HEAD_DIM
    ac_cols = AC_BLOCKS * LANES
    bd_cols = BD_BLOCKS * LANES
    mix_in = ac_cols + bd_cols

    lb_sm = jax.nn.softmax(hgrn_lb_logits.astype(F32), axis=0)
    lb_all = (jnp.cumsum(lb_sm, axis=0) - lb_sm[0:1]).reshape(depth, A_HEADS, 1, A_DK)

    b0, c0 = a_cols, a_cols + b_cols
    dq0 = c0 + c_cols
    dk0 = dq0 + D_HEADS * HEAD_DIM
    c_w = C_HPG * HEAD_DIM

    perm = jnp.array([h * HEAD_DIM + d for h in D_HEAD_PERM for d in range(HEAD_DIM)])

    def c_part(part, g0, g1):
        base = c0 + part * C_HEADS * HEAD_DIM
        return w_in[:, :, base + g0 * c_w:base + g1 * c_w]

    n_grp = len(C_PATTERNS)
    w_in_b = jnp.concatenate(
        [w_in[:, :, :a_cols]] + [c_part(p, 1, n_grp) for p in range(3)]
        + [w_in[:, :, b0:c0], w_in[:, :, dq0:dk0][:, :, perm], w_in[:, :, dk0:mix_in]]
        + [c_part(p, 0, 1) for p in range(3)] + [w_in[:, :, mix_in:]], axis=-1).astype(BF16)
    d0 = 2 * a_w + c_w
    w_br_b = jnp.concatenate(
        [w_branch[:, :2 * a_w], w_branch[:, d0:][:, perm], w_branch[:, 2 * a_w:d0]], axis=1).astype(BF16)
    w_out_b = w_out.astype(BF16)
    f1_w13, f1_w2 = ffn1_w13.astype(BF16), ffn1_w2.astype(BF16)
    f2_w13, f2_w2 = ffn2_w13.astype(BF16), ffn2_w2.astype(BF16)

    c_bias = [_window_bias(rel_bias, list(range(g * C_HPG, (g + 1) * C_HPG)), dil, win // dil)
              for g, (win, dil) in enumerate(C_PATTERNS)]
    d_bias = _window_bias(rel_bias, [C_HEADS + h for h in range(D_HEADS)], 1, D_WINDOW - 1, D_HEADS)
    no_sink = jnp.full((C_HPG // 2, 2, LANES), -jnp.inf, F32)

    h = x.reshape(M, D)
    for l in range(depth):
        h = _ffn(h, ffn1_norm, f1_w13, f1_w2, l, cfg["ffn_tm"], cfg["ffn_tf"])

        pac, pbd = _proj(h, mix_norm, w_in_b, l, ac_cols, bd_cols, cfg["proj_tm"], cfg["proj_tn"])
        pac = pac.reshape(Bn, T, ac_cols)
        pbd = pbd.reshape(Bn, T, bd_cols)

        ya = _hgrn(pac, lb_all[l], hgrn_out_norm[l].reshape(1, 1, A_DK).astype(F32), cfg["hgrn_tt"])
        yb = _stick_breaking(pbd, cfg["sb_tq"])
        c_out, c_lse = [], []
        for g, (win, dil) in enumerate(C_PATTERNS):
            if dil == 1:
                slab, cols = pbd, (_C0Q, _C0K, _C0V)
            else:
                slab, cols = pac, tuple(c + 2 * (g - 1) for c in (_CQ, _CK, _CV))
            o, lse = _window_attn(slab, c_bias[g], no_sink, dil, *cols, C_HPG // 2, 1, True, F32)
            c_out.append(o.reshape(M, c_w))
            c_lse.append(lse.reshape(M, c_w))
        yc = _combine(c_out, c_lse, cfg["comb_tm"])
        sinks = jnp.broadcast_to(attn_sinks[l].astype(F32).reshape(1, D_HEADS, 1), (1, D_HEADS, LANES))
        (yd,) = _window_attn(pbd, d_bias, sinks, 1, _DQ, _DK, _DV, 1, D_HEADS // 2, False, BF16)

        ys = (ya.reshape(M, a_w), yb.reshape(M, a_w), yc, yd.reshape(M, a_w))
        h = _merge(h, mix_norm, ys, w_in_b, mix_in, w_br_b, w_out_b, l, cfg["merge_tm"], cfg["merge_tn"])

        h = _ffn(h, ffn2_norm, f2_w13, f2_w2, l, cfg["ffn_tm"], cfg["ffn_tf"])
    return h.reshape(Bn, T, D)


_CFG = dict(ffn_tm=512, ffn_tf=512, proj_tm=512, proj_tn=512, hgrn_tt=512, sb_tq=256,
            comb_tm=1024, merge_tm=512, merge_tn=512)


def kernel(x, ffn1_norm, ffn1_w13, ffn1_w2, mix_norm, w_in, hgrn_lb_logits, hgrn_out_norm, attn_sinks,
           w_branch, w_out, ffn2_norm, ffn2_w13, ffn2_w2, rel_bias):
    return _trunk(x, ffn1_norm, ffn1_w13, ffn1_w2, mix_norm, w_in, hgrn_lb_logits, hgrn_out_norm,
                  attn_sinks, w_branch, w_out, ffn2_norm, ffn2_w13, ffn2_w2, rel_bias, _CFG)
```

```python
import functools
import math

import jax
import jax.numpy as jnp
from jax import lax
from jax.experimental import pallas as pl
from jax.experimental.pallas import tpu as pltpu

F32 = jnp.float32
BF16 = jnp.bfloat16

EPS = 1e-6
HEAD_DIM = 64
LANES = 128
A_HEADS = 4
A_DK = 128
B_HEADS = 8
C_PATTERNS = ((128, 1), (512, 4), (2048, 16))
C_HPG = 4
C_HEADS = len(C_PATTERNS) * C_HPG
D_HEADS = 8
D_KV_HEADS = 2
D_WINDOW = 128
WIN_BLOCK = 128
REL_BUCKETS = 32
REL_MAX_DIST = 2048
A_SUB = 16
VMEM_LIMIT = 60 * 1024 * 1024

_CQ, _CK, _CV = 16, 20, 24
AC_BLOCKS = 28
_BQ, _BK, _BV = 0, 4, 8
_DQ, _DK, _DV = 12, 16, 17
_C0Q, _C0K, _C0V = 18, 20, 22
BD_BLOCKS = 24
WIN_UNROLL = 4
D_HEAD_PERM = (0, 4, 1, 5, 2, 6, 3, 7)
SB_CHAINS = 2
SB_EXIT = 100.0


def _rms(x, g):
    ms = jnp.mean(x * x, axis=-1, keepdims=True)
    return x * lax.rsqrt(ms + EPS) * g


def _sigmoid(x):
    return 1.0 / (1.0 + jnp.exp(-x))


def _dot(a, b):
    return jnp.dot(a, b, preferred_element_type=F32)


def _dot_nt(a, b):
    return lax.dot_general(a, b, (((1,), (1,)), ((), ())), preferred_element_type=F32)


def _dot_tn(a, b):
    return lax.dot_general(a, b, (((0,), (0,)), ((), ())), preferred_element_type=F32)


def _split_bf16(x):
    hi = x.astype(BF16)
    lo = (x - hi.astype(F32)).astype(BF16)
    return hi, lo


def _ffn_kernel(x_ref, g_ref, w1_ref, w3_ref, w2_ref, o_ref, xn_ref):
    j = pl.program_id(1)

    @pl.when(j == 0)
    def _():
        xn_ref[...] = _rms(x_ref[...], g_ref[0:1, :]).astype(BF16)
        o_ref[...] = jnp.zeros_like(o_ref)

    xn = xn_ref[...]
    gate = _dot(xn, w1_ref[...])
    up = _dot(xn, w3_ref[...])
    act = (gate * _sigmoid(gate) * up).astype(BF16)
    o_ref[...] += _dot(act, w2_ref[...])

    @pl.when(j == pl.num_programs(1) - 1)
    def _():
        o_ref[...] = x_ref[...] + 0.5 * _rms(o_ref[...], g_ref[1:2, :])


def _ffn(h, norm, w13, w2, layer, tm, tf):
    M, D = h.shape
    F = w2.shape[1]
    nf = F // tf
    return pl.pallas_call(
        _ffn_kernel,
        grid=(M // tm, nf),
        in_specs=[
            pl.BlockSpec((tm, D), lambda i, j: (i, 0)),
            pl.BlockSpec((None, 2, D), lambda i, j: (layer, 0, 0)),
            pl.BlockSpec((None, D, tf), lambda i, j: (layer, 0, j)),
            pl.BlockSpec((None, D, tf), lambda i, j: (layer, 0, j + nf)),
            pl.BlockSpec((None, tf, D), lambda i, j: (layer, j, 0)),
        ],
        out_specs=pl.BlockSpec((tm, D), lambda i, j: (i, 0)),
        out_shape=jax.ShapeDtypeStruct((M, D), F32),
        scratch_shapes=[pltpu.VMEM((tm, D), BF16)],
        compiler_params=pltpu.CompilerParams(
            dimension_semantics=("parallel", "arbitrary"), vmem_limit_bytes=VMEM_LIMIT),
        name="ffn",
    )(h, norm, w13, w13, w2)


def _proj_kernel(x_ref, g_ref, w_ref, o32_ref, o16_ref, xn_ref, *, n32):
    j = pl.program_id(1)

    @pl.when(j == 0)
    def _():
        xn_ref[...] = _rms(x_ref[...], g_ref[0:1, :]).astype(BF16)

    @pl.when(j < n32)
    def _():
        o32_ref[...] = _dot(xn_ref[...], w_ref[...])

    @pl.when(j >= n32)
    def _():
        o16_ref[...] = _dot(xn_ref[...], w_ref[...]).astype(BF16)


def _proj(h, norm, w_in, layer, cols32, cols16, tm, tn):
    M, D = h.shape
    n32, n16 = cols32 // tn, cols16 // tn
    return pl.pallas_call(
        functools.partial(_proj_kernel, n32=n32),
        grid=(M // tm, n32 + n16),
        in_specs=[
            pl.BlockSpec((tm, D), lambda i, j: (i, 0)),
            pl.BlockSpec((None, 2, D), lambda i, j: (layer, 0, 0)),
            pl.BlockSpec((None, D, tn), lambda i, j: (layer, 0, j)),
        ],
        out_specs=[pl.BlockSpec((tm, tn), lambda i, j: (i, jnp.minimum(j, n32 - 1))),
                   pl.BlockSpec((tm, tn), lambda i, j: (i, jnp.maximum(j - n32, 0)))],
        out_shape=[jax.ShapeDtypeStruct((M, cols32), F32), jax.ShapeDtypeStruct((M, cols16), BF16)],
        scratch_shapes=[pltpu.VMEM((tm, D), BF16)],
        compiler_params=pltpu.CompilerParams(
            dimension_semantics=("parallel", "arbitrary"), vmem_limit_bytes=VMEM_LIMIT),
        name="proj",
    )(h, norm, w_in)


def _hgrn_kernel(q_ref, f_ref, i_ref, g_ref, lb_ref, ng_ref, o_ref, st_ref, kv_ref, sall_ref):
    tt = q_ref.shape[1]
    sub = A_SUB
    nb = tt // sub
    per = LANES // sub
    shape3 = (nb, sub, LANES)

    @pl.when(pl.program_id(2) == 0)
    def _():
        st_ref[...] = jnp.zeros_like(st_ref)

    lb = lb_ref[0]
    f = lb + (1.0 - lb) * _sigmoid(f_ref[0])
    k = 1.0 - f
    gl = jnp.log(f)
    r = lax.broadcasted_iota(jnp.int32, (LANES, LANES), 0)
    c = lax.broadcasted_iota(jnp.int32, (LANES, LANES), 1)
    tril = ((r // sub == c // sub) & (c <= r)).astype(BF16)
    cums = []
    for a in range(tt // LANES):
        hi, lo = _split_bf16(gl[a * LANES:(a + 1) * LANES])
        cums.append(_dot(tril, hi) + _dot(tril, lo))
    b = jnp.concatenate(cums, axis=0)
    q = q_ref[0]
    v = i_ref[0]
    b3, q3, k3, v3 = (x.reshape(shape3) for x in (b, q, k, v))

    ones = jnp.ones((LANES, LANES), BF16)
    rows = lax.broadcasted_iota(jnp.int32, (1, sub, LANES), 1)
    o3 = jnp.zeros(shape3, F32)
    for s in range(sub):
        e = jnp.exp(b3 - b3[:, s:s + 1, :])
        w = jnp.where(rows >= s, q3 * e * k3[:, s:s + 1, :], 0.0).astype(BF16)
        o3 = o3 + _dot(w.reshape(tt, LANES), ones).reshape(shape3) * v3[:, s:s + 1, :]

    b_last = b3[:, sub - 1:sub, :]
    kd = (k3 * jnp.exp(b_last - b3)).reshape(tt, LANES)
    dec = jnp.exp(b_last)
    for a in range(tt // LANES):
        v_t = v[a * LANES:(a + 1) * LANES].T.astype(BF16)
        kd_a = kd[a * LANES:(a + 1) * LANES]
        rhs = jnp.concatenate([jnp.where(r // sub == j, kd_a, 0.0).astype(BF16) for j in range(per)], axis=1)
        kv = _dot(v_t, rhs)
        for j in range(per):
            kv_ref[a * per + j] = kv[:, j * LANES:(j + 1) * LANES]

    st = st_ref[...]
    for n in range(nb):
        sall_ref[n] = st.astype(BF16)
        st = dec[n] * st + kv_ref[n]
    st_ref[...] = st

    qe = (q * jnp.exp(b)).astype(BF16)
    inter = []
    for a in range(tt // LANES):
        stack = sall_ref[a * per:(a + 1) * per].reshape(per * LANES, LANES)
        p = _dot_nt(qe[a * LANES:(a + 1) * LANES], stack)
        inter += [p[j * sub:(j + 1) * sub, j * LANES:(j + 1) * LANES] for j in range(per)]
    o = o3.reshape(tt, LANES) + jnp.concatenate(inter, axis=0)

    g = g_ref[0]
    o_ref[0] = (_rms(o, ng_ref[0]) * (g * _sigmoid(g))).astype(o_ref.dtype)


def _hgrn(pa, lb, ng, tt):
    Bn, T, _ = pa.shape
    H = A_HEADS

    def col(off):
        return pl.BlockSpec((1, tt, A_DK), lambda b, h, t: (b, t, off * H + h))

    return pl.pallas_call(
        _hgrn_kernel,
        grid=(Bn, H, T // tt),
        in_specs=[col(0), col(1), col(2), col(3),
                  pl.BlockSpec((1, 1, A_DK), lambda b, h, t: (h, 0, 0)),
                  pl.BlockSpec((1, 1, A_DK), lambda b, h, t: (0, 0, 0))],
        out_specs=pl.BlockSpec((1, tt, A_DK), lambda b, h, t: (b, t, h)),
        out_shape=jax.ShapeDtypeStruct((Bn, T, H * A_DK), BF16),
        scratch_shapes=[pltpu.VMEM((A_DK, A_DK), F32), pltpu.VMEM((tt // A_SUB, A_DK, A_DK), F32),
                        pltpu.VMEM((tt // A_SUB, A_DK, A_DK), BF16)],
        compiler_params=pltpu.CompilerParams(
            dimension_semantics=("parallel", "parallel", "arbitrary"), vmem_limit_bytes=VMEM_LIMIT),
        name="hgrn2",
    )(pa, pa, pa, pa, lb, ng)


def _sb_kernel(q_ref, k_ref, v_ref, o_ref, suf_ref, acc_ref, run_ref):
    tq = q_ref.shape[1]
    i = pl.program_id(2)
    in_h0 = lax.broadcasted_iota(jnp.int32, (tq, LANES), 1) < HEAD_DIM
    r = lax.broadcasted_iota(jnp.int32, (tq, tq), 0)
    c = lax.broadcasted_iota(jnp.int32, (tq, tq), 1)
    suf_ref[...] = (r >= c).astype(BF16)
    strict = jnp.concatenate([c < r, c < r], axis=0)

    def lanes(n):
        return slice(n * LANES, (n + 1) * LANES)

    q2s = []
    for n in range(SB_CHAINS):
        q = q_ref[0, :, lanes(n)]
        zero = jnp.zeros_like(q)
        q2s.append(jnp.concatenate([jnp.where(in_h0, q, zero), jnp.where(in_h0, zero, q)], axis=0)
                   * (HEAD_DIM ** -0.5))

    def tile(q2, kb, vb, run, diag):
        z = _dot_nt(q2, kb)
        sp = jnp.maximum(z, 0.0) + jnp.log(1.0 + jnp.exp(-jnp.abs(z)))
        if diag:
            sp = jnp.where(strict, sp, 0.0)
        hi, lo = _split_bf16(sp)
        suffix = suf_ref[...]
        cs = _dot(hi, suffix) + _dot(lo, suffix) + run
        a = jnp.exp(z - cs)
        if diag:
            a = jnp.where(strict, a, 0.0)
        return _dot(a.astype(BF16), vb), cs[:, 0:1]

    def step(rows, first):
        lowest = None
        for n in range(SB_CHAINS):
            run = jnp.zeros((2 * tq, 1), F32) if first else run_ref[n]
            pv, run = tile(q2s[n], k_ref[0, rows, lanes(n)], v_ref[0, rows, lanes(n)], run, first)
            acc_ref[n] = pv if first else acc_ref[n] + pv
            run_ref[n] = run
            low = jnp.min(run)
            lowest = low if lowest is None else jnp.minimum(lowest, low)
        return lowest

    lowest = step(pl.ds(pl.multiple_of(i * tq, tq), tq), True)

    def cond(carry):
        s, lowest = carry
        return (s < i) & (lowest <= SB_EXIT)

    def body(carry):
        s, _ = carry
        return s + 1, step(pl.ds(pl.multiple_of((i - 1 - s) * tq, tq), tq), False)

    lax.while_loop(cond, body, (jnp.int32(0), lowest))
    for n in range(SB_CHAINS):
        acc = acc_ref[n]
        o_ref[0, :, lanes(n)] = jnp.where(in_h0, acc[:tq], acc[tq:]).astype(o_ref.dtype)


def _stick_breaking(pbd, tq):
    Bn, T, _ = pbd.shape
    ngrp = B_HEADS // 2 // SB_CHAINS
    w = SB_CHAINS * LANES
    return pl.pallas_call(
        _sb_kernel,
        grid=(Bn, ngrp, T // tq),
        in_specs=[pl.BlockSpec((1, tq, w), lambda b, p, i: (b, i, _BQ // SB_CHAINS + p)),
                  pl.BlockSpec((1, T, w), lambda b, p, i: (b, 0, _BK // SB_CHAINS + p)),
                  pl.BlockSpec((1, T, w), lambda b, p, i: (b, 0, _BV // SB_CHAINS + p))],
        out_specs=pl.BlockSpec((1, tq, w), lambda b, p, i: (b, i, p)),
        out_shape=jax.ShapeDtypeStruct((Bn, T, B_HEADS // 2 * LANES), BF16),
        scratch_shapes=[pltpu.VMEM((tq, tq), BF16), pltpu.VMEM((SB_CHAINS, 2 * tq, LANES), F32),
                        pltpu.VMEM((SB_CHAINS, 2 * tq, 1), F32)],
        compiler_params=pltpu.CompilerParams(
            dimension_semantics=("parallel", "parallel", "arbitrary"), vmem_limit_bytes=VMEM_LIMIT),
        name="stick_breaking",
    )(pbd, pbd, pbd)


def _win_kernel(q_ref, k_ref, kp_ref, v_ref, vp_ref, bias_ref, sink_ref, *rest, dil, nsub, nq, n_mix):
    mix_refs, out_refs = rest[:2 * n_mix], rest[2 * n_mix + (1 if n_mix else 0):]
    blk = WIN_BLOCK
    span = blk * dil
    nrow = 2 * nq * blk
    i = pl.program_id(2)
    lane = lax.broadcasted_iota(jnp.int32, (blk, LANES), 1)
    in_h0 = lane < HEAD_DIM
    prev_col = lax.broadcasted_iota(jnp.int32, (nrow, 2 * blk), 1) < blk
    ones = jnp.ones((2 * blk, LANES), BF16)
    sinks = jnp.concatenate([jnp.broadcast_to(sink_ref[hh:hh + 1, 0:1], (blk, 1)) for hh in range(2 * nq)],
                            axis=0)
    bias = bias_ref[...]

    def rows_of(start):
        return pl.ds(start, blk, stride=dil) if dil > 1 else pl.ds(start, blk)

    def residue_class(s, c):
        cur = rows_of(s * span + c)
        q = q_ref[0, cur, :].astype(BF16)
        if s == 0:
            kp, vp = kp_ref[0, rows_of(c), :], vp_ref[0, rows_of(c), :]
        else:
            prev = rows_of((s - 1) * span + c)
            kp, vp = k_ref[0, prev, :], v_ref[0, prev, :]
        k2 = jnp.concatenate([kp.astype(BF16), k_ref[0, cur, :].astype(BF16)], axis=0)
        v2 = jnp.concatenate([vp.astype(BF16), v_ref[0, cur, :].astype(BF16)], axis=0)
        qs = [q[:, p * LANES:(p + 1) * LANES] for p in range(nq)]
        zero = jnp.zeros_like(qs[0])
        q2 = jnp.concatenate([jnp.where(in_h0, x, zero) for x in qs]
                             + [jnp.where(in_h0, zero, x) for x in qs], axis=0)
        logits = _dot_nt(q2 * (HEAD_DIM ** -0.5), k2) + bias
        if s == 0:
            logits = jnp.where(jnp.logical_and(i == 0, prev_col), -jnp.inf, logits)
        m = jnp.max(jnp.maximum(logits[:, :blk], logits[:, blk:]), axis=-1, keepdims=True)
        m = jnp.maximum(m, sinks)
        p = jnp.exp(logits - m).astype(BF16)
        if nq > 1:
            pv = _dot(p, jnp.concatenate([v2, ones], axis=1))
            num, den = pv[:, :LANES], pv[:, LANES:]
        else:
            num, den = _dot(p, v2), _dot(p, ones)
        den = den + jnp.exp(sinks - m)
        o2 = num / den

        def lane_blocks(x):
            return jnp.concatenate([jnp.where(in_h0, x[b * blk:(b + 1) * blk], x[(nq + b) * blk:(nq + b + 1) * blk])
                                    for b in range(nq)], axis=1)

        out_refs[0][0, cur, :] = lane_blocks(o2).astype(out_refs[0].dtype)
        if len(out_refs) > 1:
            out_refs[1][0, cur, :] = lane_blocks(m + jnp.log(den))

    for s in range(nsub):
        if dil <= WIN_UNROLL:
            for c in range(dil):
                residue_class(s, c)
        else:
            def body(c, carry, s=s):
                residue_class(s, c)
                return carry
            lax.fori_loop(0, dil, body, 0, unroll=WIN_UNROLL)

    if n_mix:
        outs = [r[0] for r in mix_refs[0::2]] + [out_refs[0][0]]
        lses = [r[0] for r in mix_refs[1::2]] + [out_refs[1][0]]
        top = functools.reduce(jnp.maximum, lses)
        ws = [jnp.exp(l - top) for l in lses]
        y_ref = rest[2 * n_mix]
        y_ref[0] = (sum(w * o for w, o in zip(ws, outs)) / sum(ws)).astype(y_ref.dtype)


def _window_attn(slab, bias, sinks, dil, qcol, kcol, vcol, ngrp, nq, with_lse, out_dtype, mix=()):
    Bn, T, _ = slab.shape
    span = WIN_BLOCK * dil
    nsub = max(1, WIN_UNROLL // dil)
    tb = span * nsub
    assert qcol % nq == 0

    def kvmap(col, prev):
        def f(b, g, i):
            return (b, jnp.maximum(i * nsub - 1, 0) if prev else i, col + g)
        return f

    qblk = (1, tb, nq * LANES)
    blk = (1, tb, LANES)
    pblk = (1, span, LANES)
    out_spec = pl.BlockSpec(qblk, lambda b, g, i: (b, i, g))
    out_sds = jax.ShapeDtypeStruct((Bn, T, ngrp * nq * LANES), out_dtype)
    lse_sds = jax.ShapeDtypeStruct((Bn, T, ngrp * nq * LANES), F32)
    if mix:
        out_specs, out_shape = [out_spec], [out_sds]
        scratch = [pltpu.VMEM(qblk, F32), pltpu.VMEM(qblk, F32)]
    else:
        out_specs = [out_spec, out_spec] if with_lse else [out_spec]
        out_shape = [out_sds, lse_sds] if with_lse else [out_sds]
        scratch = []
    return pl.pallas_call(
        functools.partial(_win_kernel, dil=dil, nsub=nsub, nq=nq, n_mix=len(mix) // 2),
        grid=(Bn, ngrp, T // tb),
        in_specs=[pl.BlockSpec(qblk, lambda b, g, i: (b, i, qcol // nq + g)),
                  pl.BlockSpec(blk, kvmap(kcol, False)), pl.BlockSpec(pblk, kvmap(kcol, True)),
                  pl.BlockSpec(blk, kvmap(vcol, False)), pl.BlockSpec(pblk, kvmap(vcol, True)),
                  pl.BlockSpec((None, 2 * nq * WIN_BLOCK, 2 * WIN_BLOCK), lambda b, g, i: (g, 0, 0)),
                  pl.BlockSpec((None, 2 * nq, LANES), lambda b, g, i: (g, 0, 0))] + [out_spec] * len(mix),
        out_specs=out_specs,
        out_shape=out_shape,
        scratch_shapes=scratch,
        compiler_params=pltpu.CompilerParams(
            dimension_semantics=("parallel", "parallel", "arbitrary"), vmem_limit_bytes=VMEM_LIMIT),
        name="window_attn",
    )(slab, slab, slab, slab, slab, bias, sinks, *mix)


def _merge_kernel(x_ref, g_ref, ya_ref, yb_ref, yc_ref, yd_ref,
                  ga_ref, gb_ref, gc_ref, gd_ref, wa_ref, wb_ref, wc_ref, wd_ref, wo_ref,
                  o_ref, u_ref):
    j = pl.program_id(1)

    @pl.when(j == 0)
    def _():
        u_ref[...] = _rms(x_ref[...], g_ref[0:1, :]).astype(BF16)
        o_ref[...] = jnp.zeros_like(o_ref)

    u = u_ref[...]
    merged = None
    for y_ref, gate_ref, w_ref in ((ya_ref, ga_ref, wa_ref), (yb_ref, gb_ref, wb_ref),
                                   (yc_ref, gc_ref, wc_ref), (yd_ref, gd_ref, wd_ref)):
        term = _sigmoid(_dot(u, gate_ref[...])) * _dot(y_ref[...], w_ref[...])
        merged = term if merged is None else merged + term
    o_ref[...] += _dot(merged.astype(BF16), wo_ref[...])

    @pl.when(j == pl.num_programs(1) - 1)
    def _():
        o_ref[...] = x_ref[...] + _rms(o_ref[...], g_ref[1:2, :])


def _merge(h, norm, ys, w_in, gate_col0, w_branch, w_out, layer, tm, tn):
    M, D = h.shape
    gcb = gate_col0 // tn
    dcb = D // tn

    def yspec(w):
        return pl.BlockSpec((tm, w), lambda i, j: (i, 0))

    def gspec(br):
        return pl.BlockSpec((None, D, tn), lambda i, j: (layer, 0, gcb + br * dcb + j))

    def wspec(rows, rb):
        return pl.BlockSpec((None, rows, tn), lambda i, j: (layer, rb, j))

    wa, wb, wc, wd = (y.shape[1] for y in ys)
    return pl.pallas_call(
        _merge_kernel,
        grid=(M // tm, dcb),
        in_specs=[pl.BlockSpec((tm, D), lambda i, j: (i, 0)),
                  pl.BlockSpec((None, 2, D), lambda i, j: (layer, 0, 0)),
                  yspec(wa), yspec(wb), yspec(wc), yspec(wd),
                  gspec(0), gspec(1), gspec(2), gspec(3),
                  wspec(wa, 0), wspec(wb, wa // wb), wspec(wc, (wa + wb + wd) // wc),
                  wspec(wd, (wa + wb) // wd),
                  pl.BlockSpec((None, tn, D), lambda i, j: (layer, j, 0))],
        out_specs=pl.BlockSpec((tm, D), lambda i, j: (i, 0)),
        out_shape=jax.ShapeDtypeStruct((M, D), F32),
        scratch_shapes=[pltpu.VMEM((tm, D), BF16)],
        compiler_params=pltpu.CompilerParams(
            dimension_semantics=("parallel", "arbitrary"), vmem_limit_bytes=VMEM_LIMIT),
        name="gated_merge",
    )(h, norm, *ys, w_in, w_in, w_in, w_in, w_branch, w_branch, w_branch, w_branch, w_out)


def _rel_bucket(dist):
    max_exact = REL_BUCKETS // 2
    d = jnp.maximum(dist, 1).astype(F32)
    large = max_exact + (jnp.log(d / max_exact) / math.log(REL_MAX_DIST / max_exact)
                         * (REL_BUCKETS - max_exact)).astype(jnp.int32)
    large = jnp.minimum(large, REL_BUCKETS - 1)
    return jnp.where(dist < max_exact, dist, large)


def _window_bias(rel_bias, heads, dilation, max_dist, rows_heads=2):
    row = jnp.arange(WIN_BLOCK)[:, None]
    col = jnp.arange(2 * WIN_BLOCK)[None, :]
    dist = row + WIN_BLOCK - col
    valid = (dist >= 0) & (dist <= max_dist)
    assert list(heads) == list(range(heads[0], heads[-1] + 1))
    bucket = _rel_bucket(jnp.maximum(dist, 0) * dilation)
    per_head = rel_bias.astype(F32)[:, heads[0]:heads[-1] + 1]
    hit = bucket[None, None] == jnp.arange(REL_BUCKETS)[:, None, None, None]
    tab = jnp.sum(jnp.where(hit, per_head[:, :, None, None], 0.0), axis=0)
    tab = jnp.where(valid[None], tab, -jnp.inf)
    return tab.reshape(len(heads) // rows_heads, rows_heads * WIN_BLOCK, 2 * WIN_BLOCK)


def _trunk(x, ffn1_norm, ffn1_w13, ffn1_w2, mix_norm, w_in, hgrn_lb_logits, hgrn_out_norm,
           attn_sinks, w_branch, w_out, ffn2_norm, ffn2_w13, ffn2_w2, rel_bias, cfg):
    Bn, T, D = x.shape
    depth = w_in.shape[0]
    M = Bn * T
    a_w = A_HEADS * A_DK
    a_cols = 4 * a_w
    b_cols = 3 * B_HEADS * HEAD_DIM
    c_cols = 3 * C_HEADS * HEAD_DIM
    ac_cols = AC_BLOCKS * LANES
    bd_cols = BD_BLOCKS * LANES
    mix_in = ac_cols + bd_cols

    lb_sm = jax.nn.softmax(hgrn_lb_logits.astype(F32), axis=0)
    lb_all = (jnp.cumsum(lb_sm, axis=0) - lb_sm[0:1]).reshape(depth, A_HEADS, 1, A_DK)

    b0, c0 = a_cols, a_cols + b_cols
    dq0 = c0 + c_cols
    dk0 = dq0 + D_HEADS * HEAD_DIM
    c_w = C_HPG * HEAD_DIM

    perm = jnp.array([h * HEAD_DIM + d for h in D_HEAD_PERM for d in range(HEAD_DIM)])

    def c_part(part, g0, g1):
        base = c0 + part * C_HEADS * HEAD_DIM
        return w_in[:, :, base + g0 * c_w:base + g1 * c_w]

    n_grp = len(C_PATTERNS)
    w_in_b = jnp.concatenate(
        [w_in[:, :, :a_cols]] + [c_part(p, 1, n_grp) for p in range(3)]
        + [w_in[:, :, b0:c0], w_in[:, :, dq0:dk0][:, :, perm], w_in[:, :, dk0:mix_in]]
        + [c_part(p, 0, 1) for p in range(3)] + [w_in[:, :, mix_in:]], axis=-1).astype(BF16)
    d0 = 2 * a_w + c_w
    w_br_b = jnp.concatenate(
        [w_branch[:, :2 * a_w], w_branch[:, d0:][:, perm], w_branch[:, 2 * a_w:d0]], axis=1).astype(BF16)
    w_out_b = w_out.astype(BF16)
    f1_w13, f1_w2 = ffn1_w13.astype(BF16), ffn1_w2.astype(BF16)
    f2_w13, f2_w2 = ffn2_w13.astype(BF16), ffn2_w2.astype(BF16)

    c_bias = [_window_bias(rel_bias, list(range(g * C_HPG, (g + 1) * C_HPG)), dil, win // dil)
              for g, (win, dil) in enumerate(C_PATTERNS)]
    d_bias = _window_bias(rel_bias, [C_HEADS + h for h in range(D_HEADS)], 1, D_WINDOW - 1, D_HEADS)
    no_sink = jnp.full((C_HPG // 2, 2, LANES), -jnp.inf, F32)

    h = x.reshape(M, D)
    for l in range(depth):
        h = _ffn(h, ffn1_norm, f1_w13, f1_w2, l, cfg["ffn_tm"], cfg["ffn_tf"])

        pac, pbd = _proj(h, mix_norm, w_in_b, l, ac_cols, bd_cols, cfg["proj_tm"], cfg["proj_tn"])
        pac = pac.reshape(Bn, T, ac_cols)
        pbd = pbd.reshape(Bn, T, bd_cols)

        ya = _hgrn(pac, lb_all[l], hgrn_out_norm[l].reshape(1, 1, A_DK).astype(F32), cfg["hgrn_tt"])
        yb = _stick_breaking(pbd, cfg["sb_tq"])
        mix = ()
        for g, (win, dil) in enumerate(C_PATTERNS):
            if dil == 1:
                slab, cols = pbd, (_C0Q, _C0K, _C0V)
            else:
                slab, cols = pac, tuple(c + 2 * (g - 1) for c in (_CQ, _CK, _CV))
            if g < len(C_PATTERNS) - 1:
                mix += tuple(_window_attn(slab, c_bias[g], no_sink, dil, *cols, C_HPG // 2, 1, True, F32))
            else:
                (yc,) = _window_attn(slab, c_bias[g], no_sink, dil, *cols, C_HPG // 2, 1, True, BF16, mix)
        sinks = jnp.broadcast_to(attn_sinks[l].astype(F32).reshape(1, D_HEADS, 1), (1, D_HEADS, LANES))
        (yd,) = _window_attn(pbd, d_bias, sinks, 1, _DQ, _DK, _DV, 1, D_HEADS // 2, False, BF16)

        ys = (ya.reshape(M, a_w), yb.reshape(M, a_w), yc.reshape(M, c_w), yd.reshape(M, a_w))
        h = _merge(h, mix_norm, ys, w_in_b, mix_in, w_br_b, w_out_b, l, cfg["merge_tm"], cfg["merge_tn"])

        h = _ffn(h, ffn2_norm, f2_w13, f2_w2, l, cfg["ffn_tm"], cfg["ffn_tf"])
    return h.reshape(Bn, T, D)


_CFG = dict(ffn_tm=512, ffn_tf=512, proj_tm=1024, proj_tn=512, hgrn_tt=512, sb_tq=256,
            merge_tm=512, merge_tn=512)


def kernel(x, ffn1_norm, ffn1_w13, ffn1_w2, mix_norm, w_in, hgrn_lb_logits, hgrn_out_norm, attn_sinks,
           w_branch, w_out, ffn2_norm, ffn2_w13, ffn2_w2, rel_bias):
    return _trunk(x, ffn1_norm, ffn1_w13, ffn1_w2, mix_norm, w_in, hgrn_lb_logits, hgrn_out_norm,
                  attn_sinks, w_branch, w_out, ffn2_norm, ffn2_w13, ffn2_w2, rel_bias, _CFG)
```

```python
import functools
import math

import jax
import jax.numpy as jnp
from jax import lax
from jax.experimental import pallas as pl
from jax.experimental.pallas import tpu as pltpu

F32 = jnp.float32
BF16 = jnp.bfloat16

EPS = 1e-6
HEAD_DIM = 64
LANES = 128
A_HEADS = 4
A_DK = 128
B_HEADS = 8
C_PATTERNS = ((128, 1), (512, 4), (2048, 16))
C_HPG = 4
C_HEADS = len(C_PATTERNS) * C_HPG
D_HEADS = 8
D_KV_HEADS = 2
D_WINDOW = 128
WIN_BLOCK = 128
REL_BUCKETS = 32
REL_MAX_DIST = 2048
A_SUB = 16
VMEM_LIMIT = 60 * 1024 * 1024

_CQ, _CK, _CV = 16, 20, 24
AC_BLOCKS = 28
_BQ, _BK, _BV = 0, 4, 8
_DQ, _DK, _DV = 12, 16, 17
_C0Q, _C0K, _C0V = 18, 20, 22
BD_BLOCKS = 24
WIN_UNROLL = 4
D_HEAD_PERM = (0, 4, 1, 5, 2, 6, 3, 7)
SB_CHAINS = 4
SB_EXIT = 100.0


def _rms(x, g):
    ms = jnp.mean(x * x, axis=-1, keepdims=True)
    return x * lax.rsqrt(ms + EPS) * g


def _sigmoid(x):
    return 1.0 / (1.0 + jnp.exp(-x))


def _dot(a, b):
    return jnp.dot(a, b, preferred_element_type=F32)


def _dot_nt(a, b):
    return lax.dot_general(a, b, (((1,), (1,)), ((), ())), preferred_element_type=F32)


def _dot_tn(a, b):
    return lax.dot_general(a, b, (((0,), (0,)), ((), ())), preferred_element_type=F32)


def _split_bf16(x):
    hi = x.astype(BF16)
    lo = (x - hi.astype(F32)).astype(BF16)
    return hi, lo


def _ffn_kernel(x_ref, g_ref, w1_ref, w3_ref, w2_ref, o_ref, xn_ref):
    j = pl.program_id(1)

    @pl.when(j == 0)
    def _():
        xn_ref[...] = _rms(x_ref[...], g_ref[0:1, :]).astype(BF16)
        o_ref[...] = jnp.zeros_like(o_ref)

    xn = xn_ref[...]
    gate = _dot(xn, w1_ref[...])
    up = _dot(xn, w3_ref[...])
    act = (gate * _sigmoid(gate) * up).astype(BF16)
    o_ref[...] += _dot(act, w2_ref[...])

    @pl.when(j == pl.num_programs(1) - 1)
    def _():
        o_ref[...] = x_ref[...] + 0.5 * _rms(o_ref[...], g_ref[1:2, :])


def _ffn(h, norm, w13, w2, layer, tm, tf):
    M, D = h.shape
    F = w2.shape[1]
    nf = F // tf
    return pl.pallas_call(
        _ffn_kernel,
        grid=(M // tm, nf),
        in_specs=[
            pl.BlockSpec((tm, D), lambda i, j: (i, 0)),
            pl.BlockSpec((None, 2, D), lambda i, j: (layer, 0, 0)),
            pl.BlockSpec((None, D, tf), lambda i, j: (layer, 0, j)),
            pl.BlockSpec((None, D, tf), lambda i, j: (layer, 0, j + nf)),
            pl.BlockSpec((None, tf, D), lambda i, j: (layer, j, 0)),
        ],
        out_specs=pl.BlockSpec((tm, D), lambda i, j: (i, 0)),
        out_shape=jax.ShapeDtypeStruct((M, D), F32),
        scratch_shapes=[pltpu.VMEM((tm, D), BF16)],
        compiler_params=pltpu.CompilerParams(
            dimension_semantics=("parallel", "arbitrary"), vmem_limit_bytes=VMEM_LIMIT),
        name="ffn",
    )(h, norm, w13, w13, w2)


def _proj_kernel(x_ref, g_ref, w_ref, o32_ref, o16_ref, xn_ref, *, n32):
    j = pl.program_id(1)

    @pl.when(j == 0)
    def _():
        xn_ref[...] = _rms(x_ref[...], g_ref[0:1, :]).astype(BF16)

    @pl.when(j < n32)
    def _():
        o32_ref[...] = _dot(xn_ref[...], w_ref[...])

    @pl.when(j >= n32)
    def _():
        o16_ref[...] = _dot(xn_ref[...], w_ref[...]).astype(BF16)


def _proj(h, norm, w_in, layer, cols32, cols16, tm, tn):
    M, D = h.shape
    n32, n16 = cols32 // tn, cols16 // tn
    return pl.pallas_call(
        functools.partial(_proj_kernel, n32=n32),
        grid=(M // tm, n32 + n16),
        in_specs=[
            pl.BlockSpec((tm, D), lambda i, j: (i, 0)),
            pl.BlockSpec((None, 2, D), lambda i, j: (layer, 0, 0)),
            pl.BlockSpec((None, D, tn), lambda i, j: (layer, 0, j)),
        ],
        out_specs=[pl.BlockSpec((tm, tn), lambda i, j: (i, jnp.minimum(j, n32 - 1))),
                   pl.BlockSpec((tm, tn), lambda i, j: (i, jnp.maximum(j - n32, 0)))],
        out_shape=[jax.ShapeDtypeStruct((M, cols32), F32), jax.ShapeDtypeStruct((M, cols16), BF16)],
        scratch_shapes=[pltpu.VMEM((tm, D), BF16)],
        compiler_params=pltpu.CompilerParams(
            dimension_semantics=("parallel", "arbitrary"), vmem_limit_bytes=VMEM_LIMIT),
        name="proj",
    )(h, norm, w_in)


def _hgrn_kernel(q_ref, f_ref, i_ref, g_ref, lb_ref, ng_ref, o_ref, st_ref, kv_ref, sall_ref):
    tt = q_ref.shape[1]
    sub = A_SUB
    nb = tt // sub
    per = LANES // sub
    shape3 = (nb, sub, LANES)

    @pl.when(pl.program_id(2) == 0)
    def _():
        st_ref[...] = jnp.zeros_like(st_ref)

    lb = lb_ref[0]
    f = lb + (1.0 - lb) * _sigmoid(f_ref[0])
    k = 1.0 - f
    gl = jnp.log2(f)
    r = lax.broadcasted_iota(jnp.int32, (LANES, LANES), 0)
    c = lax.broadcasted_iota(jnp.int32, (LANES, LANES), 1)
    tril = ((r // sub == c // sub) & (c <= r)).astype(BF16)
    cums = []
    for a in range(tt // LANES):
        hi, lo = _split_bf16(gl[a * LANES:(a + 1) * LANES])
        cums.append(_dot(tril, hi) + _dot(tril, lo))
    b = jnp.concatenate(cums, axis=0)
    q = q_ref[0]
    v = i_ref[0]
    b3, q3, v3 = (x.reshape(shape3) for x in (b, q, v))
    bk3 = b3 - jnp.log2(k).reshape(shape3)

    ones = jnp.ones((LANES, LANES), BF16)
    rows = lax.broadcasted_iota(jnp.int32, (1, sub, LANES), 1)
    o3 = jnp.zeros(shape3, F32)
    for s in range(sub):
        w = jnp.where(rows >= s, q3 * jnp.exp2(b3 - bk3[:, s:s + 1, :]), 0.0).astype(BF16)
        o3 = o3 + _dot(w.reshape(tt, LANES), ones).reshape(shape3) * v3[:, s:s + 1, :]

    b_last = b3[:, sub - 1:sub, :]
    kd = jnp.exp2(b_last - bk3).reshape(tt, LANES)
    dec = jnp.exp2(b_last)
    for a in range(tt // LANES):
        v_t = v[a * LANES:(a + 1) * LANES].T.astype(BF16)
        kd_a = kd[a * LANES:(a + 1) * LANES]
        rhs = jnp.concatenate([jnp.where(r // sub == j, kd_a, 0.0).astype(BF16) for j in range(per)], axis=1)
        kv = _dot(v_t, rhs)
        for j in range(per):
            kv_ref[a * per + j] = kv[:, j * LANES:(j + 1) * LANES]

    st = st_ref[...]
    for n in range(nb):
        sall_ref[n] = st.astype(BF16)
        st = dec[n] * st + kv_ref[n]
    st_ref[...] = st

    qe = (q * jnp.exp2(b)).astype(BF16)
    inter = []
    for a in range(tt // LANES):
        stack = sall_ref[a * per:(a + 1) * per].reshape(per * LANES, LANES)
        p = _dot_nt(qe[a * LANES:(a + 1) * LANES], stack)
        inter += [p[j * sub:(j + 1) * sub, j * LANES:(j + 1) * LANES] for j in range(per)]
    o = o3.reshape(tt, LANES) + jnp.concatenate(inter, axis=0)

    g = g_ref[0]
    o_ref[0] = (_rms(o, ng_ref[0]) * (g * _sigmoid(g))).astype(o_ref.dtype)


def _hgrn(pa, lb, ng, tt):
    Bn, T, _ = pa.shape
    H = A_HEADS

    def col(off):
        return pl.BlockSpec((1, tt, A_DK), lambda b, h, t: (b, t, off * H + h))

    return pl.pallas_call(
        _hgrn_kernel,
        grid=(Bn, H, T // tt),
        in_specs=[col(0), col(1), col(2), col(3),
                  pl.BlockSpec((1, 1, A_DK), lambda b, h, t: (h, 0, 0)),
                  pl.BlockSpec((1, 1, A_DK), lambda b, h, t: (0, 0, 0))],
        out_specs=pl.BlockSpec((1, tt, A_DK), lambda b, h, t: (b, t, h)),
        out_shape=jax.ShapeDtypeStruct((Bn, T, H * A_DK), BF16),
        scratch_shapes=[pltpu.VMEM((A_DK, A_DK), F32), pltpu.VMEM((tt // A_SUB, A_DK, A_DK), F32),
                        pltpu.VMEM((tt // A_SUB, A_DK, A_DK), BF16)],
        compiler_params=pltpu.CompilerParams(
            dimension_semantics=("parallel", "parallel", "arbitrary"), vmem_limit_bytes=VMEM_LIMIT),
        name="hgrn2",
    )(pa, pa, pa, pa, lb, ng)


def _sb_kernel(q_ref, k_ref, v_ref, o_ref, suf_ref, acc_ref, run_ref):
    tq = q_ref.shape[1]
    i = pl.program_id(2)
    in_h0 = lax.broadcasted_iota(jnp.int32, (tq, LANES), 1) < HEAD_DIM
    r = lax.broadcasted_iota(jnp.int32, (tq, tq), 0)
    c = lax.broadcasted_iota(jnp.int32, (tq, tq), 1)
    suf_ref[...] = (r >= c).astype(BF16)
    strict = jnp.concatenate([c < r, c < r], axis=0)

    def lanes(n):
        return slice(n * LANES, (n + 1) * LANES)

    q2s = []
    for n in range(SB_CHAINS):
        q = q_ref[0, :, lanes(n)]
        zero = jnp.zeros_like(q)
        q2s.append(jnp.concatenate([jnp.where(in_h0, q, zero), jnp.where(in_h0, zero, q)], axis=0)
                   * (HEAD_DIM ** -0.5))

    def tile(q2, kb, vb, run, diag):
        z = _dot_nt(q2, kb)
        sp = jnp.maximum(z, 0.0) + jnp.log(1.0 + jnp.exp(-jnp.abs(z)))
        if diag:
            sp = jnp.where(strict, sp, 0.0)
        hi, lo = _split_bf16(sp)
        suffix = suf_ref[...]
        cs = _dot(hi, suffix) + _dot(lo, suffix) + run
        a = jnp.exp(z - cs)
        if diag:
            a = jnp.where(strict, a, 0.0)
        return _dot(a.astype(BF16), vb), cs[:, 0:1]

    def step(rows, first):
        lowest = None
        for n in range(SB_CHAINS):
            run = jnp.zeros((2 * tq, 1), F32) if first else run_ref[n]
            pv, run = tile(q2s[n], k_ref[0, rows, lanes(n)], v_ref[0, rows, lanes(n)], run, first)
            acc_ref[n] = pv if first else acc_ref[n] + pv
            run_ref[n] = run
            low = jnp.min(run)
            lowest = low if lowest is None else jnp.minimum(lowest, low)
        return lowest

    lowest = step(pl.ds(pl.multiple_of(i * tq, tq), tq), True)

    def cond(carry):
        s, lowest = carry
        return (s < i) & (lowest <= SB_EXIT)

    def body(carry):
        s, _ = carry
        return s + 1, step(pl.ds(pl.multiple_of((i - 1 - s) * tq, tq), tq), False)

    lax.while_loop(cond, body, (jnp.int32(0), lowest))
    for n in range(SB_CHAINS):
        acc = acc_ref[n]
        o_ref[0, :, lanes(n)] = jnp.where(in_h0, acc[:tq], acc[tq:]).astype(o_ref.dtype)


def _stick_breaking(pbd, tq):
    Bn, T, _ = pbd.shape
    ngrp = B_HEADS // 2 // SB_CHAINS
    w = SB_CHAINS * LANES
    return pl.pallas_call(
        _sb_kernel,
        grid=(Bn, ngrp, T // tq),
        in_specs=[pl.BlockSpec((1, tq, w), lambda b, p, i: (b, i, _BQ // SB_CHAINS + p)),
                  pl.BlockSpec((1, T, w), lambda b, p, i: (b, 0, _BK // SB_CHAINS + p)),
                  pl.BlockSpec((1, T, w), lambda b, p, i: (b, 0, _BV // SB_CHAINS + p))],
        out_specs=pl.BlockSpec((1, tq, w), lambda b, p, i: (b, i, p)),
        out_shape=jax.ShapeDtypeStruct((Bn, T, B_HEADS // 2 * LANES), BF16),
        scratch_shapes=[pltpu.VMEM((tq, tq), BF16), pltpu.VMEM((SB_CHAINS, 2 * tq, LANES), F32),
                        pltpu.VMEM((SB_CHAINS, 2 * tq, 1), F32)],
        compiler_params=pltpu.CompilerParams(
            dimension_semantics=("parallel", "parallel", "arbitrary"), vmem_limit_bytes=VMEM_LIMIT),
        name="stick_breaking",
    )(pbd, pbd, pbd)


def _win_kernel(q_ref, k_ref, kp_ref, v_ref, vp_ref, bias_ref, sink_ref, *rest, dil, nsub, nq, n_mix):
    mix_refs, out_refs = rest[:2 * n_mix], rest[2 * n_mix + (1 if n_mix else 0):]
    blk = WIN_BLOCK
    span = blk * dil
    nrow = 2 * nq * blk
    i = pl.program_id(2)
    lane = lax.broadcasted_iota(jnp.int32, (blk, LANES), 1)
    in_h0 = lane < HEAD_DIM
    prev_col = lax.broadcasted_iota(jnp.int32, (nrow, 2 * blk), 1) < blk
    ones = jnp.ones((2 * blk, LANES), BF16)
    sinks = jnp.concatenate([jnp.broadcast_to(sink_ref[hh:hh + 1, 0:1], (blk, 1)) for hh in range(2 * nq)],
                            axis=0)
    bias = bias_ref[...]

    def rows_of(start):
        return pl.ds(start, blk, stride=dil) if dil > 1 else pl.ds(start, blk)

    def residue_class(s, c):
        cur = rows_of(s * span + c)
        q = q_ref[0, cur, :].astype(BF16)
        if s == 0:
            kp, vp = kp_ref[0, rows_of(c), :], vp_ref[0, rows_of(c), :]
        else:
            prev = rows_of((s - 1) * span + c)
            kp, vp = k_ref[0, prev, :], v_ref[0, prev, :]
        k2 = jnp.concatenate([kp.astype(BF16), k_ref[0, cur, :].astype(BF16)], axis=0)
        v2 = jnp.concatenate([vp.astype(BF16), v_ref[0, cur, :].astype(BF16)], axis=0)
        qs = [q[:, p * LANES:(p + 1) * LANES] for p in range(nq)]
        zero = jnp.zeros_like(qs[0])
        q2 = jnp.concatenate([jnp.where(in_h0, x, zero) for x in qs]
                             + [jnp.where(in_h0, zero, x) for x in qs], axis=0)
        logits = _dot_nt(q2 * (HEAD_DIM ** -0.5), k2) + bias
        if s == 0:
            logits = jnp.where(jnp.logical_and(i == 0, prev_col), -jnp.inf, logits)
        m = jnp.max(jnp.maximum(logits[:, :blk], logits[:, blk:]), axis=-1, keepdims=True)
        m = jnp.maximum(m, sinks)
        p = jnp.exp(logits - m).astype(BF16)
        if nq > 1:
            pv = _dot(p, jnp.concatenate([v2, ones], axis=1))
            num, den = pv[:, :LANES], pv[:, LANES:]
        else:
            num, den = _dot(p, v2), _dot(p, ones)
        den = den + jnp.exp(sinks - m)
        o2 = num / den

        def lane_blocks(x):
            return jnp.concatenate([jnp.where(in_h0, x[b * blk:(b + 1) * blk], x[(nq + b) * blk:(nq + b + 1) * blk])
                                    for b in range(nq)], axis=1)

        out_refs[0][0, cur, :] = lane_blocks(o2).astype(out_refs[0].dtype)
        if len(out_refs) > 1:
            out_refs[1][0, cur, :] = lane_blocks(m + jnp.log(den))

    for s in range(nsub):
        if dil <= WIN_UNROLL:
            for c in range(dil):
                residue_class(s, c)
        else:
            def body(c, carry, s=s):
                residue_class(s, c)
                return carry
            lax.fori_loop(0, dil, body, 0, unroll=WIN_UNROLL)

    if n_mix:
        outs = [r[0] for r in mix_refs[0::2]] + [out_refs[0][0]]
        lses = [r[0] for r in mix_refs[1::2]] + [out_refs[1][0]]
        top = functools.reduce(jnp.maximum, lses)
        ws = [jnp.exp(l - top) for l in lses]
        y_ref = rest[2 * n_mix]
        y_ref[0] = (sum(w * o for w, o in zip(ws, outs)) / sum(ws)).astype(y_ref.dtype)


def _window_attn(slab, bias, sinks, dil, qcol, kcol, vcol, ngrp, nq, with_lse, out_dtype, mix=()):
    Bn, T, _ = slab.shape
    span = WIN_BLOCK * dil
    nsub = max(1, WIN_UNROLL // dil)
    tb = span * nsub
    assert qcol % nq == 0

    def kvmap(col, prev):
        def f(b, g, i):
            return (b, jnp.maximum(i * nsub - 1, 0) if prev else i, col + g)
        return f

    qblk = (1, tb, nq * LANES)
    blk = (1, tb, LANES)
    pblk = (1, span, LANES)
    out_spec = pl.BlockSpec(qblk, lambda b, g, i: (b, i, g))
    out_sds = jax.ShapeDtypeStruct((Bn, T, ngrp * nq * LANES), out_dtype)
    lse_sds = jax.ShapeDtypeStruct((Bn, T, ngrp * nq * LANES), F32)
    if mix:
        out_specs, out_shape = [out_spec], [out_sds]
        scratch = [pltpu.VMEM(qblk, F32), pltpu.VMEM(qblk, F32)]
    else:
        out_specs = [out_spec, out_spec] if with_lse else [out_spec]
        out_shape = [out_sds, lse_sds] if with_lse else [out_sds]
        scratch = []
    return pl.pallas_call(
        functools.partial(_win_kernel, dil=dil, nsub=nsub, nq=nq, n_mix=len(mix) // 2),
        grid=(Bn, ngrp, T // tb),
        in_specs=[pl.BlockSpec(qblk, lambda b, g, i: (b, i, qcol // nq + g)),
                  pl.BlockSpec(blk, kvmap(kcol, False)), pl.BlockSpec(pblk, kvmap(kcol, True)),
                  pl.BlockSpec(blk, kvmap(vcol, False)), pl.BlockSpec(pblk, kvmap(vcol, True)),
                  pl.BlockSpec((None, 2 * nq * WIN_BLOCK, 2 * WIN_BLOCK), lambda b, g, i: (g, 0, 0)),
                  pl.BlockSpec((None, 2 * nq, LANES), lambda b, g, i: (g, 0, 0))] + [out_spec] * len(mix),
        out_specs=out_specs,
        out_shape=out_shape,
        scratch_shapes=scratch,
        compiler_params=pltpu.CompilerParams(
            dimension_semantics=("parallel", "parallel", "arbitrary"), vmem_limit_bytes=VMEM_LIMIT),
        name="window_attn",
    )(slab, slab, slab, slab, slab, bias, sinks, *mix)


def _merge_kernel(x_ref, g_ref, ya_ref, yb_ref, yc_ref, yd_ref,
                  ga_ref, gb_ref, gc_ref, gd_ref, wa_ref, wb_ref, wc_ref, wd_ref, wo_ref,
                  o_ref, u_ref):
    j = pl.program_id(1)

    @pl.when(j == 0)
    def _():
        u_ref[...] = _rms(x_ref[...], g_ref[0:1, :]).astype(BF16)
        o_ref[...] = jnp.zeros_like(o_ref)

    u = u_ref[...]
    merged = None
    for y_ref, gate_ref, w_ref in ((ya_ref, ga_ref, wa_ref), (yb_ref, gb_ref, wb_ref),
                                   (yc_ref, gc_ref, wc_ref), (yd_ref, gd_ref, wd_ref)):
        term = _sigmoid(_dot(u, gate_ref[...])) * _dot(y_ref[...], w_ref[...])
        merged = term if merged is None else merged + term
    o_ref[...] += _dot(merged.astype(BF16), wo_ref[...])

    @pl.when(j == pl.num_programs(1) - 1)
    def _():
        o_ref[...] = x_ref[...] + _rms(o_ref[...], g_ref[1:2, :])


def _merge(h, norm, ys, w_in, gate_col0, w_branch, w_out, layer, tm, tn):
    M, D = h.shape
    gcb = gate_col0 // tn
    dcb = D // tn

    def yspec(w):
        return pl.BlockSpec((tm, w), lambda i, j: (i, 0))

    def gspec(br):
        return pl.BlockSpec((None, D, tn), lambda i, j: (layer, 0, gcb + br * dcb + j))

    def wspec(rows, rb):
        return pl.BlockSpec((None, rows, tn), lambda i, j: (layer, rb, j))

    wa, wb, wc, wd = (y.shape[1] for y in ys)
    return pl.pallas_call(
        _merge_kernel,
        grid=(M // tm, dcb),
        in_specs=[pl.BlockSpec((tm, D), lambda i, j: (i, 0)),
                  pl.BlockSpec((None, 2, D), lambda i, j: (layer, 0, 0)),
                  yspec(wa), yspec(wb), yspec(wc), yspec(wd),
                  gspec(0), gspec(1), gspec(2), gspec(3),
                  wspec(wa, 0), wspec(wb, wa // wb), wspec(wc, (wa + wb + wd) // wc),
                  wspec(wd, (wa + wb) // wd),
                  pl.BlockSpec((None, tn, D), lambda i, j: (layer, j, 0))],
        out_specs=pl.BlockSpec((tm, D), lambda i, j: (i, 0)),
        out_shape=jax.ShapeDtypeStruct((M, D), F32),
        scratch_shapes=[pltpu.VMEM((tm, D), BF16)],
        compiler_params=pltpu.CompilerParams(
            dimension_semantics=("parallel", "arbitrary"), vmem_limit_bytes=VMEM_LIMIT),
        name="gated_merge",
    )(h, norm, *ys, w_in, w_in, w_in, w_in, w_branch, w_branch, w_branch, w_branch, w_out)


def _rel_bucket(dist):
    max_exact = REL_BUCKETS // 2
    d = jnp.maximum(dist, 1).astype(F32)
    large = max_exact + (jnp.log(d / max_exact) / math.log(REL_MAX_DIST / max_exact)
                         * (REL_BUCKETS - max_exact)).astype(jnp.int32)
    large = jnp.minimum(large, REL_BUCKETS - 1)
    return jnp.where(dist < max_exact, dist, large)


def _window_bias(rel_bias, heads, dilation, max_dist, rows_heads=2):
    row = jnp.arange(WIN_BLOCK)[:, None]
    col = jnp.arange(2 * WIN_BLOCK)[None, :]
    dist = row + WIN_BLOCK - col
    valid = (dist >= 0) & (dist <= max_dist)
    assert list(heads) == list(range(heads[0], heads[-1] + 1))
    bucket = _rel_bucket(jnp.maximum(dist, 0) * dilation)
    per_head = rel_bias.astype(F32)[:, heads[0]:heads[-1] + 1]
    hit = bucket[None, None] == jnp.arange(REL_BUCKETS)[:, None, None, None]
    tab = jnp.sum(jnp.where(hit, per_head[:, :, None, None], 0.0), axis=0)
    tab = jnp.where(valid[None], tab, -jnp.inf)
    return tab.reshape(len(heads) // rows_heads, rows_heads * WIN_BLOCK, 2 * WIN_BLOCK)


def _trunk(x, ffn1_norm, ffn1_w13, ffn1_w2, mix_norm, w_in, hgrn_lb_logits, hgrn_out_norm,
           attn_sinks, w_branch, w_out, ffn2_norm, ffn2_w13, ffn2_w2, rel_bias, cfg):
    Bn, T, D = x.shape
    depth = w_in.shape[0]
    M = Bn * T
    a_w = A_HEADS * A_DK
    a_cols = 4 * a_w
    b_cols = 3 * B_HEADS * HEAD_DIM
    c_cols = 3 * C_HEADS * HEAD_DIM
    ac_cols = AC_BLOCKS * LANES
    bd_cols = BD_BLOCKS * LANES
    mix_in = ac_cols + bd_cols

    lb_sm = jax.nn.softmax(hgrn_lb_logits.astype(F32), axis=0)
    lb_all = (jnp.cumsum(lb_sm, axis=0) - lb_sm[0:1]).reshape(depth, A_HEADS, 1, A_DK)

    b0, c0 = a_cols, a_cols + b_cols
    dq0 = c0 + c_cols
    dk0 = dq0 + D_HEADS * HEAD_DIM
    c_w = C_HPG * HEAD_DIM

    perm = jnp.array([h * HEAD_DIM + d for h in D_HEAD_PERM for d in range(HEAD_DIM)])

    def c_part(part, g0, g1):
        base = c0 + part * C_HEADS * HEAD_DIM
        return w_in[:, :, base + g0 * c_w:base + g1 * c_w]

    n_grp = len(C_PATTERNS)
    w_in_b = jnp.concatenate(
        [w_in[:, :, :a_cols]] + [c_part(p, 1, n_grp) for p in range(3)]
        + [w_in[:, :, b0:c0], w_in[:, :, dq0:dk0][:, :, perm], w_in[:, :, dk0:mix_in]]
        + [c_part(p, 0, 1) for p in range(3)] + [w_in[:, :, mix_in:]], axis=-1).astype(BF16)
    d0 = 2 * a_w + c_w
    w_br_b = jnp.concatenate(
        [w_branch[:, :2 * a_w], w_branch[:, d0:][:, perm], w_branch[:, 2 * a_w:d0]], axis=1).astype(BF16)
    w_out_b = w_out.astype(BF16)
    f1_w13, f1_w2 = ffn1_w13.astype(BF16), ffn1_w2.astype(BF16)
    f2_w13, f2_w2 = ffn2_w13.astype(BF16), ffn2_w2.astype(BF16)

    c_bias = [_window_bias(rel_bias, list(range(g * C_HPG, (g + 1) * C_HPG)), dil, win // dil)
              for g, (win, dil) in enumerate(C_PATTERNS)]
    d_bias = _window_bias(rel_bias, [C_HEADS + h for h in range(D_HEADS)], 1, D_WINDOW - 1, D_HEADS)
    no_sink = jnp.full((C_HPG // 2, 2, LANES), -jnp.inf, F32)

    h = x.reshape(M, D)
    for l in range(depth):
        h = _ffn(h, ffn1_norm, f1_w13, f1_w2, l, cfg["ffn_tm"], cfg["ffn_tf"])

        pac, pbd = _proj(h, mix_norm, w_in_b, l, ac_cols, bd_cols, cfg["proj_tm"], cfg["proj_tn"])
        pac = pac.reshape(Bn, T, ac_cols)
        pbd = pbd.reshape(Bn, T, bd_cols)

        ya = _hgrn(pac, lb_all[l], hgrn_out_norm[l].reshape(1, 1, A_DK).astype(F32), cfg["hgrn_tt"])
        yb = _stick_breaking(pbd, cfg["sb_tq"])
        mix = ()
        for g, (win, dil) in enumerate(C_PATTERNS):
            if dil == 1:
                slab, cols = pbd, (_C0Q, _C0K, _C0V)
            else:
                slab, cols = pac, tuple(c + 2 * (g - 1) for c in (_CQ, _CK, _CV))
            if g < len(C_PATTERNS) - 1:
                mix += tuple(_window_attn(slab, c_bias[g], no_sink, dil, *cols, C_HPG // 2, 1, True, F32))
            else:
                (yc,) = _window_attn(slab, c_bias[g], no_sink, dil, *cols, C_HPG // 2, 1, True, BF16, mix)
        sinks = jnp.broadcast_to(attn_sinks[l].astype(F32).reshape(1, D_HEADS, 1), (1, D_HEADS, LANES))
        (yd,) = _window_attn(pbd, d_bias, sinks, 1, _DQ, _DK, _DV, 1, D_HEADS // 2, False, BF16)

        ys = (ya.reshape(M, a_w), yb.reshape(M, a_w), yc.reshape(M, c_w), yd.reshape(M, a_w))
        h = _merge(h, mix_norm, ys, w_in_b, mix_in, w_br_b, w_out_b, l, cfg["merge_tm"], cfg["merge_tn"])

        h = _ffn(h, ffn2_norm, f2_w13, f2_w2, l, cfg["ffn_tm"], cfg["ffn_tf"])
    return h.reshape(Bn, T, D)


_CFG = dict(ffn_tm=512, ffn_tf=512, proj_tm=1024, proj_tn=512, hgrn_tt=512, sb_tq=256,
            merge_tm=512, merge_tn=512)


def kernel(x, ffn1_norm, ffn1_w13, ffn1_w2, mix_norm, w_in, hgrn_lb_logits, hgrn_out_norm, attn_sinks,
           w_branch, w_out, ffn2_norm, ffn2_w13, ffn2_w2, rel_bias):
    return _trunk(x, ffn1_norm, ffn1_w13, ffn1_w2, mix_norm, w_in, hgrn_lb_logits, hgrn_out_norm,
                  attn_sinks, w_branch, w_out, ffn2_norm, ffn2_w13, ffn2_w2, rel_bias, _CFG)
```

```python
import functools
import math

import jax
import jax.numpy as jnp
from jax import lax
from jax.experimental import pallas as pl
from jax.experimental.pallas import tpu as pltpu

F32 = jnp.float32
BF16 = jnp.bfloat16

EPS = 1e-6
HEAD_DIM = 64
LANES = 128
A_HEADS = 4
A_DK = 128
B_HEADS = 8
C_PATTERNS = ((128, 1), (512, 4), (2048, 16))
C_HPG = 4
C_HEADS = len(C_PATTERNS) * C_HPG
D_HEADS = 8
D_KV_HEADS = 2
D_WINDOW = 128
WIN_BLOCK = 128
REL_BUCKETS = 32
REL_MAX_DIST = 2048
A_SUB = 16
VMEM_LIMIT = 60 * 1024 * 1024

_CQ, _CK, _CV = 16, 20, 24
AC_BLOCKS = 28
_BQ, _BK, _BV = 0, 4, 8
_DQ, _DK, _DV = 12, 16, 17
_C0Q, _C0K, _C0V = 18, 20, 22
BD_BLOCKS = 24
WIN_UNROLL = 4
D_HEAD_PERM = (0, 4, 1, 5, 2, 6, 3, 7)
SB_CHAINS = 4
SB_EXIT = 100.0


def _rms(x, g):
    ms = jnp.mean(x * x, axis=-1, keepdims=True)
    return x * lax.rsqrt(ms + EPS) * g


def _sigmoid(x):
    return 1.0 / (1.0 + jnp.exp(-x))


def _dot(a, b):
    return jnp.dot(a, b, preferred_element_type=F32)


def _dot_nt(a, b):
    return lax.dot_general(a, b, (((1,), (1,)), ((), ())), preferred_element_type=F32)


def _dot_tn(a, b):
    return lax.dot_general(a, b, (((0,), (0,)), ((), ())), preferred_element_type=F32)


def _split_bf16(x):
    hi = x.astype(BF16)
    lo = (x - hi.astype(F32)).astype(BF16)
    return hi, lo


def _ffn_kernel(x_ref, g_ref, w1a_ref, w3a_ref, w2a_ref, w1b_ref, w3b_ref, w2b_ref, o_ref, xn_ref, *, odd):
    j = pl.program_id(1)
    last = pl.num_programs(1) - 1

    @pl.when(j == 0)
    def _():
        xn_ref[...] = _rms(x_ref[...], g_ref[0:1, :]).astype(BF16)
        o_ref[...] = jnp.zeros_like(o_ref)

    def part(w1_ref, w3_ref, w2_ref):
        xn = xn_ref[...]
        gate = _dot(xn, w1_ref[...])
        up = _dot(xn, w3_ref[...])
        return _dot((gate * _sigmoid(gate) * up).astype(BF16), w2_ref[...])

    def both():
        o_ref[...] += part(w1a_ref, w3a_ref, w2a_ref) + part(w1b_ref, w3b_ref, w2b_ref)

    if odd:
        pl.when(j < last)(both)

        @pl.when(j == last)
        def _():
            o_ref[...] += part(w1a_ref, w3a_ref, w2a_ref)
    else:
        both()

    @pl.when(j == last)
    def _():
        o_ref[...] = x_ref[...] + 0.5 * _rms(o_ref[...], g_ref[1:2, :])


def _ffn(h, norm, w13, w2, layer, tm, tf):
    M, D = h.shape
    F = w2.shape[1]
    nf = F // tf

    def tile(j, b):
        return jnp.minimum(2 * j + b, nf - 1)

    def w_specs(b):
        return [pl.BlockSpec((None, D, tf), lambda i, j: (layer, 0, tile(j, b))),
                pl.BlockSpec((None, D, tf), lambda i, j: (layer, 0, tile(j, b) + nf)),
                pl.BlockSpec((None, tf, D), lambda i, j: (layer, tile(j, b), 0))]

    return pl.pallas_call(
        functools.partial(_ffn_kernel, odd=nf % 2 == 1),
        grid=(M // tm, (nf + 1) // 2),
        in_specs=[
            pl.BlockSpec((tm, D), lambda i, j: (i, 0)),
            pl.BlockSpec((None, 2, D), lambda i, j: (layer, 0, 0)),
        ] + w_specs(0) + w_specs(1),
        out_specs=pl.BlockSpec((tm, D), lambda i, j: (i, 0)),
        out_shape=jax.ShapeDtypeStruct((M, D), F32),
        scratch_shapes=[pltpu.VMEM((tm, D), BF16)],
        compiler_params=pltpu.CompilerParams(
            dimension_semantics=("parallel", "arbitrary"), vmem_limit_bytes=VMEM_LIMIT),
        name="ffn",
    )(h, norm, w13, w13, w2, w13, w13, w2)


def _proj_kernel(x_ref, g_ref, w_ref, o32_ref, o16_ref, xn_ref, *, n32):
    j = pl.program_id(1)

    @pl.when(j == 0)
    def _():
        xn_ref[...] = _rms(x_ref[...], g_ref[0:1, :]).astype(BF16)

    @pl.when(j < n32)
    def _():
        o32_ref[...] = _dot(xn_ref[...], w_ref[...])

    @pl.when(j >= n32)
    def _():
        o16_ref[...] = _dot(xn_ref[...], w_ref[...]).astype(BF16)


def _proj(h, norm, w_in, layer, cols32, cols16, tm, tn):
    M, D = h.shape
    n32, n16 = cols32 // tn, cols16 // tn
    return pl.pallas_call(
        functools.partial(_proj_kernel, n32=n32),
        grid=(M // tm, n32 + n16),
        in_specs=[
            pl.BlockSpec((tm, D), lambda i, j: (i, 0)),
            pl.BlockSpec((None, 2, D), lambda i, j: (layer, 0, 0)),
            pl.BlockSpec((None, D, tn), lambda i, j: (layer, 0, j)),
        ],
        out_specs=[pl.BlockSpec((tm, tn), lambda i, j: (i, jnp.minimum(j, n32 - 1))),
                   pl.BlockSpec((tm, tn), lambda i, j: (i, jnp.maximum(j - n32, 0)))],
        out_shape=[jax.ShapeDtypeStruct((M, cols32), F32), jax.ShapeDtypeStruct((M, cols16), BF16)],
        scratch_shapes=[pltpu.VMEM((tm, D), BF16)],
        compiler_params=pltpu.CompilerParams(
            dimension_semantics=("parallel", "arbitrary"), vmem_limit_bytes=VMEM_LIMIT),
        name="proj",
    )(h, norm, w_in)


def _hgrn_kernel(q_ref, f_ref, i_ref, g_ref, lb_ref, ng_ref, o_ref, st_ref, kv_ref, sall_ref):
    tt = q_ref.shape[1]
    sub = A_SUB
    nb = tt // sub
    per = LANES // sub
    shape3 = (nb, sub, LANES)

    @pl.when(pl.program_id(2) == 0)
    def _():
        st_ref[...] = jnp.zeros_like(st_ref)

    lb = lb_ref[0]
    f = lb + (1.0 - lb) * _sigmoid(f_ref[0])
    k = 1.0 - f
    gl = jnp.log2(f)
    r = lax.broadcasted_iota(jnp.int32, (LANES, LANES), 0)
    c = lax.broadcasted_iota(jnp.int32, (LANES, LANES), 1)
    tril = ((r // sub == c // sub) & (c <= r)).astype(BF16)
    cums = []
    for a in range(tt // LANES):
        hi, lo = _split_bf16(gl[a * LANES:(a + 1) * LANES])
        cums.append(_dot(tril, hi) + _dot(tril, lo))
    b = jnp.concatenate(cums, axis=0)
    q = q_ref[0]
    v = i_ref[0]
    b3, q3, v3 = (x.reshape(shape3) for x in (b, q, v))
    bk3 = b3 - jnp.log2(k).reshape(shape3)

    ones = jnp.ones((LANES, LANES), BF16)
    rows = lax.broadcasted_iota(jnp.int32, (1, sub, LANES), 1)
    o3 = jnp.zeros(shape3, F32)
    for s in range(sub):
        w = jnp.where(rows >= s, q3 * jnp.exp2(b3 - bk3[:, s:s + 1, :]), 0.0).astype(BF16)
        o3 = o3 + _dot(w.reshape(tt, LANES), ones).reshape(shape3) * v3[:, s:s + 1, :]

    b_last = b3[:, sub - 1:sub, :]
    kd = jnp.exp2(b_last - bk3).reshape(tt, LANES)
    dec = jnp.exp2(b_last)
    for a in range(tt // LANES):
        v_t = v[a * LANES:(a + 1) * LANES].T.astype(BF16)
        kd_a = kd[a * LANES:(a + 1) * LANES]
        rhs = jnp.concatenate([jnp.where(r // sub == j, kd_a, 0.0).astype(BF16) for j in range(per)], axis=1)
        kv = _dot(v_t, rhs)
        for j in range(per):
            kv_ref[a * per + j] = kv[:, j * LANES:(j + 1) * LANES]

    st = st_ref[...]
    for n in range(nb):
        sall_ref[n] = st.astype(BF16)
        st = dec[n] * st + kv_ref[n]
    st_ref[...] = st

    qe = (q * jnp.exp2(b)).astype(BF16)
    inter = []
    for a in range(tt // LANES):
        stack = sall_ref[a * per:(a + 1) * per].reshape(per * LANES, LANES)
        p = _dot_nt(qe[a * LANES:(a + 1) * LANES], stack)
        inter += [p[j * sub:(j + 1) * sub, j * LANES:(j + 1) * LANES] for j in range(per)]
    o = o3.reshape(tt, LANES) + jnp.concatenate(inter, axis=0)

    g = g_ref[0]
    o_ref[0] = (_rms(o, ng_ref[0]) * (g * _sigmoid(g))).astype(o_ref.dtype)


def _hgrn(pa, lb, ng, tt):
    Bn, T, _ = pa.shape
    H = A_HEADS

    def col(off):
        return pl.BlockSpec((1, tt, A_DK), lambda b, h, t: (b, t, off * H + h))

    return pl.pallas_call(
        _hgrn_kernel,
        grid=(Bn, H, T // tt),
        in_specs=[col(0), col(1), col(2), col(3),
                  pl.BlockSpec((1, 1, A_DK), lambda b, h, t: (h, 0, 0)),
                  pl.BlockSpec((1, 1, A_DK), lambda b, h, t: (0, 0, 0))],
        out_specs=pl.BlockSpec((1, tt, A_DK), lambda b, h, t: (b, t, h)),
        out_shape=jax.ShapeDtypeStruct((Bn, T, H * A_DK), BF16),
        scratch_shapes=[pltpu.VMEM((A_DK, A_DK), F32), pltpu.VMEM((tt // A_SUB, A_DK, A_DK), F32),
                        pltpu.VMEM((tt // A_SUB, A_DK, A_DK), BF16)],
        compiler_params=pltpu.CompilerParams(
            dimension_semantics=("parallel", "parallel", "arbitrary"), vmem_limit_bytes=VMEM_LIMIT),
        name="hgrn2",
    )(pa, pa, pa, pa, lb, ng)


def _sb_kernel(q_ref, k_ref, v_ref, o_ref, suf_ref, acc_ref, run_ref):
    tq = q_ref.shape[1]
    i = pl.program_id(2)
    in_h0 = lax.broadcasted_iota(jnp.int32, (tq, LANES), 1) < HEAD_DIM
    r = lax.broadcasted_iota(jnp.int32, (tq, tq), 0)
    c = lax.broadcasted_iota(jnp.int32, (tq, tq), 1)
    suf_ref[...] = (r >= c).astype(BF16)
    strict = jnp.concatenate([c < r, c < r], axis=0)

    def lanes(n):
        return slice(n * LANES, (n + 1) * LANES)

    q2s = []
    for n in range(SB_CHAINS):
        q = q_ref[0, :, lanes(n)]
        zero = jnp.zeros_like(q)
        q2s.append(jnp.concatenate([jnp.where(in_h0, q, zero), jnp.where(in_h0, zero, q)], axis=0)
                   * (HEAD_DIM ** -0.5))

    def tile(q2, kb, vb, run, diag):
        z = _dot_nt(q2, kb)
        sp = jnp.maximum(z, 0.0) + jnp.log(1.0 + jnp.exp(-jnp.abs(z)))
        if diag:
            sp = jnp.where(strict, sp, 0.0)
        hi, lo = _split_bf16(sp)
        suffix = suf_ref[...]
        cs = _dot(hi, suffix) + _dot(lo, suffix) + run
        a = jnp.exp(z - cs)
        if diag:
            a = jnp.where(strict, a, 0.0)
        return _dot(a.astype(BF16), vb), cs[:, 0:1]

    def step(rows, first):
        lowest = None
        for n in range(SB_CHAINS):
            run = jnp.zeros((2 * tq, 1), F32) if first else run_ref[n]
            pv, run = tile(q2s[n], k_ref[0, rows, lanes(n)], v_ref[0, rows, lanes(n)], run, first)
            acc_ref[n] = pv if first else acc_ref[n] + pv
            run_ref[n] = run
            low = jnp.min(run)
            lowest = low if lowest is None else jnp.minimum(lowest, low)
        return lowest

    lowest = step(pl.ds(pl.multiple_of(i * tq, tq), tq), True)

    def cond(carry):
        s, lowest = carry
        return (s < i) & (lowest <= SB_EXIT)

    def body(carry):
        s, _ = carry
        return s + 1, step(pl.ds(pl.multiple_of((i - 1 - s) * tq, tq), tq), False)

    lax.while_loop(cond, body, (jnp.int32(0), lowest))
    for n in range(SB_CHAINS):
        acc = acc_ref[n]
        o_ref[0, :, lanes(n)] = jnp.where(in_h0, acc[:tq], acc[tq:]).astype(o_ref.dtype)


def _stick_breaking(pbd, tq):
    Bn, T, _ = pbd.shape
    ngrp = B_HEADS // 2 // SB_CHAINS
    w = SB_CHAINS * LANES
    return pl.pallas_call(
        _sb_kernel,
        grid=(Bn, ngrp, T // tq),
        in_specs=[pl.BlockSpec((1, tq, w), lambda b, p, i: (b, i, _BQ // SB_CHAINS + p)),
                  pl.BlockSpec((1, T, w), lambda b, p, i: (b, 0, _BK // SB_CHAINS + p)),
                  pl.BlockSpec((1, T, w), lambda b, p, i: (b, 0, _BV // SB_CHAINS + p))],
        out_specs=pl.BlockSpec((1, tq, w), lambda b, p, i: (b, i, p)),
        out_shape=jax.ShapeDtypeStruct((Bn, T, B_HEADS // 2 * LANES), BF16),
        scratch_shapes=[pltpu.VMEM((tq, tq), BF16), pltpu.VMEM((SB_CHAINS, 2 * tq, LANES), F32),
                        pltpu.VMEM((SB_CHAINS, 2 * tq, 1), F32)],
        compiler_params=pltpu.CompilerParams(
            dimension_semantics=("parallel", "parallel", "arbitrary"), vmem_limit_bytes=VMEM_LIMIT),
        name="stick_breaking",
    )(pbd, pbd, pbd)


def _win_kernel(q_ref, k_ref, kp_ref, v_ref, vp_ref, bias_ref, sink_ref, *rest, dil, nsub, nq, n_mix):
    mix_refs, out_refs = rest[:2 * n_mix], rest[2 * n_mix + (1 if n_mix else 0):]
    blk = WIN_BLOCK
    span = blk * dil
    nrow = 2 * nq * blk
    i = pl.program_id(2)
    lane = lax.broadcasted_iota(jnp.int32, (blk, LANES), 1)
    in_h0 = lane < HEAD_DIM
    prev_col = lax.broadcasted_iota(jnp.int32, (nrow, 2 * blk), 1) < blk
    ones = jnp.ones((2 * blk, LANES), BF16)
    sinks = jnp.concatenate([jnp.broadcast_to(sink_ref[hh:hh + 1, 0:1], (blk, 1)) for hh in range(2 * nq)],
                            axis=0)
    bias = bias_ref[...]

    def rows_of(start):
        return pl.ds(start, blk, stride=dil) if dil > 1 else pl.ds(start, blk)

    def residue_class(s, c):
        cur = rows_of(s * span + c)
        q = q_ref[0, cur, :].astype(BF16)
        if s == 0:
            kp, vp = kp_ref[0, rows_of(c), :], vp_ref[0, rows_of(c), :]
        else:
            prev = rows_of((s - 1) * span + c)
            kp, vp = k_ref[0, prev, :], v_ref[0, prev, :]
        k2 = jnp.concatenate([kp.astype(BF16), k_ref[0, cur, :].astype(BF16)], axis=0)
        v2 = jnp.concatenate([vp.astype(BF16), v_ref[0, cur, :].astype(BF16)], axis=0)
        qs = [q[:, p * LANES:(p + 1) * LANES] for p in range(nq)]
        zero = jnp.zeros_like(qs[0])
        q2 = jnp.concatenate([jnp.where(in_h0, x, zero) for x in qs]
                             + [jnp.where(in_h0, zero, x) for x in qs], axis=0)
        logits = _dot_nt(q2 * (HEAD_DIM ** -0.5), k2) + bias
        if s == 0:
            logits = jnp.where(jnp.logical_and(i == 0, prev_col), -jnp.inf, logits)
        m = jnp.max(jnp.maximum(logits[:, :blk], logits[:, blk:]), axis=-1, keepdims=True)
        m = jnp.maximum(m, sinks)
        p = jnp.exp(logits - m).astype(BF16)
        if nq > 1:
            pv = _dot(p, jnp.concatenate([v2, ones], axis=1))
            num, den = pv[:, :LANES], pv[:, LANES:]
        else:
            num, den = _dot(p, v2), _dot(p, ones)
        den = den + jnp.exp(sinks - m)
        o2 = num / den

        def lane_blocks(x):
            return jnp.concatenate([jnp.where(in_h0, x[b * blk:(b + 1) * blk], x[(nq + b) * blk:(nq + b + 1) * blk])
                                    for b in range(nq)], axis=1)

        out_refs[0][0, cur, :] = lane_blocks(o2).astype(out_refs[0].dtype)
        if len(out_refs) > 1:
            out_refs[1][0, cur, :] = lane_blocks(m + jnp.log(den))

    for s in range(nsub):
        if dil <= WIN_UNROLL:
            for c in range(dil):
                residue_class(s, c)
        else:
            def body(c, carry, s=s):
                residue_class(s, c)
                return carry
            lax.fori_loop(0, dil, body, 0, unroll=WIN_UNROLL)

    if n_mix:
        outs = [r[0] for r in mix_refs[0::2]] + [out_refs[0][0]]
        lses = [r[0] for r in mix_refs[1::2]] + [out_refs[1][0]]
        top = functools.reduce(jnp.maximum, lses)
        ws = [jnp.exp(l - top) for l in lses]
        y_ref = rest[2 * n_mix]
        y_ref[0] = (sum(w * o for w, o in zip(ws, outs)) / sum(ws)).astype(y_ref.dtype)


def _window_attn(slab, bias, sinks, dil, qcol, kcol, vcol, ngrp, nq, with_lse, out_dtype, mix=()):
    Bn, T, _ = slab.shape
    span = WIN_BLOCK * dil
    nsub = max(1, WIN_UNROLL // dil)
    tb = span * nsub
    assert qcol % nq == 0

    def kvmap(col, prev):
        def f(b, g, i):
            return (b, jnp.maximum(i * nsub - 1, 0) if prev else i, col + g)
        return f

    qblk = (1, tb, nq * LANES)
    blk = (1, tb, LANES)
    pblk = (1, span, LANES)
    out_spec = pl.BlockSpec(qblk, lambda b, g, i: (b, i, g))
    out_sds = jax.ShapeDtypeStruct((Bn, T, ngrp * nq * LANES), out_dtype)
    lse_sds = jax.ShapeDtypeStruct((Bn, T, ngrp * nq * LANES), F32)
    if mix:
        out_specs, out_shape = [out_spec], [out_sds]
        scratch = [pltpu.VMEM(qblk, F32), pltpu.VMEM(qblk, F32)]
    else:
        out_specs = [out_spec, out_spec] if with_lse else [out_spec]
        out_shape = [out_sds, lse_sds] if with_lse else [out_sds]
        scratch = []
    return pl.pallas_call(
        functools.partial(_win_kernel, dil=dil, nsub=nsub, nq=nq, n_mix=len(mix) // 2),
        grid=(Bn, ngrp, T // tb),
        in_specs=[pl.BlockSpec(qblk, lambda b, g, i: (b, i, qcol // nq + g)),
                  pl.BlockSpec(blk, kvmap(kcol, False)), pl.BlockSpec(pblk, kvmap(kcol, True)),
                  pl.BlockSpec(blk, kvmap(vcol, False)), pl.BlockSpec(pblk, kvmap(vcol, True)),
                  pl.BlockSpec((None, 2 * nq * WIN_BLOCK, 2 * WIN_BLOCK), lambda b, g, i: (g, 0, 0)),
                  pl.BlockSpec((None, 2 * nq, LANES), lambda b, g, i: (g, 0, 0))] + [out_spec] * len(mix),
        out_specs=out_specs,
        out_shape=out_shape,
        scratch_shapes=scratch,
        compiler_params=pltpu.CompilerParams(
            dimension_semantics=("parallel", "parallel", "arbitrary"), vmem_limit_bytes=VMEM_LIMIT),
        name="window_attn",
    )(slab, slab, slab, slab, slab, bias, sinks, *mix)


def _merge_kernel(x_ref, g_ref, ya_ref, yb_ref, yc_ref, yd_ref,
                  ga_ref, gb_ref, gc_ref, gd_ref, wa_ref, wb_ref, wc_ref, wd_ref, wo_ref,
                  o_ref, u_ref):
    j = pl.program_id(1)

    @pl.when(j == 0)
    def _():
        u_ref[...] = _rms(x_ref[...], g_ref[0:1, :]).astype(BF16)
        o_ref[...] = jnp.zeros_like(o_ref)

    u = u_ref[...]
    merged = None
    for y_ref, gate_ref, w_ref in ((ya_ref, ga_ref, wa_ref), (yb_ref, gb_ref, wb_ref),
                                   (yc_ref, gc_ref, wc_ref), (yd_ref, gd_ref, wd_ref)):
        term = _sigmoid(_dot(u, gate_ref[...])) * _dot(y_ref[...], w_ref[...])
        merged = term if merged is None else merged + term
    o_ref[...] += _dot(merged.astype(BF16), wo_ref[...])

    @pl.when(j == pl.num_programs(1) - 1)
    def _():
        o_ref[...] = x_ref[...] + _rms(o_ref[...], g_ref[1:2, :])


def _merge(h, norm, ys, w_in, gate_col0, w_branch, w_out, layer, tm, tn):
    M, D = h.shape
    gcb = gate_col0 // tn
    dcb = D // tn

    def yspec(w):
        return pl.BlockSpec((tm, w), lambda i, j: (i, 0))

    def gspec(br):
        return pl.BlockSpec((None, D, tn), lambda i, j: (layer, 0, gcb + br * dcb + j))

    def wspec(rows, rb):
        return pl.BlockSpec((None, rows, tn), lambda i, j: (layer, rb, j))

    wa, wb, wc, wd = (y.shape[1] for y in ys)
    return pl.pallas_call(
        _merge_kernel,
        grid=(M // tm, dcb),
        in_specs=[pl.BlockSpec((tm, D), lambda i, j: (i, 0)),
                  pl.BlockSpec((None, 2, D), lambda i, j: (layer, 0, 0)),
                  yspec(wa), yspec(wb), yspec(wc), yspec(wd),
                  gspec(0), gspec(1), gspec(2), gspec(3),
                  wspec(wa, 0), wspec(wb, wa // wb), wspec(wc, (wa + wb + wd) // wc),
                  wspec(wd, (wa + wb) // wd),
                  pl.BlockSpec((None, tn, D), lambda i, j: (layer, j, 0))],
        out_specs=pl.BlockSpec((tm, D), lambda i, j: (i, 0)),
        out_shape=jax.ShapeDtypeStruct((M, D), F32),
        scratch_shapes=[pltpu.VMEM((tm, D), BF16)],
        compiler_params=pltpu.CompilerParams(
            dimension_semantics=("parallel", "arbitrary"), vmem_limit_bytes=VMEM_LIMIT),
        name="gated_merge",
    )(h, norm, *ys, w_in, w_in, w_in, w_in, w_branch, w_branch, w_branch, w_branch, w_out)


def _rel_bucket(dist):
    max_exact = REL_BUCKETS // 2
    d = jnp.maximum(dist, 1).astype(F32)
    large = max_exact + (jnp.log(d / max_exact) / math.log(REL_MAX_DIST / max_exact)
                         * (REL_BUCKETS - max_exact)).astype(jnp.int32)
    large = jnp.minimum(large, REL_BUCKETS - 1)
    return jnp.where(dist < max_exact, dist, large)


def _window_bias(rel_bias, heads, dilation, max_dist, rows_heads=2):
    row = jnp.arange(WIN_BLOCK)[:, None]
    col = jnp.arange(2 * WIN_BLOCK)[None, :]
    dist = row + WIN_BLOCK - col
    valid = (dist >= 0) & (dist <= max_dist)
    assert list(heads) == list(range(heads[0], heads[-1] + 1))
    bucket = _rel_bucket(jnp.maximum(dist, 0) * dilation)
    per_head = rel_bias.astype(F32)[:, heads[0]:heads[-1] + 1]
    hit = bucket[None, None] == jnp.arange(REL_BUCKETS)[:, None, None, None]
    tab = jnp.sum(jnp.where(hit, per_head[:, :, None, None], 0.0), axis=0)
    tab = jnp.where(valid[None], tab, -jnp.inf)
    return tab.reshape(len(heads) // rows_heads, rows_heads * WIN_BLOCK, 2 * WIN_BLOCK)


def _trunk(x, ffn1_norm, ffn1_w13, ffn1_w2, mix_norm, w_in, hgrn_lb_logits, hgrn_out_norm,
           attn_sinks, w_branch, w_out, ffn2_norm, ffn2_w13, ffn2_w2, rel_bias, cfg):
    Bn, T, D = x.shape
    depth = w_in.shape[0]
    M = Bn * T
    a_w = A_HEADS * A_DK
    a_cols = 4 * a_w
    b_cols = 3 * B_HEADS * HEAD_DIM
    c_cols = 3 * C_HEADS * HEAD_DIM
    ac_cols = AC_BLOCKS * LANES
    bd_cols = BD_BLOCKS * LANES
    mix_in = ac_cols + bd_cols

    lb_sm = jax.nn.softmax(hgrn_lb_logits.astype(F32), axis=0)
    lb_all = (jnp.cumsum(lb_sm, axis=0) - lb_sm[0:1]).reshape(depth, A_HEADS, 1, A_DK)

    b0, c0 = a_cols, a_cols + b_cols
    dq0 = c0 + c_cols
    dk0 = dq0 + D_HEADS * HEAD_DIM
    c_w = C_HPG * HEAD_DIM

    perm = jnp.array([h * HEAD_DIM + d for h in D_HEAD_PERM for d in range(HEAD_DIM)])

    def c_part(part, g0, g1):
        base = c0 + part * C_HEADS * HEAD_DIM
        return w_in[:, :, base + g0 * c_w:base + g1 * c_w]

    n_grp = len(C_PATTERNS)
    w_in_b = jnp.concatenate(
        [w_in[:, :, :a_cols]] + [c_part(p, 1, n_grp) for p in range(3)]
        + [w_in[:, :, b0:c0], w_in[:, :, dq0:dk0][:, :, perm], w_in[:, :, dk0:mix_in]]
        + [c_part(p, 0, 1) for p in range(3)] + [w_in[:, :, mix_in:]], axis=-1).astype(BF16)
    d0 = 2 * a_w + c_w
    w_br_b = jnp.concatenate(
        [w_branch[:, :2 * a_w], w_branch[:, d0:][:, perm], w_branch[:, 2 * a_w:d0]], axis=1).astype(BF16)
    w_out_b = w_out.astype(BF16)
    f1_w13, f1_w2 = ffn1_w13.astype(BF16), ffn1_w2.astype(BF16)
    f2_w13, f2_w2 = ffn2_w13.astype(BF16), ffn2_w2.astype(BF16)

    c_bias = [_window_bias(rel_bias, list(range(g * C_HPG, (g + 1) * C_HPG)), dil, win // dil)
              for g, (win, dil) in enumerate(C_PATTERNS)]
    d_bias = _window_bias(rel_bias, [C_HEADS + h for h in range(D_HEADS)], 1, D_WINDOW - 1, D_HEADS)
    no_sink = jnp.full((C_HPG // 2, 2, LANES), -jnp.inf, F32)

    h = x.reshape(M, D)
    for l in range(depth):
        h = _ffn(h, ffn1_norm, f1_w13, f1_w2, l, cfg["ffn_tm"], cfg["ffn_tf"])

        pac, pbd = _proj(h, mix_norm, w_in_b, l, ac_cols, bd_cols, cfg["proj_tm"], cfg["proj_tn"])
        pac = pac.reshape(Bn, T, ac_cols)
        pbd = pbd.reshape(Bn, T, bd_cols)

        ya = _hgrn(pac, lb_all[l], hgrn_out_norm[l].reshape(1, 1, A_DK).astype(F32), cfg["hgrn_tt"])
        yb = _stick_breaking(pbd, cfg["sb_tq"])
        mix = ()
        for g, (win, dil) in enumerate(C_PATTERNS):
            if dil == 1:
                slab, cols = pbd, (_C0Q, _C0K, _C0V)
            else:
                slab, cols = pac, tuple(c + 2 * (g - 1) for c in (_CQ, _CK, _CV))
            if g < len(C_PATTERNS) - 1:
                mix += tuple(_window_attn(slab, c_bias[g], no_sink, dil, *cols, C_HPG // 2, 1, True, F32))
            else:
                (yc,) = _window_attn(slab, c_bias[g], no_sink, dil, *cols, C_HPG // 2, 1, True, BF16, mix)
        sinks = jnp.broadcast_to(attn_sinks[l].astype(F32).reshape(1, D_HEADS, 1), (1, D_HEADS, LANES))
        (yd,) = _window_attn(pbd, d_bias, sinks, 1, _DQ, _DK, _DV, 1, D_HEADS // 2, False, BF16)

        ys = (ya.reshape(M, a_w), yb.reshape(M, a_w), yc.reshape(M, c_w), yd.reshape(M, a_w))
        h = _merge(h, mix_norm, ys, w_in_b, mix_in, w_br_b, w_out_b, l, cfg["merge_tm"], cfg["merge_tn"])

        h = _ffn(h, ffn2_norm, f2_w13, f2_w2, l, cfg["ffn_tm"], cfg["ffn_tf"])
    return h.reshape(Bn, T, D)


_CFG = dict(ffn_tm=512, ffn_tf=512, proj_tm=1024, proj_tn=512, hgrn_tt=512, sb_tq=256,
            merge_tm=512, merge_tn=512)


def kernel(x, ffn1_norm, ffn1_w13, ffn1_w2, mix_norm, w_in, hgrn_lb_logits, hgrn_out_norm, attn_sinks,
           w_branch, w_out, ffn2_norm, ffn2_w13, ffn2_w2, rel_bias):
    return _trunk(x, ffn1_norm, ffn1_w13, ffn1_w2, mix_norm, w_in, hgrn_lb_logits, hgrn_out_norm,
                  attn_sinks, w_branch, w_out, ffn2_norm, ffn2_w13, ffn2_w2, rel_bias, _CFG)
```

```python
import functools
import math

import jax
import jax.numpy as jnp
from jax import lax
from jax.experimental import pallas as pl
from jax.experimental.pallas import tpu as pltpu

F32 = jnp.float32
BF16 = jnp.bfloat16

EPS = 1e-6
HEAD_DIM = 64
LANES = 128
A_HEADS = 4
A_DK = 128
B_HEADS = 8
C_PATTERNS = ((128, 1), (512, 4), (2048, 16))
C_HPG = 4
C_HEADS = len(C_PATTERNS) * C_HPG
D_HEADS = 8
D_KV_HEADS = 2
D_WINDOW = 128
WIN_BLOCK = 128
REL_BUCKETS = 32
REL_MAX_DIST = 2048
A_SUB = 16
VMEM_LIMIT = 60 * 1024 * 1024

_CQ, _CK, _CV = 16, 20, 24
AC_BLOCKS = 28
_BQ, _BK, _BV = 0, 4, 8
_DQ, _DK, _DV = 12, 16, 17
_C0Q, _C0K, _C0V = 18, 20, 22
BD_BLOCKS = 24
WIN_UNROLL = 4
D_HEAD_PERM = (0, 4, 1, 5, 2, 6, 3, 7)
SB_CHAINS = 4
SB_EXIT = 100.0


def _rms(x, g):
    ms = jnp.mean(x * x, axis=-1, keepdims=True)
    return x * lax.rsqrt(ms + EPS) * g


def _sigmoid(x):
    return 1.0 / (1.0 + jnp.exp(-x))


def _dot(a, b):
    return jnp.dot(a, b, preferred_element_type=F32)


def _dot_nt(a, b):
    return lax.dot_general(a, b, (((1,), (1,)), ((), ())), preferred_element_type=F32)


def _dot_tn(a, b):
    return lax.dot_general(a, b, (((0,), (0,)), ((), ())), preferred_element_type=F32)


def _split_bf16(x):
    hi = x.astype(BF16)
    lo = (x - hi.astype(F32)).astype(BF16)
    return hi, lo


def _ffn_kernel(x_ref, g_ref, w1a_ref, w3a_ref, w2a_ref, w1b_ref, w3b_ref, w2b_ref, o_ref, xn_ref, *, odd):
    j = pl.program_id(1)
    last = pl.num_programs(1) - 1

    @pl.when(j == 0)
    def _():
        xn_ref[...] = _rms(x_ref[...], g_ref[0:1, :]).astype(BF16)
        o_ref[...] = jnp.zeros_like(o_ref)

    def part(w1_ref, w3_ref, w2_ref):
        xn = xn_ref[...]
        gate = _dot(xn, w1_ref[...])
        up = _dot(xn, w3_ref[...])
        return _dot((gate * _sigmoid(gate) * up).astype(BF16), w2_ref[...])

    def both():
        o_ref[...] += part(w1a_ref, w3a_ref, w2a_ref) + part(w1b_ref, w3b_ref, w2b_ref)

    if odd:
        pl.when(j < last)(both)

        @pl.when(j == last)
        def _():
            o_ref[...] += part(w1a_ref, w3a_ref, w2a_ref)
    else:
        both()

    @pl.when(j == last)
    def _():
        o_ref[...] = x_ref[...] + 0.5 * _rms(o_ref[...], g_ref[1:2, :])


def _ffn(h, norm, w13, w2, layer, tm, tf):
    M, D = h.shape
    F = w2.shape[1]
    nf = F // tf

    def tile(j, b):
        return jnp.minimum(2 * j + b, nf - 1)

    def w_specs(b):
        return [pl.BlockSpec((None, None, D, tf), lambda i, j: (layer, tile(j, b), 0, 0)),
                pl.BlockSpec((None, None, D, tf), lambda i, j: (layer, tile(j, b) + nf, 0, 0)),
                pl.BlockSpec((None, tf, D), lambda i, j: (layer, tile(j, b), 0))]

    return pl.pallas_call(
        functools.partial(_ffn_kernel, odd=nf % 2 == 1),
        grid=(M // tm, (nf + 1) // 2),
        in_specs=[
            pl.BlockSpec((tm, D), lambda i, j: (i, 0)),
            pl.BlockSpec((None, 2, D), lambda i, j: (layer, 0, 0)),
        ] + w_specs(0) + w_specs(1),
        out_specs=pl.BlockSpec((tm, D), lambda i, j: (i, 0)),
        out_shape=jax.ShapeDtypeStruct((M, D), F32),
        scratch_shapes=[pltpu.VMEM((tm, D), BF16)],
        compiler_params=pltpu.CompilerParams(
            dimension_semantics=("parallel", "arbitrary"), vmem_limit_bytes=VMEM_LIMIT),
        name="ffn",
    )(h, norm, w13, w13, w2, w13, w13, w2)


def _proj_kernel(x_ref, g_ref, w_ref, o32_ref, o16_ref, xn_ref, *, n32):
    j = pl.program_id(1)

    @pl.when(j == 0)
    def _():
        xn_ref[...] = _rms(x_ref[...], g_ref[0:1, :]).astype(BF16)

    @pl.when(j < n32)
    def _():
        o32_ref[...] = _dot(xn_ref[...], w_ref[...])

    @pl.when(j >= n32)
    def _():
        o16_ref[...] = _dot(xn_ref[...], w_ref[...]).astype(BF16)


def _proj(h, norm, w_in, layer, cols32, cols16, tm, tn):
    M, D = h.shape
    n32, n16 = cols32 // tn, cols16 // tn
    return pl.pallas_call(
        functools.partial(_proj_kernel, n32=n32),
        grid=(M // tm, n32 + n16),
        in_specs=[
            pl.BlockSpec((tm, D), lambda i, j: (i, 0)),
            pl.BlockSpec((None, 2, D), lambda i, j: (layer, 0, 0)),
            pl.BlockSpec((None, D, tn), lambda i, j: (layer, 0, j)),
        ],
        out_specs=[pl.BlockSpec((tm, tn), lambda i, j: (i, jnp.minimum(j, n32 - 1))),
                   pl.BlockSpec((tm, tn), lambda i, j: (i, jnp.maximum(j - n32, 0)))],
        out_shape=[jax.ShapeDtypeStruct((M, cols32), F32), jax.ShapeDtypeStruct((M, cols16), BF16)],
        scratch_shapes=[pltpu.VMEM((tm, D), BF16)],
        compiler_params=pltpu.CompilerParams(
            dimension_semantics=("parallel", "arbitrary"), vmem_limit_bytes=VMEM_LIMIT),
        name="proj",
    )(h, norm, w_in)


def _hgrn_kernel(q_ref, f_ref, i_ref, g_ref, lb_ref, ng_ref, o_ref, st_ref, kv_ref, sall_ref):
    tt = q_ref.shape[1]
    sub = A_SUB
    nb = tt // sub
    per = LANES // sub
    shape3 = (nb, sub, LANES)

    @pl.when(pl.program_id(2) == 0)
    def _():
        st_ref[...] = jnp.zeros_like(st_ref)

    lb = lb_ref[0]
    f = lb + (1.0 - lb) * _sigmoid(f_ref[0])
    k = 1.0 - f
    gl = jnp.log2(f)
    r = lax.broadcasted_iota(jnp.int32, (LANES, LANES), 0)
    c = lax.broadcasted_iota(jnp.int32, (LANES, LANES), 1)
    tril = ((r // sub == c // sub) & (c <= r)).astype(BF16)
    cums = []
    for a in range(tt // LANES):
        hi, lo = _split_bf16(gl[a * LANES:(a + 1) * LANES])
        cums.append(_dot(tril, hi) + _dot(tril, lo))
    b = jnp.concatenate(cums, axis=0)
    q = q_ref[0]
    v = i_ref[0]
    b3, q3, v3 = (x.reshape(shape3) for x in (b, q, v))
    bk3 = b3 - jnp.log2(k).reshape(shape3)

    ones = jnp.ones((LANES, LANES), BF16)
    rows = lax.broadcasted_iota(jnp.int32, (1, sub, LANES), 1)
    o3 = jnp.zeros(shape3, F32)
    for s in range(sub):
        w = jnp.where(rows >= s, q3 * jnp.exp2(b3 - bk3[:, s:s + 1, :]), 0.0).astype(BF16)
        o3 = o3 + _dot(w.reshape(tt, LANES), ones).reshape(shape3) * v3[:, s:s + 1, :]

    b_last = b3[:, sub - 1:sub, :]
    kd = jnp.exp2(b_last - bk3).reshape(tt, LANES)
    dec = jnp.exp2(b_last)
    for a in range(tt // LANES):
        v_t = v[a * LANES:(a + 1) * LANES].T.astype(BF16)
        kd_a = kd[a * LANES:(a + 1) * LANES]
        rhs = jnp.concatenate([jnp.where(r // sub == j, kd_a, 0.0).astype(BF16) for j in range(per)], axis=1)
        kv = _dot(v_t, rhs)
        for j in range(per):
            kv_ref[a * per + j] = kv[:, j * LANES:(j + 1) * LANES]

    st = st_ref[...]
    for n in range(nb):
        sall_ref[n] = st.astype(BF16)
        st = dec[n] * st + kv_ref[n]
    st_ref[...] = st

    qe = (q * jnp.exp2(b)).astype(BF16)
    inter = []
    for a in range(tt // LANES):
        stack = sall_ref[a * per:(a + 1) * per].reshape(per * LANES, LANES)
        p = _dot_nt(qe[a * LANES:(a + 1) * LANES], stack)
        inter += [p[j * sub:(j + 1) * sub, j * LANES:(j + 1) * LANES] for j in range(per)]
    o = o3.reshape(tt, LANES) + jnp.concatenate(inter, axis=0)

    g = g_ref[0]
    o_ref[0] = (_rms(o, ng_ref[0]) * (g * _sigmoid(g))).astype(o_ref.dtype)


def _hgrn(pa, lb, ng, tt):
    Bn, T, _ = pa.shape
    H = A_HEADS

    def col(off):
        return pl.BlockSpec((1, tt, A_DK), lambda b, h, t: (b, t, off * H + h))

    return pl.pallas_call(
        _hgrn_kernel,
        grid=(Bn, H, T // tt),
        in_specs=[col(0), col(1), col(2), col(3),
                  pl.BlockSpec((1, 1, A_DK), lambda b, h, t: (h, 0, 0)),
                  pl.BlockSpec((1, 1, A_DK), lambda b, h, t: (0, 0, 0))],
        out_specs=pl.BlockSpec((1, tt, A_DK), lambda b, h, t: (b, t, h)),
        out_shape=jax.ShapeDtypeStruct((Bn, T, H * A_DK), BF16),
        scratch_shapes=[pltpu.VMEM((A_DK, A_DK), F32), pltpu.VMEM((tt // A_SUB, A_DK, A_DK), F32),
                        pltpu.VMEM((tt // A_SUB, A_DK, A_DK), BF16)],
        compiler_params=pltpu.CompilerParams(
            dimension_semantics=("parallel", "parallel", "arbitrary"), vmem_limit_bytes=VMEM_LIMIT),
        name="hgrn2",
    )(pa, pa, pa, pa, lb, ng)


def _sb_kernel(q_ref, k_ref, v_ref, o_ref, suf_ref, acc_ref, run_ref):
    tq = q_ref.shape[1]
    i = pl.program_id(2)
    in_h0 = lax.broadcasted_iota(jnp.int32, (tq, LANES), 1) < HEAD_DIM
    r = lax.broadcasted_iota(jnp.int32, (tq, tq), 0)
    c = lax.broadcasted_iota(jnp.int32, (tq, tq), 1)
    suf_ref[...] = (r >= c).astype(BF16)
    strict = jnp.concatenate([c < r, c < r], axis=0)

    def lanes(n):
        return slice(n * LANES, (n + 1) * LANES)

    q2s = []
    for n in range(SB_CHAINS):
        q = q_ref[0, :, lanes(n)]
        zero = jnp.zeros_like(q)
        q2s.append(jnp.concatenate([jnp.where(in_h0, q, zero), jnp.where(in_h0, zero, q)], axis=0)
                   * (HEAD_DIM ** -0.5))

    def tile(q2, kb, vb, run, diag):
        z = _dot_nt(q2, kb)
        sp = jnp.maximum(z, 0.0) + jnp.log(1.0 + jnp.exp(-jnp.abs(z)))
        if diag:
            sp = jnp.where(strict, sp, 0.0)
        hi, lo = _split_bf16(sp)
        suffix = suf_ref[...]
        cs = _dot(hi, suffix) + _dot(lo, suffix) + run
        a = jnp.exp(z - cs)
        if diag:
            a = jnp.where(strict, a, 0.0)
        return _dot(a.astype(BF16), vb), cs[:, 0:1]

    def step(rows, first):
        lowest = None
        for n in range(SB_CHAINS):
            run = jnp.zeros((2 * tq, 1), F32) if first else run_ref[n]
            pv, run = tile(q2s[n], k_ref[0, rows, lanes(n)], v_ref[0, rows, lanes(n)], run, first)
            acc_ref[n] = pv if first else acc_ref[n] + pv
            run_ref[n] = run
            low = jnp.min(run)
            lowest = low if lowest is None else jnp.minimum(lowest, low)
        return lowest

    lowest = step(pl.ds(pl.multiple_of(i * tq, tq), tq), True)

    def cond(carry):
        s, lowest = carry
        return (s < i) & (lowest <= SB_EXIT)

    def body(carry):
        s, _ = carry
        return s + 1, step(pl.ds(pl.multiple_of((i - 1 - s) * tq, tq), tq), False)

    lax.while_loop(cond, body, (jnp.int32(0), lowest))
    for n in range(SB_CHAINS):
        acc = acc_ref[n]
        o_ref[0, :, lanes(n)] = jnp.where(in_h0, acc[:tq], acc[tq:]).astype(o_ref.dtype)


def _stick_breaking(pbd, tq):
    Bn, T, _ = pbd.shape
    ngrp = B_HEADS // 2 // SB_CHAINS
    w = SB_CHAINS * LANES
    return pl.pallas_call(
        _sb_kernel,
        grid=(Bn, ngrp, T // tq),
        in_specs=[pl.BlockSpec((1, tq, w), lambda b, p, i: (b, i, _BQ // SB_CHAINS + p)),
                  pl.BlockSpec((1, T, w), lambda b, p, i: (b, 0, _BK // SB_CHAINS + p)),
                  pl.BlockSpec((1, T, w), lambda b, p, i: (b, 0, _BV // SB_CHAINS + p))],
        out_specs=pl.BlockSpec((1, tq, w), lambda b, p, i: (b, i, p)),
        out_shape=jax.ShapeDtypeStruct((Bn, T, B_HEADS // 2 * LANES), BF16),
        scratch_shapes=[pltpu.VMEM((tq, tq), BF16), pltpu.VMEM((SB_CHAINS, 2 * tq, LANES), F32),
                        pltpu.VMEM((SB_CHAINS, 2 * tq, 1), F32)],
        compiler_params=pltpu.CompilerParams(
            dimension_semantics=("parallel", "parallel", "arbitrary"), vmem_limit_bytes=VMEM_LIMIT),
        name="stick_breaking",
    )(pbd, pbd, pbd)


def _win_kernel(q_ref, k_ref, kp_ref, v_ref, vp_ref, bias_ref, sink_ref, *rest, dil, nsub, nq, n_mix):
    mix_refs, out_refs = rest[:2 * n_mix], rest[2 * n_mix + (1 if n_mix else 0):]
    blk = WIN_BLOCK
    span = blk * dil
    nrow = 2 * nq * blk
    i = pl.program_id(2)
    lane = lax.broadcasted_iota(jnp.int32, (blk, LANES), 1)
    in_h0 = lane < HEAD_DIM
    prev_col = lax.broadcasted_iota(jnp.int32, (nrow, 2 * blk), 1) < blk
    ones = jnp.ones((2 * blk, LANES), BF16)
    sinks = jnp.concatenate([jnp.broadcast_to(sink_ref[hh:hh + 1, 0:1], (blk, 1)) for hh in range(2 * nq)],
                            axis=0)
    bias = bias_ref[...]

    def rows_of(start):
        return pl.ds(start, blk, stride=dil) if dil > 1 else pl.ds(start, blk)

    def residue_class(s, c):
        cur = rows_of(s * span + c)
        q = q_ref[0, cur, :].astype(BF16)
        if s == 0:
            kp, vp = kp_ref[0, rows_of(c), :], vp_ref[0, rows_of(c), :]
        else:
            prev = rows_of((s - 1) * span + c)
            kp, vp = k_ref[0, prev, :], v_ref[0, prev, :]
        k2 = jnp.concatenate([kp.astype(BF16), k_ref[0, cur, :].astype(BF16)], axis=0)
        v2 = jnp.concatenate([vp.astype(BF16), v_ref[0, cur, :].astype(BF16)], axis=0)
        qs = [q[:, p * LANES:(p + 1) * LANES] for p in range(nq)]
        zero = jnp.zeros_like(qs[0])
        q2 = jnp.concatenate([jnp.where(in_h0, x, zero) for x in qs]
                             + [jnp.where(in_h0, zero, x) for x in qs], axis=0)
        logits = _dot_nt(q2 * (HEAD_DIM ** -0.5), k2) + bias
        if s == 0:
            logits = jnp.where(jnp.logical_and(i == 0, prev_col), -jnp.inf, logits)
        m = jnp.max(jnp.maximum(logits[:, :blk], logits[:, blk:]), axis=-1, keepdims=True)
        m = jnp.maximum(m, sinks)
        p = jnp.exp(logits - m).astype(BF16)
        if nq > 1:
            pv = _dot(p, jnp.concatenate([v2, ones], axis=1))
            num, den = pv[:, :LANES], pv[:, LANES:]
        else:
            num, den = _dot(p, v2), _dot(p, ones)
        den = den + jnp.exp(sinks - m)
        o2 = num / den

        def lane_blocks(x):
            return jnp.concatenate([jnp.where(in_h0, x[b * blk:(b + 1) * blk], x[(nq + b) * blk:(nq + b + 1) * blk])
                                    for b in range(nq)], axis=1)

        out_refs[0][0, cur, :] = lane_blocks(o2).astype(out_refs[0].dtype)
        if len(out_refs) > 1:
            out_refs[1][0, cur, :] = lane_blocks(m + jnp.log(den))

    for s in range(nsub):
        if dil <= WIN_UNROLL:
            for c in range(dil):
                residue_class(s, c)
        else:
            def body(c, carry, s=s):
                residue_class(s, c)
                return carry
            lax.fori_loop(0, dil, body, 0, unroll=WIN_UNROLL)

    if n_mix:
        outs = [r[0] for r in mix_refs[0::2]] + [out_refs[0][0]]
        lses = [r[0] for r in mix_refs[1::2]] + [out_refs[1][0]]
        top = functools.reduce(jnp.maximum, lses)
        ws = [jnp.exp(l - top) for l in lses]
        y_ref = rest[2 * n_mix]
        y_ref[0] = (sum(w * o for w, o in zip(ws, outs)) / sum(ws)).astype(y_ref.dtype)


def _window_attn(slab, bias, sinks, dil, qcol, kcol, vcol, ngrp, nq, with_lse, out_dtype, mix=()):
    Bn, T, _ = slab.shape
    span = WIN_BLOCK * dil
    nsub = max(1, WIN_UNROLL // dil)
    tb = span * nsub
    assert qcol % nq == 0

    def kvmap(col, prev):
        def f(b, g, i):
            return (b, jnp.maximum(i * nsub - 1, 0) if prev else i, col + g)
        return f

    qblk = (1, tb, nq * LANES)
    blk = (1, tb, LANES)
    pblk = (1, span, LANES)
    out_spec = pl.BlockSpec(qblk, lambda b, g, i: (b, i, g))
    out_sds = jax.ShapeDtypeStruct((Bn, T, ngrp * nq * LANES), out_dtype)
    lse_sds = jax.ShapeDtypeStruct((Bn, T, ngrp * nq * LANES), F32)
    if mix:
        out_specs, out_shape = [out_spec], [out_sds]
        scratch = [pltpu.VMEM(qblk, F32), pltpu.VMEM(qblk, F32)]
    else:
        out_specs = [out_spec, out_spec] if with_lse else [out_spec]
        out_shape = [out_sds, lse_sds] if with_lse else [out_sds]
        scratch = []
    return pl.pallas_call(
        functools.partial(_win_kernel, dil=dil, nsub=nsub, nq=nq, n_mix=len(mix) // 2),
        grid=(Bn, ngrp, T // tb),
        in_specs=[pl.BlockSpec(qblk, lambda b, g, i: (b, i, qcol // nq + g)),
                  pl.BlockSpec(blk, kvmap(kcol, False)), pl.BlockSpec(pblk, kvmap(kcol, True)),
                  pl.BlockSpec(blk, kvmap(vcol, False)), pl.BlockSpec(pblk, kvmap(vcol, True)),
                  pl.BlockSpec((None, 2 * nq * WIN_BLOCK, 2 * WIN_BLOCK), lambda b, g, i: (g, 0, 0)),
                  pl.BlockSpec((None, 2 * nq, LANES), lambda b, g, i: (g, 0, 0))] + [out_spec] * len(mix),
        out_specs=out_specs,
        out_shape=out_shape,
        scratch_shapes=scratch,
        compiler_params=pltpu.CompilerParams(
            dimension_semantics=("parallel", "parallel", "arbitrary"), vmem_limit_bytes=VMEM_LIMIT),
        name="window_attn",
    )(slab, slab, slab, slab, slab, bias, sinks, *mix)


def _merge_kernel(x_ref, g_ref, ya_ref, yb_ref, yc_ref, yd_ref,
                  ga_ref, gb_ref, gc_ref, gd_ref, wa_ref, wb_ref, wc_ref, wd_ref, wo_ref,
                  o_ref, u_ref):
    j = pl.program_id(1)

    @pl.when(j == 0)
    def _():
        u_ref[...] = _rms(x_ref[...], g_ref[0:1, :]).astype(BF16)
        o_ref[...] = jnp.zeros_like(o_ref)

    u = u_ref[...]
    merged = None
    for y_ref, gate_ref, w_ref in ((ya_ref, ga_ref, wa_ref), (yb_ref, gb_ref, wb_ref),
                                   (yc_ref, gc_ref, wc_ref), (yd_ref, gd_ref, wd_ref)):
        term = _sigmoid(_dot(u, gate_ref[...])) * _dot(y_ref[...], w_ref[...])
        merged = term if merged is None else merged + term
    o_ref[...] += _dot(merged.astype(BF16), wo_ref[...])

    @pl.when(j == pl.num_programs(1) - 1)
    def _():
        o_ref[...] = x_ref[...] + _rms(o_ref[...], g_ref[1:2, :])


def _merge(h, norm, ys, w_in, gate_col0, w_branch, w_out, layer, tm, tn):
    M, D = h.shape
    gcb = gate_col0 // tn
    dcb = D // tn

    def yspec(w):
        return pl.BlockSpec((tm, w), lambda i, j: (i, 0))

    def gspec(br):
        return pl.BlockSpec((None, D, tn), lambda i, j: (layer, 0, gcb + br * dcb + j))

    def wspec(rows, rb):
        return pl.BlockSpec((None, rows, tn), lambda i, j: (layer, rb, j))

    wa, wb, wc, wd = (y.shape[1] for y in ys)
    return pl.pallas_call(
        _merge_kernel,
        grid=(M // tm, dcb),
        in_specs=[pl.BlockSpec((tm, D), lambda i, j: (i, 0)),
                  pl.BlockSpec((None, 2, D), lambda i, j: (layer, 0, 0)),
                  yspec(wa), yspec(wb), yspec(wc), yspec(wd),
                  gspec(0), gspec(1), gspec(2), gspec(3),
                  wspec(wa, 0), wspec(wb, wa // wb), wspec(wc, (wa + wb + wd) // wc),
                  wspec(wd, (wa + wb) // wd),
                  pl.BlockSpec((None, tn, D), lambda i, j: (layer, j, 0))],
        out_specs=pl.BlockSpec((tm, D), lambda i, j: (i, 0)),
        out_shape=jax.ShapeDtypeStruct((M, D), F32),
        scratch_shapes=[pltpu.VMEM((tm, D), BF16)],
        compiler_params=pltpu.CompilerParams(
            dimension_semantics=("parallel", "arbitrary"), vmem_limit_bytes=VMEM_LIMIT),
        name="gated_merge",
    )(h, norm, *ys, w_in, w_in, w_in, w_in, w_branch, w_branch, w_branch, w_branch, w_out)


def _rel_bucket(dist):
    max_exact = REL_BUCKETS // 2
    d = jnp.maximum(dist, 1).astype(F32)
    large = max_exact + (jnp.log(d / max_exact) / math.log(REL_MAX_DIST / max_exact)
                         * (REL_BUCKETS - max_exact)).astype(jnp.int32)
    large = jnp.minimum(large, REL_BUCKETS - 1)
    return jnp.where(dist < max_exact, dist, large)


def _window_bias(rel_bias, heads, dilation, max_dist, rows_heads=2):
    row = jnp.arange(WIN_BLOCK)[:, None]
    col = jnp.arange(2 * WIN_BLOCK)[None, :]
    dist = row + WIN_BLOCK - col
    valid = (dist >= 0) & (dist <= max_dist)
    assert list(heads) == list(range(heads[0], heads[-1] + 1))
    bucket = _rel_bucket(jnp.maximum(dist, 0) * dilation)
    per_head = rel_bias.astype(F32)[:, heads[0]:heads[-1] + 1]
    hit = bucket[None, None] == jnp.arange(REL_BUCKETS)[:, None, None, None]
    tab = jnp.sum(jnp.where(hit, per_head[:, :, None, None], 0.0), axis=0)
    tab = jnp.where(valid[None], tab, -jnp.inf)
    return tab.reshape(len(heads) // rows_heads, rows_heads * WIN_BLOCK, 2 * WIN_BLOCK)


def _trunk(x, ffn1_norm, ffn1_w13, ffn1_w2, mix_norm, w_in, hgrn_lb_logits, hgrn_out_norm,
           attn_sinks, w_branch, w_out, ffn2_norm, ffn2_w13, ffn2_w2, rel_bias, cfg):
    Bn, T, D = x.shape
    depth = w_in.shape[0]
    M = Bn * T
    a_w = A_HEADS * A_DK
    a_cols = 4 * a_w
    b_cols = 3 * B_HEADS * HEAD_DIM
    c_cols = 3 * C_HEADS * HEAD_DIM
    ac_cols = AC_BLOCKS * LANES
    bd_cols = BD_BLOCKS * LANES
    mix_in = ac_cols + bd_cols

    lb_sm = jax.nn.softmax(hgrn_lb_logits.astype(F32), axis=0)
    lb_all = (jnp.cumsum(lb_sm, axis=0) - lb_sm[0:1]).reshape(depth, A_HEADS, 1, A_DK)

    b0, c0 = a_cols, a_cols + b_cols
    dq0 = c0 + c_cols
    dk0 = dq0 + D_HEADS * HEAD_DIM
    c_w = C_HPG * HEAD_DIM

    perm = jnp.array([h * HEAD_DIM + d for h in D_HEAD_PERM for d in range(HEAD_DIM)])

    def c_part(part, g0, g1):
        base = c0 + part * C_HEADS * HEAD_DIM
        return w_in[:, :, base + g0 * c_w:base + g1 * c_w]

    n_grp = len(C_PATTERNS)
    w_in_b = jnp.concatenate(
        [w_in[:, :, :a_cols]] + [c_part(p, 1, n_grp) for p in range(3)]
        + [w_in[:, :, b0:c0], w_in[:, :, dq0:dk0][:, :, perm], w_in[:, :, dk0:mix_in]]
        + [c_part(p, 0, 1) for p in range(3)] + [w_in[:, :, mix_in:]], axis=-1).astype(BF16)
    d0 = 2 * a_w + c_w
    w_br_b = jnp.concatenate(
        [w_branch[:, :2 * a_w], w_branch[:, d0:][:, perm], w_branch[:, 2 * a_w:d0]], axis=1).astype(BF16)
    w_out_b = w_out.astype(BF16)
    def col_tiles(w, tn):
        return jnp.swapaxes(w.astype(BF16).reshape(depth, w.shape[1], -1, tn), 1, 2)

    f1_w13, f1_w2 = col_tiles(ffn1_w13, cfg["ffn_tf"]), ffn1_w2.astype(BF16)
    f2_w13, f2_w2 = col_tiles(ffn2_w13, cfg["ffn_tf"]), ffn2_w2.astype(BF16)

    c_bias = [_window_bias(rel_bias, list(range(g * C_HPG, (g + 1) * C_HPG)), dil, win // dil)
              for g, (win, dil) in enumerate(C_PATTERNS)]
    d_bias = _window_bias(rel_bias, [C_HEADS + h for h in range(D_HEADS)], 1, D_WINDOW - 1, D_HEADS)
    no_sink = jnp.full((C_HPG // 2, 2, LANES), -jnp.inf, F32)

    h = x.reshape(M, D)
    for l in range(depth):
        h = _ffn(h, ffn1_norm, f1_w13, f1_w2, l, cfg["ffn_tm"], cfg["ffn_tf"])

        pac, pbd = _proj(h, mix_norm, w_in_b, l, ac_cols, bd_cols, cfg["proj_tm"], cfg["proj_tn"])
        pac = pac.reshape(Bn, T, ac_cols)
        pbd = pbd.reshape(Bn, T, bd_cols)

        ya = _hgrn(pac, lb_all[l], hgrn_out_norm[l].reshape(1, 1, A_DK).astype(F32), cfg["hgrn_tt"])
        yb = _stick_breaking(pbd, cfg["sb_tq"])
        mix = ()
        for g, (win, dil) in enumerate(C_PATTERNS):
            if dil == 1:
                slab, cols = pbd, (_C0Q, _C0K, _C0V)
            else:
                slab, cols = pac, tuple(c + 2 * (g - 1) for c in (_CQ, _CK, _CV))
            if g < len(C_PATTERNS) - 1:
                mix += tuple(_window_attn(slab, c_bias[g], no_sink, dil, *cols, C_HPG // 2, 1, True, F32))
            else:
                (yc,) = _window_attn(slab, c_bias[g], no_sink, dil, *cols, C_HPG // 2, 1, True, BF16, mix)
        sinks = jnp.broadcast_to(attn_sinks[l].astype(F32).reshape(1, D_HEADS, 1), (1, D_HEADS, LANES))
        (yd,) = _window_attn(pbd, d_bias, sinks, 1, _DQ, _DK, _DV, 1, D_HEADS // 2, False, BF16)

        ys = (ya.reshape(M, a_w), yb.reshape(M, a_w), yc.reshape(M, c_w), yd.reshape(M, a_w))
        h = _merge(h, mix_norm, ys, w_in_b, mix_in, w_br_b, w_out_b, l, cfg["merge_tm"], cfg["merge_tn"])

        h = _ffn(h, ffn2_norm, f2_w13, f2_w2, l, cfg["ffn_tm"], cfg["ffn_tf"])
    return h.reshape(Bn, T, D)


_CFG = dict(ffn_tm=512, ffn_tf=512, proj_tm=1024, proj_tn=512, hgrn_tt=512, sb_tq=256,
            merge_tm=512, merge_tn=512)


def kernel(x, ffn1_norm, ffn1_w13, ffn1_w2, mix_norm, w_in, hgrn_lb_logits, hgrn_out_norm, attn_sinks,
           w_branch, w_out, ffn2_norm, ffn2_w13, ffn2_w2, rel_bias):
    return _trunk(x, ffn1_norm, ffn1_w13, ffn1_w2, mix_norm, w_in, hgrn_lb_logits, hgrn_out_norm,
                  attn_sinks, w_branch, w_out, ffn2_norm, ffn2_w13, ffn2_w2, rel_bias, _CFG)
```

```python
import functools
import math

import jax
import jax.numpy as jnp
from jax import lax
from jax.experimental import pallas as pl
from jax.experimental.pallas import tpu as pltpu

F32 = jnp.float32
BF16 = jnp.bfloat16

EPS = 1e-6
HEAD_DIM = 64
LANES = 128
A_HEADS = 4
A_DK = 128
B_HEADS = 8
C_PATTERNS = ((128, 1), (512, 4), (2048, 16))
C_HPG = 4
C_HEADS = len(C_PATTERNS) * C_HPG
D_HEADS = 8
D_KV_HEADS = 2
D_WINDOW = 128
WIN_BLOCK = 128
REL_BUCKETS = 32
REL_MAX_DIST = 2048
A_SUB = 16
VMEM_LIMIT = 60 * 1024 * 1024

_CQ, _CK, _CV = 16, 20, 24
AC_BLOCKS = 28
_BQ, _BK, _BV = 0, 4, 8
_DQ, _DK, _DV = 12, 16, 17
_C0Q, _C0K, _C0V = 18, 20, 22
BD_BLOCKS = 24
WIN_UNROLL = 4
D_HEAD_PERM = (0, 4, 1, 5, 2, 6, 3, 7)
SB_CHAINS = 4
SB_EXIT = 100.0


def _rms(x, g):
    ms = jnp.mean(x * x, axis=-1, keepdims=True)
    return x * lax.rsqrt(ms + EPS) * g


def _sigmoid(x):
    return 1.0 / (1.0 + jnp.exp(-x))


def _dot(a, b):
    return jnp.dot(a, b, preferred_element_type=F32)


def _dot_nt(a, b):
    return lax.dot_general(a, b, (((1,), (1,)), ((), ())), preferred_element_type=F32)


def _dot_tn(a, b):
    return lax.dot_general(a, b, (((0,), (0,)), ((), ())), preferred_element_type=F32)


def _split_bf16(x):
    hi = x.astype(BF16)
    lo = (x - hi.astype(F32)).astype(BF16)
    return hi, lo


def _ffn_kernel(x_ref, g_ref, w1_ref, w3_ref, w2_ref, o_ref, xn_ref):
    j = pl.program_id(1)

    @pl.when(j == 0)
    def _():
        xn_ref[...] = _rms(x_ref[...], g_ref[0:1, :]).astype(BF16)
        o_ref[...] = jnp.zeros_like(o_ref)

    xn = xn_ref[...]
    gate = _dot(xn, w1_ref[...])
    up = _dot(xn, w3_ref[...])
    act = (gate * _sigmoid(gate) * up).astype(BF16)
    o_ref[...] += _dot(act, w2_ref[...])

    @pl.when(j == pl.num_programs(1) - 1)
    def _():
        o_ref[...] = x_ref[...] + 0.5 * _rms(o_ref[...], g_ref[1:2, :])


def _ffn(h, norm, w13, w2, layer, tm, tf):
    M, D = h.shape
    F = w2.shape[1]
    nf = F // tf
    return pl.pallas_call(
        _ffn_kernel,
        grid=(M // tm, nf),
        in_specs=[
            pl.BlockSpec((tm, D), lambda i, j: (i, 0)),
            pl.BlockSpec((None, 2, D), lambda i, j: (layer, 0, 0)),
            pl.BlockSpec((None, D, tf), lambda i, j: (layer, 0, j)),
            pl.BlockSpec((None, D, tf), lambda i, j: (layer, 0, j + nf)),
            pl.BlockSpec((None, tf, D), lambda i, j: (layer, j, 0)),
        ],
        out_specs=pl.BlockSpec((tm, D), lambda i, j: (i, 0)),
        out_shape=jax.ShapeDtypeStruct((M, D), F32),
        scratch_shapes=[pltpu.VMEM((tm, D), BF16)],
        compiler_params=pltpu.CompilerParams(
            dimension_semantics=("parallel", "arbitrary"), vmem_limit_bytes=VMEM_LIMIT),
        name="ffn",
    )(h, norm, w13, w13, w2)


def _proj_kernel(x_ref, g_ref, w_ref, o32_ref, o16_ref, xn_ref, *, n32):
    j = pl.program_id(1)

    @pl.when(j == 0)
    def _():
        xn_ref[...] = _rms(x_ref[...], g_ref[0:1, :]).astype(BF16)

    @pl.when(j < n32)
    def _():
        o32_ref[...] = _dot(xn_ref[...], w_ref[...])

    @pl.when(j >= n32)
    def _():
        o16_ref[...] = _dot(xn_ref[...], w_ref[...]).astype(BF16)


def _proj(h, norm, w_in, layer, cols32, cols16, tm, tn):
    M, D = h.shape
    n32, n16 = cols32 // tn, cols16 // tn
    return pl.pallas_call(
        functools.partial(_proj_kernel, n32=n32),
        grid=(M // tm, n32 + n16),
        in_specs=[
            pl.BlockSpec((tm, D), lambda i, j: (i, 0)),
            pl.BlockSpec((None, 2, D), lambda i, j: (layer, 0, 0)),
            pl.BlockSpec((None, D, tn), lambda i, j: (layer, 0, j)),
        ],
        out_specs=[pl.BlockSpec((tm, tn), lambda i, j: (i, jnp.minimum(j, n32 - 1))),
                   pl.BlockSpec((tm, tn), lambda i, j: (i, jnp.maximum(j - n32, 0)))],
        out_shape=[jax.ShapeDtypeStruct((M, cols32), F32), jax.ShapeDtypeStruct((M, cols16), BF16)],
        scratch_shapes=[pltpu.VMEM((tm, D), BF16)],
        compiler_params=pltpu.CompilerParams(
            dimension_semantics=("parallel", "arbitrary"), vmem_limit_bytes=VMEM_LIMIT),
        name="proj",
    )(h, norm, w_in)


def _hgrn_kernel(q_ref, f_ref, i_ref, g_ref, lb_ref, ng_ref, o_ref, st_ref, kv_ref, sall_ref):
    tt = q_ref.shape[1]
    sub = A_SUB
    nb = tt // sub
    per = LANES // sub
    shape3 = (nb, sub, LANES)

    @pl.when(pl.program_id(2) == 0)
    def _():
        st_ref[...] = jnp.zeros_like(st_ref)

    lb = lb_ref[0]
    f = lb + (1.0 - lb) * _sigmoid(f_ref[0])
    k = 1.0 - f
    gl = jnp.log2(f)
    r = lax.broadcasted_iota(jnp.int32, (LANES, LANES), 0)
    c = lax.broadcasted_iota(jnp.int32, (LANES, LANES), 1)
    tril = ((r // sub == c // sub) & (c <= r)).astype(BF16)
    cums = []
    for a in range(tt // LANES):
        hi, lo = _split_bf16(gl[a * LANES:(a + 1) * LANES])
        cums.append(_dot(tril, hi) + _dot(tril, lo))
    b = jnp.concatenate(cums, axis=0)
    q = q_ref[0]
    v = i_ref[0]
    b3, q3, v3 = (x.reshape(shape3) for x in (b, q, v))
    bk3 = b3 - jnp.log2(k).reshape(shape3)

    ones = jnp.ones((LANES, LANES), BF16)
    rows = lax.broadcasted_iota(jnp.int32, (1, sub, LANES), 1)
    o3 = jnp.zeros(shape3, F32)
    for s in range(sub):
        w = jnp.where(rows >= s, q3 * jnp.exp2(b3 - bk3[:, s:s + 1, :]), 0.0).astype(BF16)
        o3 = o3 + _dot(w.reshape(tt, LANES), ones).reshape(shape3) * v3[:, s:s + 1, :]

    b_last = b3[:, sub - 1:sub, :]
    kd = jnp.exp2(b_last - bk3).reshape(tt, LANES)
    dec = jnp.exp2(b_last)
    for a in range(tt // LANES):
        v_t = v[a * LANES:(a + 1) * LANES].T.astype(BF16)
        kd_a = kd[a * LANES:(a + 1) * LANES]
        rhs = jnp.concatenate([jnp.where(r // sub == j, kd_a, 0.0).astype(BF16) for j in range(per)], axis=1)
        kv = _dot(v_t, rhs)
        for j in range(per):
            kv_ref[a * per + j] = kv[:, j * LANES:(j + 1) * LANES]

    st = st_ref[...]
    for n in range(nb):
        sall_ref[n] = st.astype(BF16)
        st = dec[n] * st + kv_ref[n]
    st_ref[...] = st

    qe = (q * jnp.exp2(b)).astype(BF16)
    inter = []
    for a in range(tt // LANES):
        stack = sall_ref[a * per:(a + 1) * per].reshape(per * LANES, LANES)
        p = _dot_nt(qe[a * LANES:(a + 1) * LANES], stack)
        inter += [p[j * sub:(j + 1) * sub, j * LANES:(j + 1) * LANES] for j in range(per)]
    o = o3.reshape(tt, LANES) + jnp.concatenate(inter, axis=0)

    g = g_ref[0]
    o_ref[0] = (_rms(o, ng_ref[0]) * (g * _sigmoid(g))).astype(o_ref.dtype)


def _hgrn(pa, lb, ng, tt):
    Bn, T, _ = pa.shape
    H = A_HEADS

    def col(off):
        return pl.BlockSpec((1, tt, A_DK), lambda b, h, t: (b, t, off * H + h))

    return pl.pallas_call(
        _hgrn_kernel,
        grid=(Bn, H, T // tt),
        in_specs=[col(0), col(1), col(2), col(3),
                  pl.BlockSpec((1, 1, A_DK), lambda b, h, t: (h, 0, 0)),
                  pl.BlockSpec((1, 1, A_DK), lambda b, h, t: (0, 0, 0))],
        out_specs=pl.BlockSpec((1, tt, A_DK), lambda b, h, t: (b, t, h)),
        out_shape=jax.ShapeDtypeStruct((Bn, T, H * A_DK), BF16),
        scratch_shapes=[pltpu.VMEM((A_DK, A_DK), F32), pltpu.VMEM((tt // A_SUB, A_DK, A_DK), F32),
                        pltpu.VMEM((tt // A_SUB, A_DK, A_DK), BF16)],
        compiler_params=pltpu.CompilerParams(
            dimension_semantics=("parallel", "parallel", "arbitrary"), vmem_limit_bytes=VMEM_LIMIT),
        name="hgrn2",
    )(pa, pa, pa, pa, lb, ng)


def _sb_kernel(q_ref, k_ref, v_ref, o_ref, suf_ref, acc_ref, run_ref):
    tq = q_ref.shape[1]
    i = pl.program_id(2)
    in_h0 = lax.broadcasted_iota(jnp.int32, (tq, LANES), 1) < HEAD_DIM
    r = lax.broadcasted_iota(jnp.int32, (tq, tq), 0)
    c = lax.broadcasted_iota(jnp.int32, (tq, tq), 1)
    suf_ref[...] = (r >= c).astype(BF16)
    strict = jnp.concatenate([c < r, c < r], axis=0)

    def lanes(n):
        return slice(n * LANES, (n + 1) * LANES)

    q2s = []
    for n in range(SB_CHAINS):
        q = q_ref[0, :, lanes(n)]
        zero = jnp.zeros_like(q)
        q2s.append(jnp.concatenate([jnp.where(in_h0, q, zero), jnp.where(in_h0, zero, q)], axis=0)
                   * (HEAD_DIM ** -0.5))

    def tile(q2, kb, vb, run, diag):
        z = _dot_nt(q2, kb)
        sp = jnp.maximum(z, 0.0) + jnp.log(1.0 + jnp.exp(-jnp.abs(z)))
        if diag:
            sp = jnp.where(strict, sp, 0.0)
        hi, lo = _split_bf16(sp)
        suffix = suf_ref[...]
        cs = _dot(hi, suffix) + _dot(lo, suffix) + run
        a = jnp.exp(z - cs)
        if diag:
            a = jnp.where(strict, a, 0.0)
        return _dot(a.astype(BF16), vb), cs[:, 0:1]

    def step(rows, first):
        lowest = None
        for n in range(SB_CHAINS):
            run = jnp.zeros((2 * tq, 1), F32) if first else run_ref[n]
            pv, run = tile(q2s[n], k_ref[0, rows, lanes(n)], v_ref[0, rows, lanes(n)], run, first)
            acc_ref[n] = pv if first else acc_ref[n] + pv
            run_ref[n] = run
            low = jnp.min(run)
            lowest = low if lowest is None else jnp.minimum(lowest, low)
        return lowest

    lowest = step(pl.ds(pl.multiple_of(i * tq, tq), tq), True)

    def cond(carry):
        s, lowest = carry
        return (s < i) & (lowest <= SB_EXIT)

    def body(carry):
        s, _ = carry
        return s + 1, step(pl.ds(pl.multiple_of((i - 1 - s) * tq, tq), tq), False)

    lax.while_loop(cond, body, (jnp.int32(0), lowest))
    for n in range(SB_CHAINS):
        acc = acc_ref[n]
        o_ref[0, :, lanes(n)] = jnp.where(in_h0, acc[:tq], acc[tq:]).astype(o_ref.dtype)


def _stick_breaking(pbd, tq):
    Bn, T, _ = pbd.shape
    ngrp = B_HEADS // 2 // SB_CHAINS
    w = SB_CHAINS * LANES
    return pl.pallas_call(
        _sb_kernel,
        grid=(Bn, ngrp, T // tq),
        in_specs=[pl.BlockSpec((1, tq, w), lambda b, p, i: (b, i, _BQ // SB_CHAINS + p)),
                  pl.BlockSpec((1, T, w), lambda b, p, i: (b, 0, _BK // SB_CHAINS + p)),
                  pl.BlockSpec((1, T, w), lambda b, p, i: (b, 0, _BV // SB_CHAINS + p))],
        out_specs=pl.BlockSpec((1, tq, w), lambda b, p, i: (b, i, p)),
        out_shape=jax.ShapeDtypeStruct((Bn, T, B_HEADS // 2 * LANES), BF16),
        scratch_shapes=[pltpu.VMEM((tq, tq), BF16), pltpu.VMEM((SB_CHAINS, 2 * tq, LANES), F32),
                        pltpu.VMEM((SB_CHAINS, 2 * tq, 1), F32)],
        compiler_params=pltpu.CompilerParams(
            dimension_semantics=("parallel", "parallel", "arbitrary"), vmem_limit_bytes=VMEM_LIMIT),
        name="stick_breaking",
    )(pbd, pbd, pbd)


def _win_kernel(q_ref, k_ref, kp_ref, v_ref, vp_ref, bias_ref, sink_ref, *rest, dil, nsub, nq, n_mix):
    mix_refs, out_refs = rest[:2 * n_mix], rest[2 * n_mix + (1 if n_mix else 0):]
    blk = WIN_BLOCK
    span = blk * dil
    nrow = 2 * nq * blk
    i = pl.program_id(2)
    lane = lax.broadcasted_iota(jnp.int32, (blk, LANES), 1)
    in_h0 = lane < HEAD_DIM
    prev_col = lax.broadcasted_iota(jnp.int32, (nrow, 2 * blk), 1) < blk
    ones = jnp.ones((2 * blk, LANES), BF16)
    sinks = jnp.concatenate([jnp.broadcast_to(sink_ref[hh:hh + 1, 0:1], (blk, 1)) for hh in range(2 * nq)],
                            axis=0)
    bias = bias_ref[...]

    def rows_of(start):
        return pl.ds(start, blk, stride=dil) if dil > 1 else pl.ds(start, blk)

    def residue_class(s, c):
        cur = rows_of(s * span + c)
        q = q_ref[0, cur, :].astype(BF16)
        if s == 0:
            kp, vp = kp_ref[0, rows_of(c), :], vp_ref[0, rows_of(c), :]
        else:
            prev = rows_of((s - 1) * span + c)
            kp, vp = k_ref[0, prev, :], v_ref[0, prev, :]
        k2 = jnp.concatenate([kp.astype(BF16), k_ref[0, cur, :].astype(BF16)], axis=0)
        v2 = jnp.concatenate([vp.astype(BF16), v_ref[0, cur, :].astype(BF16)], axis=0)
        qs = [q[:, p * LANES:(p + 1) * LANES] for p in range(nq)]
        zero = jnp.zeros_like(qs[0])
        q2 = jnp.concatenate([jnp.where(in_h0, x, zero) for x in qs]
                             + [jnp.where(in_h0, zero, x) for x in qs], axis=0)
        logits = _dot_nt(q2 * (HEAD_DIM ** -0.5), k2) + bias
        if s == 0:
            logits = jnp.where(jnp.logical_and(i == 0, prev_col), -jnp.inf, logits)
        m = jnp.max(jnp.maximum(logits[:, :blk], logits[:, blk:]), axis=-1, keepdims=True)
        m = jnp.maximum(m, sinks)
        p = jnp.exp(logits - m).astype(BF16)
        if nq > 1:
            pv = _dot(p, jnp.concatenate([v2, ones], axis=1))
            num, den = pv[:, :LANES], pv[:, LANES:]
        else:
            num, den = _dot(p, v2), _dot(p, ones)
        den = den + jnp.exp(sinks - m)
        o2 = num / den

        def lane_blocks(x):
            return jnp.concatenate([jnp.where(in_h0, x[b * blk:(b + 1) * blk], x[(nq + b) * blk:(nq + b + 1) * blk])
                                    for b in range(nq)], axis=1)

        out_refs[0][0, cur, :] = lane_blocks(o2).astype(out_refs[0].dtype)
        if len(out_refs) > 1:
            out_refs[1][0, cur, :] = lane_blocks(m + jnp.log(den))

    for s in range(nsub):
        if dil <= WIN_UNROLL:
            for c in range(dil):
                residue_class(s, c)
        else:
            def body(c, carry, s=s):
                residue_class(s, c)
                return carry
            lax.fori_loop(0, dil, body, 0, unroll=WIN_UNROLL)

    if n_mix:
        outs = [r[0] for r in mix_refs[0::2]] + [out_refs[0][0]]
        lses = [r[0] for r in mix_refs[1::2]] + [out_refs[1][0]]
        top = functools.reduce(jnp.maximum, lses)
        ws = [jnp.exp(l - top) for l in lses]
        y_ref = rest[2 * n_mix]
        y_ref[0] = (sum(w * o for w, o in zip(ws, outs)) / sum(ws)).astype(y_ref.dtype)


def _window_attn(slab, bias, sinks, dil, qcol, kcol, vcol, ngrp, nq, with_lse, out_dtype, mix=()):
    Bn, T, _ = slab.shape
    span = WIN_BLOCK * dil
    nsub = max(1, WIN_UNROLL // dil)
    tb = span * nsub
    assert qcol % nq == 0

    def kvmap(col, prev):
        def f(b, g, i):
            return (b, jnp.maximum(i * nsub - 1, 0) if prev else i, col + g)
        return f

    qblk = (1, tb, nq * LANES)
    blk = (1, tb, LANES)
    pblk = (1, span, LANES)
    out_spec = pl.BlockSpec(qblk, lambda b, g, i: (b, i, g))
    out_sds = jax.ShapeDtypeStruct((Bn, T, ngrp * nq * LANES), out_dtype)
    lse_sds = jax.ShapeDtypeStruct((Bn, T, ngrp * nq * LANES), F32)
    if mix:
        out_specs, out_shape = [out_spec], [out_sds]
        scratch = [pltpu.VMEM(qblk, F32), pltpu.VMEM(qblk, F32)]
    else:
        out_specs = [out_spec, out_spec] if with_lse else [out_spec]
        out_shape = [out_sds, lse_sds] if with_lse else [out_sds]
        scratch = []
    return pl.pallas_call(
        functools.partial(_win_kernel, dil=dil, nsub=nsub, nq=nq, n_mix=len(mix) // 2),
        grid=(Bn, ngrp, T // tb),
        in_specs=[pl.BlockSpec(qblk, lambda b, g, i: (b, i, qcol // nq + g)),
                  pl.BlockSpec(blk, kvmap(kcol, False)), pl.BlockSpec(pblk, kvmap(kcol, True)),
                  pl.BlockSpec(blk, kvmap(vcol, False)), pl.BlockSpec(pblk, kvmap(vcol, True)),
                  pl.BlockSpec((None, 2 * nq * WIN_BLOCK, 2 * WIN_BLOCK), lambda b, g, i: (g, 0, 0)),
                  pl.BlockSpec((None, 2 * nq, LANES), lambda b, g, i: (g, 0, 0))] + [out_spec] * len(mix),
        out_specs=out_specs,
        out_shape=out_shape,
        scratch_shapes=scratch,
        compiler_params=pltpu.CompilerParams(
            dimension_semantics=("parallel", "parallel", "arbitrary"), vmem_limit_bytes=VMEM_LIMIT),
        name="window_attn",
    )(slab, slab, slab, slab, slab, bias, sinks, *mix)


def _merge_kernel(x_ref, g_ref, ya_ref, yb_ref, yc_ref, yd_ref,
                  ga_ref, gb_ref, gc_ref, gd_ref, wa_ref, wb_ref, wc_ref, wd_ref, wo_ref,
                  o_ref, u_ref):
    j = pl.program_id(1)

    @pl.when(j == 0)
    def _():
        u_ref[...] = _rms(x_ref[...], g_ref[0:1, :]).astype(BF16)
        o_ref[...] = jnp.zeros_like(o_ref)

    u = u_ref[...]
    merged = None
    for y_ref, gate_ref, w_ref in ((ya_ref, ga_ref, wa_ref), (yb_ref, gb_ref, wb_ref),
                                   (yc_ref, gc_ref, wc_ref), (yd_ref, gd_ref, wd_ref)):
        term = _sigmoid(_dot(u, gate_ref[...])) * _dot(y_ref[...], w_ref[...])
        merged = term if merged is None else merged + term
    o_ref[...] += _dot(merged.astype(BF16), wo_ref[...])

    @pl.when(j == pl.num_programs(1) - 1)
    def _():
        o_ref[...] = x_ref[...] + _rms(o_ref[...], g_ref[1:2, :])


def _merge(h, norm, ys, w_in, gate_col0, w_branch, w_out, layer, tm, tn):
    M, D = h.shape
    gcb = gate_col0 // tn
    dcb = D // tn

    def yspec(w):
        return pl.BlockSpec((tm, w), lambda i, j: (i, 0))

    def gspec(br):
        return pl.BlockSpec((None, D, tn), lambda i, j: (layer, 0, gcb + br * dcb + j))

    def wspec(rows, rb):
        return pl.BlockSpec((None, rows, tn), lambda i, j: (layer, rb, j))

    wa, wb, wc, wd = (y.shape[1] for y in ys)
    return pl.pallas_call(
        _merge_kernel,
        grid=(M // tm, dcb),
        in_specs=[pl.BlockSpec((tm, D), lambda i, j: (i, 0)),
                  pl.BlockSpec((None, 2, D), lambda i, j: (layer, 0, 0)),
                  yspec(wa), yspec(wb), yspec(wc), yspec(wd),
                  gspec(0), gspec(1), gspec(2), gspec(3),
                  wspec(wa, 0), wspec(wb, wa // wb), wspec(wc, (wa + wb + wd) // wc),
                  wspec(wd, (wa + wb) // wd),
                  pl.BlockSpec((None, tn, D), lambda i, j: (layer, j, 0))],
        out_specs=pl.BlockSpec((tm, D), lambda i, j: (i, 0)),
        out_shape=jax.ShapeDtypeStruct((M, D), F32),
        scratch_shapes=[pltpu.VMEM((tm, D), BF16)],
        compiler_params=pltpu.CompilerParams(
            dimension_semantics=("parallel", "arbitrary"), vmem_limit_bytes=VMEM_LIMIT),
        name="gated_merge",
    )(h, norm, *ys, w_in, w_in, w_in, w_in, w_branch, w_branch, w_branch, w_branch, w_out)


def _rel_bucket(dist):
    max_exact = REL_BUCKETS // 2
    d = jnp.maximum(dist, 1).astype(F32)
    large = max_exact + (jnp.log(d / max_exact) / math.log(REL_MAX_DIST / max_exact)
                         * (REL_BUCKETS - max_exact)).astype(jnp.int32)
    large = jnp.minimum(large, REL_BUCKETS - 1)
    return jnp.where(dist < max_exact, dist, large)


def _window_bias(rel_bias, heads, dilation, max_dist, rows_heads=2):
    row = jnp.arange(WIN_BLOCK)[:, None]
    col = jnp.arange(2 * WIN_BLOCK)[None, :]
    dist = row + WIN_BLOCK - col
    valid = (dist >= 0) & (dist <= max_dist)
    assert list(heads) == list(range(heads[0], heads[-1] + 1))
    bucket = _rel_bucket(jnp.maximum(dist, 0) * dilation)
    per_head = rel_bias.astype(F32)[:, heads[0]:heads[-1] + 1]
    hit = bucket[None, None] == jnp.arange(REL_BUCKETS)[:, None, None, None]
    tab = jnp.sum(jnp.where(hit, per_head[:, :, None, None], 0.0), axis=0)
    tab = jnp.where(valid[None], tab, -jnp.inf)
    return tab.reshape(len(heads) // rows_heads, rows_heads * WIN_BLOCK, 2 * WIN_BLOCK)


def _trunk(x, ffn1_norm, ffn1_w13, ffn1_w2, mix_norm, w_in, hgrn_lb_logits, hgrn_out_norm,
           attn_sinks, w_branch, w_out, ffn2_norm, ffn2_w13, ffn2_w2, rel_bias, cfg):
    Bn, T, D = x.shape
    depth = w_in.shape[0]
    M = Bn * T
    a_w = A_HEADS * A_DK
    a_cols = 4 * a_w
    b_cols = 3 * B_HEADS * HEAD_DIM
    c_cols = 3 * C_HEADS * HEAD_DIM
    ac_cols = AC_BLOCKS * LANES
    bd_cols = BD_BLOCKS * LANES
    mix_in = ac_cols + bd_cols

    lb_sm = jax.nn.softmax(hgrn_lb_logits.astype(F32), axis=0)
    lb_all = (jnp.cumsum(lb_sm, axis=0) - lb_sm[0:1]).reshape(depth, A_HEADS, 1, A_DK)

    b0, c0 = a_cols, a_cols + b_cols
    dq0 = c0 + c_cols
    dk0 = dq0 + D_HEADS * HEAD_DIM
    c_w = C_HPG * HEAD_DIM

    perm = jnp.array([h * HEAD_DIM + d for h in D_HEAD_PERM for d in range(HEAD_DIM)])

    def c_part(part, g0, g1):
        base = c0 + part * C_HEADS * HEAD_DIM
        return w_in[:, :, base + g0 * c_w:base + g1 * c_w]

    n_grp = len(C_PATTERNS)
    w_in_b = jnp.concatenate(
        [w_in[:, :, :a_cols]] + [c_part(p, 1, n_grp) for p in range(3)]
        + [w_in[:, :, b0:c0], w_in[:, :, dq0:dk0][:, :, perm], w_in[:, :, dk0:mix_in]]
        + [c_part(p, 0, 1) for p in range(3)] + [w_in[:, :, mix_in:]], axis=-1).astype(BF16)
    d0 = 2 * a_w + c_w
    w_br_b = jnp.concatenate(
        [w_branch[:, :2 * a_w], w_branch[:, d0:][:, perm], w_branch[:, 2 * a_w:d0]], axis=1).astype(BF16)
    w_out_b = w_out.astype(BF16)
    f1_w13, f1_w2 = ffn1_w13.astype(BF16), ffn1_w2.astype(BF16)
    f2_w13, f2_w2 = ffn2_w13.astype(BF16), ffn2_w2.astype(BF16)

    c_bias = [_window_bias(rel_bias, list(range(g * C_HPG, (g + 1) * C_HPG)), dil, win // dil)
              for g, (win, dil) in enumerate(C_PATTERNS)]
    d_bias = _window_bias(rel_bias, [C_HEADS + h for h in range(D_HEADS)], 1, D_WINDOW - 1, D_HEADS)
    no_sink = jnp.full((C_HPG // 2, 2, LANES), -jnp.inf, F32)

    h = x.reshape(M, D)
    for l in range(depth):
        h = _ffn(h, ffn1_norm, f1_w13, f1_w2, l, cfg["ffn_tm"], cfg["ffn_tf"])

        pac, pbd = _proj(h, mix_norm, w_in_b, l, ac_cols, bd_cols, cfg["proj_tm"], cfg["proj_tn"])
        pac = pac.reshape(Bn, T, ac_cols)
        pbd = pbd.reshape(Bn, T, bd_cols)

        ya = _hgrn(pac, lb_all[l], hgrn_out_norm[l].reshape(1, 1, A_DK).astype(F32), cfg["hgrn_tt"])
        yb = _stick_breaking(pbd, cfg["sb_tq"])
        mix = ()
        for g, (win, dil) in enumerate(C_PATTERNS):
            if dil == 1:
                slab, cols = pbd, (_C0Q, _C0K, _C0V)
            else:
                slab, cols = pac, tuple(c + 2 * (g - 1) for c in (_CQ, _CK, _CV))
            if g < len(C_PATTERNS) - 1:
                mix += tuple(_window_attn(slab, c_bias[g], no_sink, dil, *cols, C_HPG // 2, 1, True, F32))
            else:
                (yc,) = _window_attn(slab, c_bias[g], no_sink, dil, *cols, C_HPG // 2, 1, True, BF16, mix)
        sinks = jnp.broadcast_to(attn_sinks[l].astype(F32).reshape(1, D_HEADS, 1), (1, D_HEADS, LANES))
        (yd,) = _window_attn(pbd, d_bias, sinks, 1, _DQ, _DK, _DV, 1, D_HEADS // 2, False, BF16)

        ys = (ya.reshape(M, a_w), yb.reshape(M, a_w), yc.reshape(M, c_w), yd.reshape(M, a_w))
        h = _merge(h, mix_norm, ys, w_in_b, mix_in, w_br_b, w_out_b, l, cfg["merge_tm"], cfg["merge_tn"])

        h = _ffn(h, ffn2_norm, f2_w13, f2_w2, l, cfg["ffn_tm"], cfg["ffn_tf"])
    return h.reshape(Bn, T, D)


_CFG = dict(ffn_tm=1024, ffn_tf=256, proj_tm=1024, proj_tn=512, hgrn_tt=512, sb_tq=256,
            merge_tm=512, merge_tn=512)


def kernel(x, ffn1_norm, ffn1_w13, ffn1_w2, mix_norm, w_in, hgrn_lb_logits, hgrn_out_norm, attn_sinks,
           w_branch, w_out, ffn2_norm, ffn2_w13, ffn2_w2, rel_bias):
    return _trunk(x, ffn1_norm, ffn1_w13, ffn1_w2, mix_norm, w_in, hgrn_lb_logits, hgrn_out_norm,
                  attn_sinks, w_branch, w_out, ffn2_norm, ffn2_w13, ffn2_w2, rel_bias, _CFG)
```

```python
import functools
import math

import jax
import jax.numpy as jnp
from jax import lax
from jax.experimental import pallas as pl
from jax.experimental.pallas import tpu as pltpu

F32 = jnp.float32
BF16 = jnp.bfloat16

EPS = 1e-6
HEAD_DIM = 64
LANES = 128
A_HEADS = 4
A_DK = 128
B_HEADS = 8
C_PATTERNS = ((128, 1), (512, 4), (2048, 16))
C_HPG = 4
C_HEADS = len(C_PATTERNS) * C_HPG
D_HEADS = 8
D_WINDOW = 128
WIN_BLOCK = 128
REL_BUCKETS = 32
REL_MAX_DIST = 2048
A_SUB = 16
VMEM_LIMIT = 60 * 1024 * 1024

_CQ, _CK, _CV = 16, 20, 24
AC_BLOCKS = 28
_BQ, _BK, _BV = 0, 4, 8
_DQ, _DK, _DV = 12, 16, 17
_C0Q, _C0K, _C0V = 18, 20, 22
BD_BLOCKS = 24
WIN_UNROLL = 4
D_HEAD_PERM = (0, 4, 1, 5, 2, 6, 3, 7)
SB_CHAINS = 4
SB_EXIT = 100.0


def _rms(x, g):
    ms = jnp.mean(x * x, axis=-1, keepdims=True)
    return x * lax.rsqrt(ms + EPS) * g


def _sigmoid(x):
    return 1.0 / (1.0 + jnp.exp(-x))


def _dot(a, b):
    return jnp.dot(a, b, preferred_element_type=F32)


def _dot_nt(a, b):
    return lax.dot_general(a, b, (((1,), (1,)), ((), ())), preferred_element_type=F32)


def _split_bf16(x):
    hi = x.astype(BF16)
    lo = (x - hi.astype(F32)).astype(BF16)
    return hi, lo


def _ffn_kernel(x_ref, g_ref, w1_ref, w3_ref, w2_ref, o_ref, xn_ref):
    j = pl.program_id(1)

    @pl.when(j == 0)
    def _():
        xn_ref[...] = _rms(x_ref[...], g_ref[0:1, :]).astype(BF16)
        o_ref[...] = jnp.zeros_like(o_ref)

    xn = xn_ref[...]
    gate = _dot(xn, w1_ref[...])
    up = _dot(xn, w3_ref[...])
    act = (gate * _sigmoid(gate) * up).astype(BF16)
    o_ref[...] += _dot(act, w2_ref[...])

    @pl.when(j == pl.num_programs(1) - 1)
    def _():
        o_ref[...] = x_ref[...] + 0.5 * _rms(o_ref[...], g_ref[1:2, :])


def _ffn(h, norm, w13, w2, layer, tm, tf):
    M, D = h.shape
    F = w2.shape[1]
    nf = F // tf
    return pl.pallas_call(
        _ffn_kernel,
        grid=(M // tm, nf),
        in_specs=[
            pl.BlockSpec((tm, D), lambda i, j: (i, 0)),
            pl.BlockSpec((None, 2, D), lambda i, j: (layer, 0, 0)),
            pl.BlockSpec((None, D, tf), lambda i, j: (layer, 0, j)),
            pl.BlockSpec((None, D, tf), lambda i, j: (layer, 0, j + nf)),
            pl.BlockSpec((None, tf, D), lambda i, j: (layer, j, 0)),
        ],
        out_specs=pl.BlockSpec((tm, D), lambda i, j: (i, 0)),
        out_shape=jax.ShapeDtypeStruct((M, D), F32),
        scratch_shapes=[pltpu.VMEM((tm, D), BF16)],
        compiler_params=pltpu.CompilerParams(
            dimension_semantics=("parallel", "arbitrary"), vmem_limit_bytes=VMEM_LIMIT),
        name="ffn",
    )(h, norm, w13, w13, w2)


def _proj_kernel(x_ref, g_ref, w_ref, o32_ref, o16_ref, xn_ref, *, n32):
    j = pl.program_id(1)

    @pl.when(j == 0)
    def _():
        xn_ref[...] = _rms(x_ref[...], g_ref[0:1, :]).astype(BF16)

    @pl.when(j < n32)
    def _():
        o32_ref[...] = _dot(xn_ref[...], w_ref[...])

    @pl.when(j >= n32)
    def _():
        o16_ref[...] = _dot(xn_ref[...], w_ref[...]).astype(BF16)


def _proj(h, norm, w_in, layer, cols32, cols16, tm, tn):
    M, D = h.shape
    n32, n16 = cols32 // tn, cols16 // tn
    return pl.pallas_call(
        functools.partial(_proj_kernel, n32=n32),
        grid=(M // tm, n32 + n16),
        in_specs=[
            pl.BlockSpec((tm, D), lambda i, j: (i, 0)),
            pl.BlockSpec((None, 2, D), lambda i, j: (layer, 0, 0)),
            pl.BlockSpec((None, D, tn), lambda i, j: (layer, 0, j)),
        ],
        out_specs=[pl.BlockSpec((tm, tn), lambda i, j: (i, jnp.minimum(j, n32 - 1))),
                   pl.BlockSpec((tm, tn), lambda i, j: (i, jnp.maximum(j - n32, 0)))],
        out_shape=[jax.ShapeDtypeStruct((M, cols32), F32), jax.ShapeDtypeStruct((M, cols16), BF16)],
        scratch_shapes=[pltpu.VMEM((tm, D), BF16)],
        compiler_params=pltpu.CompilerParams(
            dimension_semantics=("parallel", "arbitrary"), vmem_limit_bytes=VMEM_LIMIT),
        name="proj",
    )(h, norm, w_in)


def _hgrn_kernel(q_ref, f_ref, i_ref, g_ref, lb_ref, ng_ref, o_ref, st_ref, kv_ref, sall_ref):
    tt = q_ref.shape[1]
    sub = A_SUB
    nb = tt // sub
    per = LANES // sub
    shape3 = (nb, sub, LANES)

    @pl.when(pl.program_id(2) == 0)
    def _():
        st_ref[...] = jnp.zeros_like(st_ref)

    lb = lb_ref[0]
    f = lb + (1.0 - lb) * _sigmoid(f_ref[0])
    k = 1.0 - f
    gl = jnp.log2(f)
    r = lax.broadcasted_iota(jnp.int32, (LANES, LANES), 0)
    c = lax.broadcasted_iota(jnp.int32, (LANES, LANES), 1)
    tril = ((r // sub == c // sub) & (c <= r)).astype(BF16)
    cums = []
    for a in range(tt // LANES):
        hi, lo = _split_bf16(gl[a * LANES:(a + 1) * LANES])
        cums.append(_dot(tril, hi) + _dot(tril, lo))
    b = jnp.concatenate(cums, axis=0)
    q = q_ref[0]
    v = i_ref[0]
    b3, q3, v3 = (x.reshape(shape3) for x in (b, q, v))
    bk3 = b3 - jnp.log2(k).reshape(shape3)

    ones = jnp.ones((LANES, LANES), BF16)
    half = sub // 2
    rows = lax.broadcasted_iota(jnp.int32, (1, sub, LANES), 1)
    acc = [jnp.zeros(shape3, F32), jnp.zeros((nb, half, LANES), F32)]
    for s in range(sub):
        up = s >= half
        lo_row = half if up else 0
        qs, bs, rs = (x[:, lo_row:, :] for x in (q3, b3, rows))
        w = jnp.where(rs >= s, qs * jnp.exp2(bs - bk3[:, s:s + 1, :]), 0.0)
        w = w.reshape(-1, LANES).astype(BF16)
        acc[up] = acc[up] + _dot(w, ones).reshape(qs.shape) * v3[:, s:s + 1, :]
    o3 = acc[0] + jnp.concatenate([jnp.zeros_like(acc[1]), acc[1]], axis=1)

    b_last = b3[:, sub - 1:sub, :]
    kd = jnp.exp2(b_last - bk3).reshape(tt, LANES)
    dec = jnp.exp2(b_last)
    for a in range(tt // LANES):
        v_t = v[a * LANES:(a + 1) * LANES].T.astype(BF16)
        kd_a = kd[a * LANES:(a + 1) * LANES]
        rhs = jnp.concatenate([jnp.where(r // sub == j, kd_a, 0.0).astype(BF16) for j in range(per)], axis=1)
        kv = _dot(v_t, rhs)
        for j in range(per):
            kv_ref[a * per + j] = kv[:, j * LANES:(j + 1) * LANES]

    st = st_ref[...]
    for n in range(nb):
        sall_ref[n] = st.astype(BF16)
        st = dec[n] * st + kv_ref[n]
    st_ref[...] = st

    qe = (q * jnp.exp2(b)).astype(BF16)
    inter = []
    for a in range(tt // LANES):
        stack = sall_ref[a * per:(a + 1) * per].reshape(per * LANES, LANES)
        p = _dot_nt(qe[a * LANES:(a + 1) * LANES], stack)
        inter += [p[j * sub:(j + 1) * sub, j * LANES:(j + 1) * LANES] for j in range(per)]
    o = o3.reshape(tt, LANES) + jnp.concatenate(inter, axis=0)

    g = g_ref[0]
    o_ref[0] = (_rms(o, ng_ref[0]) * (g * _sigmoid(g))).astype(o_ref.dtype)


def _hgrn(pa, lb, ng, tt):
    Bn, T, _ = pa.shape
    H = A_HEADS

    def col(off):
        return pl.BlockSpec((1, tt, A_DK), lambda b, h, t: (b, t, off * H + h))

    return pl.pallas_call(
        _hgrn_kernel,
        grid=(Bn, H, T // tt),
        in_specs=[col(0), col(1), col(2), col(3),
                  pl.BlockSpec((1, 1, A_DK), lambda b, h, t: (h, 0, 0)),
                  pl.BlockSpec((1, 1, A_DK), lambda b, h, t: (0, 0, 0))],
        out_specs=pl.BlockSpec((1, tt, A_DK), lambda b, h, t: (b, t, h)),
        out_shape=jax.ShapeDtypeStruct((Bn, T, H * A_DK), BF16),
        scratch_shapes=[pltpu.VMEM((A_DK, A_DK), F32), pltpu.VMEM((tt // A_SUB, A_DK, A_DK), F32),
                        pltpu.VMEM((tt // A_SUB, A_DK, A_DK), BF16)],
        compiler_params=pltpu.CompilerParams(
            dimension_semantics=("parallel", "parallel", "arbitrary"), vmem_limit_bytes=VMEM_LIMIT),
        name="hgrn2",
    )(pa, pa, pa, pa, lb, ng)


def _sb_kernel(q_ref, k_ref, v_ref, o_ref, suf_ref, acc_ref, run_ref):
    tq = q_ref.shape[1]
    i = pl.program_id(2)
    in_h0 = lax.broadcasted_iota(jnp.int32, (tq, LANES), 1) < HEAD_DIM
    r = lax.broadcasted_iota(jnp.int32, (tq, tq), 0)
    c = lax.broadcasted_iota(jnp.int32, (tq, tq), 1)
    suf_ref[...] = (r >= c).astype(BF16)
    strict = jnp.concatenate([c < r, c < r], axis=0)

    def lanes(n):
        return slice(n * LANES, (n + 1) * LANES)

    q2s = []
    for n in range(SB_CHAINS):
        q = q_ref[0, :, lanes(n)]
        zero = jnp.zeros_like(q)
        q2s.append(jnp.concatenate([jnp.where(in_h0, q, zero), jnp.where(in_h0, zero, q)], axis=0)
                   * (HEAD_DIM ** -0.5))

    def tile(q2, kb, vb, run, diag):
        z = _dot_nt(q2, kb)
        sp = jnp.maximum(z, 0.0) + jnp.log(1.0 + jnp.exp(-jnp.abs(z)))
        if diag:
            sp = jnp.where(strict, sp, 0.0)
        hi, lo = _split_bf16(sp)
        suffix = suf_ref[...]
        cs = _dot(hi, suffix) + _dot(lo, suffix) + run
        a = jnp.exp(z - cs)
        if diag:
            a = jnp.where(strict, a, 0.0)
        return _dot(a.astype(BF16), vb), cs[:, 0:1]

    def step(rows, first):
        lowest = None
        for n in range(SB_CHAINS):
            run = jnp.zeros((2 * tq, 1), F32) if first else run_ref[n]
            pv, run = tile(q2s[n], k_ref[0, rows, lanes(n)], v_ref[0, rows, lanes(n)], run, first)
            acc_ref[n] = pv if first else acc_ref[n] + pv
            run_ref[n] = run
            low = jnp.min(run)
            lowest = low if lowest is None else jnp.minimum(lowest, low)
        return lowest

    lowest = step(pl.ds(pl.multiple_of(i * tq, tq), tq), True)

    def cond(carry):
        s, lowest = carry
        return (s < i) & (lowest <= SB_EXIT)

    def body(carry):
        s, _ = carry
        return s + 1, step(pl.ds(pl.multiple_of((i - 1 - s) * tq, tq), tq), False)

    lax.while_loop(cond, body, (jnp.int32(0), lowest))
    for n in range(SB_CHAINS):
        acc = acc_ref[n]
        o_ref[0, :, lanes(n)] = jnp.where(in_h0, acc[:tq], acc[tq:]).astype(o_ref.dtype)


def _stick_breaking(pbd, tq):
    Bn, T, _ = pbd.shape
    ngrp = B_HEADS // 2 // SB_CHAINS
    w = SB_CHAINS * LANES
    return pl.pallas_call(
        _sb_kernel,
        grid=(Bn, ngrp, T // tq),
        in_specs=[pl.BlockSpec((1, tq, w), lambda b, p, i: (b, i, _BQ // SB_CHAINS + p)),
                  pl.BlockSpec((1, T, w), lambda b, p, i: (b, 0, _BK // SB_CHAINS + p)),
                  pl.BlockSpec((1, T, w), lambda b, p, i: (b, 0, _BV // SB_CHAINS + p))],
        out_specs=pl.BlockSpec((1, tq, w), lambda b, p, i: (b, i, p)),
        out_shape=jax.ShapeDtypeStruct((Bn, T, B_HEADS // 2 * LANES), BF16),
        scratch_shapes=[pltpu.VMEM((tq, tq), BF16), pltpu.VMEM((SB_CHAINS, 2 * tq, LANES), F32),
                        pltpu.VMEM((SB_CHAINS, 2 * tq, 1), F32)],
        compiler_params=pltpu.CompilerParams(
            dimension_semantics=("parallel", "parallel", "arbitrary"), vmem_limit_bytes=VMEM_LIMIT),
        name="stick_breaking",
    )(pbd, pbd, pbd)


def _win_kernel(q_ref, k_ref, kp_ref, v_ref, vp_ref, bias_ref, sink_ref, *rest, dil, nsub, nq, n_mix):
    mix_refs, out_refs = rest[:2 * n_mix], rest[2 * n_mix + (1 if n_mix else 0):]
    blk = WIN_BLOCK
    span = blk * dil
    nrow = 2 * nq * blk
    i = pl.program_id(2)
    lane = lax.broadcasted_iota(jnp.int32, (blk, LANES), 1)
    in_h0 = lane < HEAD_DIM
    prev_col = lax.broadcasted_iota(jnp.int32, (nrow, 2 * blk), 1) < blk
    ones = jnp.ones((2 * blk, LANES), BF16)
    sinks = jnp.concatenate([jnp.broadcast_to(sink_ref[hh:hh + 1, 0:1], (blk, 1)) for hh in range(2 * nq)],
                            axis=0)
    bias = bias_ref[...]

    def rows_of(start):
        return pl.ds(start, blk, stride=dil) if dil > 1 else pl.ds(start, blk)

    def residue_class(s, c):
        cur = rows_of(s * span + c)
        q = q_ref[0, cur, :].astype(BF16)
        if s == 0:
            kp, vp = kp_ref[0, rows_of(c), :], vp_ref[0, rows_of(c), :]
        else:
            prev = rows_of((s - 1) * span + c)
            kp, vp = k_ref[0, prev, :], v_ref[0, prev, :]
        k2 = jnp.concatenate([kp.astype(BF16), k_ref[0, cur, :].astype(BF16)], axis=0)
        v2 = jnp.concatenate([vp.astype(BF16), v_ref[0, cur, :].astype(BF16)], axis=0)
        qs = [q[:, p * LANES:(p + 1) * LANES] for p in range(nq)]
        zero = jnp.zeros_like(qs[0])
        q2 = jnp.concatenate([jnp.where(in_h0, x, zero) for x in qs]
                             + [jnp.where(in_h0, zero, x) for x in qs], axis=0)
        logits = _dot_nt(q2 * (HEAD_DIM ** -0.5), k2) + bias
        if s == 0:
            logits = jnp.where(jnp.logical_and(i == 0, prev_col), -jnp.inf, logits)
        m = jnp.max(jnp.maximum(logits[:, :blk], logits[:, blk:]), axis=-1, keepdims=True)
        m = jnp.maximum(m, sinks)
        p = jnp.exp(logits - m).astype(BF16)
        if nq > 1:
            pv = _dot(p, jnp.concatenate([v2, ones], axis=1))
            num, den = pv[:, :LANES], pv[:, LANES:]
        else:
            num, den = _dot(p, v2), _dot(p, ones)
        den = den + jnp.exp(sinks - m)
        o2 = num / den

        def lane_blocks(x):
            return jnp.concatenate([jnp.where(in_h0, x[b * blk:(b + 1) * blk], x[(nq + b) * blk:(nq + b + 1) * blk])
                                    for b in range(nq)], axis=1)

        out_refs[0][0, cur, :] = lane_blocks(o2).astype(out_refs[0].dtype)
        if len(out_refs) > 1:
            out_refs[1][0, cur, :] = lane_blocks(m + jnp.log(den))

    for s in range(nsub):
        if dil <= WIN_UNROLL:
            for c in range(dil):
                residue_class(s, c)
        else:
            def body(c, carry, s=s):
                residue_class(s, c)
                return carry
            lax.fori_loop(0, dil, body, 0, unroll=WIN_UNROLL)

    if n_mix:
        outs = [r[0] for r in mix_refs[0::2]] + [out_refs[0][0]]
        lses = [r[0] for r in mix_refs[1::2]] + [out_refs[1][0]]
        top = functools.reduce(jnp.maximum, lses)
        ws = [jnp.exp(l - top) for l in lses]
        y_ref = rest[2 * n_mix]
        y_ref[0] = (sum(w * o for w, o in zip(ws, outs)) / sum(ws)).astype(y_ref.dtype)


def _window_attn(slab, bias, sinks, dil, qcol, kcol, vcol, ngrp, nq, with_lse, out_dtype, mix=()):
    Bn, T, _ = slab.shape
    span = WIN_BLOCK * dil
    nsub = max(1, WIN_UNROLL // dil)
    tb = span * nsub
    assert qcol % nq == 0

    def kvmap(col, prev):
        def f(b, g, i):
            return (b, jnp.maximum(i * nsub - 1, 0) if prev else i, col + g)
        return f

    qblk = (1, tb, nq * LANES)
    blk = (1, tb, LANES)
    pblk = (1, span, LANES)
    out_spec = pl.BlockSpec(qblk, lambda b, g, i: (b, i, g))
    out_sds = jax.ShapeDtypeStruct((Bn, T, ngrp * nq * LANES), out_dtype)
    lse_sds = jax.ShapeDtypeStruct((Bn, T, ngrp * nq * LANES), F32)
    if mix:
        out_specs, out_shape = [out_spec], [out_sds]
        scratch = [pltpu.VMEM(qblk, F32), pltpu.VMEM(qblk, F32)]
    else:
        out_specs = [out_spec, out_spec] if with_lse else [out_spec]
        out_shape = [out_sds, lse_sds] if with_lse else [out_sds]
        scratch = []
    return pl.pallas_call(
        functools.partial(_win_kernel, dil=dil, nsub=nsub, nq=nq, n_mix=len(mix) // 2),
        grid=(Bn, ngrp, T // tb),
        in_specs=[pl.BlockSpec(qblk, lambda b, g, i: (b, i, qcol // nq + g)),
                  pl.BlockSpec(blk, kvmap(kcol, False)), pl.BlockSpec(pblk, kvmap(kcol, True)),
                  pl.BlockSpec(blk, kvmap(vcol, False)), pl.BlockSpec(pblk, kvmap(vcol, True)),
                  pl.BlockSpec((None, 2 * nq * WIN_BLOCK, 2 * WIN_BLOCK), lambda b, g, i: (g, 0, 0)),
                  pl.BlockSpec((None, 2 * nq, LANES), lambda b, g, i: (g, 0, 0))] + [out_spec] * len(mix),
        out_specs=out_specs,
        out_shape=out_shape,
        scratch_shapes=scratch,
        compiler_params=pltpu.CompilerParams(
            dimension_semantics=("parallel", "parallel", "arbitrary"), vmem_limit_bytes=VMEM_LIMIT),
        name="window_attn",
    )(slab, slab, slab, slab, slab, bias, sinks, *mix)


def _merge_kernel(x_ref, g_ref, ya_ref, yb_ref, yc_ref, yd_ref,
                  ga_ref, gb_ref, gc_ref, gd_ref, wa_ref, wb_ref, wc_ref, wd_ref, wo_ref,
                  o_ref, u_ref):
    j = pl.program_id(1)

    @pl.when(j == 0)
    def _():
        u_ref[...] = _rms(x_ref[...], g_ref[0:1, :]).astype(BF16)
        o_ref[...] = jnp.zeros_like(o_ref)

    u = u_ref[...]
    merged = None
    for y_ref, gate_ref, w_ref in ((ya_ref, ga_ref, wa_ref), (yb_ref, gb_ref, wb_ref),
                                   (yc_ref, gc_ref, wc_ref), (yd_ref, gd_ref, wd_ref)):
        term = _sigmoid(_dot(u, gate_ref[...])) * _dot(y_ref[...], w_ref[...])
        merged = term if merged is None else merged + term
    o_ref[...] += _dot(merged.astype(BF16), wo_ref[...])

    @pl.when(j == pl.num_programs(1) - 1)
    def _():
        o_ref[...] = x_ref[...] + _rms(o_ref[...], g_ref[1:2, :])


def _merge(h, norm, ys, w_in, gate_col0, w_branch, w_out, layer, tm, tn):
    M, D = h.shape
    gcb = gate_col0 // tn
    dcb = D // tn

    def yspec(w):
        return pl.BlockSpec((tm, w), lambda i, j: (i, 0))

    def gspec(br):
        return pl.BlockSpec((None, D, tn), lambda i, j: (layer, 0, gcb + br * dcb + j))

    def wspec(rows, rb):
        return pl.BlockSpec((None, rows, tn), lambda i, j: (layer, rb, j))

    wa, wb, wc, wd = (y.shape[1] for y in ys)
    return pl.pallas_call(
        _merge_kernel,
        grid=(M // tm, dcb),
        in_specs=[pl.BlockSpec((tm, D), lambda i, j: (i, 0)),
                  pl.BlockSpec((None, 2, D), lambda i, j: (layer, 0, 0)),
                  yspec(wa), yspec(wb), yspec(wc), yspec(wd),
                  gspec(0), gspec(1), gspec(2), gspec(3),
                  wspec(wa, 0), wspec(wb, wa // wb), wspec(wc, (wa + wb + wd) // wc),
                  wspec(wd, (wa + wb) // wd),
                  pl.BlockSpec((None, tn, D), lambda i, j: (layer, j, 0))],
        out_specs=pl.BlockSpec((tm, D), lambda i, j: (i, 0)),
        out_shape=jax.ShapeDtypeStruct((M, D), F32),
        scratch_shapes=[pltpu.VMEM((tm, D), BF16)],
        compiler_params=pltpu.CompilerParams(
            dimension_semantics=("parallel", "arbitrary"), vmem_limit_bytes=VMEM_LIMIT),
        name="gated_merge",
    )(h, norm, *ys, w_in, w_in, w_in, w_in, w_branch, w_branch, w_branch, w_branch, w_out)


def _rel_bucket(dist):
    max_exact = REL_BUCKETS // 2
    d = jnp.maximum(dist, 1).astype(F32)
    large = max_exact + (jnp.log(d / max_exact) / math.log(REL_MAX_DIST / max_exact)
                         * (REL_BUCKETS - max_exact)).astype(jnp.int32)
    large = jnp.minimum(large, REL_BUCKETS - 1)
    return jnp.where(dist < max_exact, dist, large)


def _window_bias(rel_bias, heads, dilation, max_dist, rows_heads=2):
    row = jnp.arange(WIN_BLOCK)[:, None]
    col = jnp.arange(2 * WIN_BLOCK)[None, :]
    dist = row + WIN_BLOCK - col
    valid = (dist >= 0) & (dist <= max_dist)
    assert list(heads) == list(range(heads[0], heads[-1] + 1))
    bucket = _rel_bucket(jnp.maximum(dist, 0) * dilation)
    per_head = rel_bias.astype(F32)[:, heads[0]:heads[-1] + 1]
    hit = bucket[None, None] == jnp.arange(REL_BUCKETS)[:, None, None, None]
    tab = jnp.sum(jnp.where(hit, per_head[:, :, None, None], 0.0), axis=0)
    tab = jnp.where(valid[None], tab, -jnp.inf)
    return tab.reshape(len(heads) // rows_heads, rows_heads * WIN_BLOCK, 2 * WIN_BLOCK)


def _trunk(x, ffn1_norm, ffn1_w13, ffn1_w2, mix_norm, w_in, hgrn_lb_logits, hgrn_out_norm,
           attn_sinks, w_branch, w_out, ffn2_norm, ffn2_w13, ffn2_w2, rel_bias, cfg):
    Bn, T, D = x.shape
    depth = w_in.shape[0]
    M = Bn * T
    a_w = A_HEADS * A_DK
    a_cols = 4 * a_w
    b_cols = 3 * B_HEADS * HEAD_DIM
    c_cols = 3 * C_HEADS * HEAD_DIM
    ac_cols = AC_BLOCKS * LANES
    bd_cols = BD_BLOCKS * LANES
    mix_in = ac_cols + bd_cols

    lb_sm = jax.nn.softmax(hgrn_lb_logits.astype(F32), axis=0)
    lb_all = (jnp.cumsum(lb_sm, axis=0) - lb_sm[0:1]).reshape(depth, A_HEADS, 1, A_DK)

    b0, c0 = a_cols, a_cols + b_cols
    dq0 = c0 + c_cols
    dk0 = dq0 + D_HEADS * HEAD_DIM
    c_w = C_HPG * HEAD_DIM

    perm = jnp.array([h * HEAD_DIM + d for h in D_HEAD_PERM for d in range(HEAD_DIM)])

    def c_part(part, g0, g1):
        base = c0 + part * C_HEADS * HEAD_DIM
        return w_in[:, :, base + g0 * c_w:base + g1 * c_w]

    n_grp = len(C_PATTERNS)
    w_in_b = jnp.concatenate(
        [w_in[:, :, :a_cols]] + [c_part(p, 1, n_grp) for p in range(3)]
        + [w_in[:, :, b0:c0], w_in[:, :, dq0:dk0][:, :, perm], w_in[:, :, dk0:mix_in]]
        + [c_part(p, 0, 1) for p in range(3)] + [w_in[:, :, mix_in:]], axis=-1).astype(BF16)
    d0 = 2 * a_w + c_w
    w_br_b = jnp.concatenate(
        [w_branch[:, :2 * a_w], w_branch[:, d0:][:, perm], w_branch[:, 2 * a_w:d0]], axis=1).astype(BF16)
    w_out_b = w_out.astype(BF16)
    f1_w13, f1_w2 = ffn1_w13.astype(BF16), ffn1_w2.astype(BF16)
    f2_w13, f2_w2 = ffn2_w13.astype(BF16), ffn2_w2.astype(BF16)

    c_bias = [_window_bias(rel_bias, list(range(g * C_HPG, (g + 1) * C_HPG)), dil, win // dil)
              for g, (win, dil) in enumerate(C_PATTERNS)]
    d_bias = _window_bias(rel_bias, [C_HEADS + h for h in range(D_HEADS)], 1, D_WINDOW - 1, D_HEADS)
    no_sink = jnp.full((C_HPG // 2, 2, LANES), -jnp.inf, F32)

    h = x.reshape(M, D)
    for l in range(depth):
        h = _ffn(h, ffn1_norm, f1_w13, f1_w2, l, cfg["ffn_tm"], cfg["ffn_tf"])

        pac, pbd = _proj(h, mix_norm, w_in_b, l, ac_cols, bd_cols, cfg["proj_tm"], cfg["proj_tn"])
        pac = pac.reshape(Bn, T, ac_cols)
        pbd = pbd.reshape(Bn, T, bd_cols)

        ya = _hgrn(pac, lb_all[l], hgrn_out_norm[l].reshape(1, 1, A_DK).astype(F32), cfg["hgrn_tt"])
        yb = _stick_breaking(pbd, cfg["sb_tq"])
        mix = ()
        for g, (win, dil) in enumerate(C_PATTERNS):
            if dil == 1:
                slab, cols = pbd, (_C0Q, _C0K, _C0V)
            else:
                slab, cols = pac, tuple(c + 2 * (g - 1) for c in (_CQ, _CK, _CV))
            if g < len(C_PATTERNS) - 1:
                mix += tuple(_window_attn(slab, c_bias[g], no_sink, dil, *cols, C_HPG // 2, 1, True, F32))
            else:
                (yc,) = _window_attn(slab, c_bias[g], no_sink, dil, *cols, C_HPG // 2, 1, True, BF16, mix)
        sinks = jnp.broadcast_to(attn_sinks[l].astype(F32).reshape(1, D_HEADS, 1), (1, D_HEADS, LANES))
        (yd,) = _window_attn(pbd, d_bias, sinks, 1, _DQ, _DK, _DV, 1, D_HEADS // 2, False, BF16)

        ys = (ya.reshape(M, a_w), yb.reshape(M, a_w), yc.reshape(M, c_w), yd.reshape(M, a_w))
        h = _merge(h, mix_norm, ys, w_in_b, mix_in, w_br_b, w_out_b, l, cfg["merge_tm"], cfg["merge_tn"])

        h = _ffn(h, ffn2_norm, f2_w13, f2_w2, l, cfg["ffn_tm"], cfg["ffn_tf"])
    return h.reshape(Bn, T, D)


_CFG = dict(ffn_tm=512, ffn_tf=512, proj_tm=1024, proj_tn=512, hgrn_tt=512, sb_tq=256,
            merge_tm=512, merge_tn=512)


def kernel(x, ffn1_norm, ffn1_w13, ffn1_w2, mix_norm, w_in, hgrn_lb_logits, hgrn_out_norm, attn_sinks,
           w_branch, w_out, ffn2_norm, ffn2_w13, ffn2_w2, rel_bias):
    return _trunk(x, ffn1_norm, ffn1_w13, ffn1_w2, mix_norm, w_in, hgrn_lb_logits, hgrn_out_norm,
                  attn_sinks, w_branch, w_out, ffn2_norm, ffn2_w13, ffn2_w2, rel_bias, _CFG)
```

```python
import functools
import math

import jax
import jax.numpy as jnp
from jax import lax
from jax.experimental import pallas as pl
from jax.experimental.pallas import tpu as pltpu

F32 = jnp.float32
BF16 = jnp.bfloat16

EPS = 1e-6
HEAD_DIM = 64
LANES = 128
A_HEADS = 4
A_DK = 128
B_HEADS = 8
C_PATTERNS = ((128, 1), (512, 4), (2048, 16))
C_HPG = 4
C_HEADS = len(C_PATTERNS) * C_HPG
D_HEADS = 8
D_WINDOW = 128
WIN_BLOCK = 128
REL_BUCKETS = 32
REL_MAX_DIST = 2048
A_SUB = 16
VMEM_LIMIT = 60 * 1024 * 1024

_CQ, _CK, _CV = 16, 20, 24
AC_BLOCKS = 28
_BQ, _BK, _BV = 0, 4, 8
_DQ, _DK, _DV = 12, 16, 17
_C0Q, _C0K, _C0V = 18, 20, 22
BD_BLOCKS = 24
WIN_UNROLL = 4
D_HEAD_PERM = (0, 4, 1, 5, 2, 6, 3, 7)
SB_CHAINS = 4
SB_EXIT = 100.0


def _rms(x, g):
    ms = jnp.mean(x * x, axis=-1, keepdims=True)
    return x * lax.rsqrt(ms + EPS) * g


def _sigmoid(x):
    return 1.0 / (1.0 + jnp.exp(-x))


def _dot(a, b):
    return jnp.dot(a, b, preferred_element_type=F32)


def _dot_nt(a, b):
    return lax.dot_general(a, b, (((1,), (1,)), ((), ())), preferred_element_type=F32)


def _split_bf16(x):
    hi = x.astype(BF16)
    lo = (x - hi.astype(F32)).astype(BF16)
    return hi, lo


def _ffn_kernel(x_ref, g_ref, w1_ref, w3_ref, w2_ref, o_ref, xn_ref):
    j = pl.program_id(1)

    @pl.when(j == 0)
    def _():
        xn_ref[...] = _rms(x_ref[...], g_ref[0:1, :]).astype(BF16)
        o_ref[...] = jnp.zeros_like(o_ref)

    xn = xn_ref[...]
    gate = _dot(xn, w1_ref[...])
    up = _dot(xn, w3_ref[...])
    act = (gate * _sigmoid(gate) * up).astype(BF16)
    o_ref[...] += _dot(act, w2_ref[...])

    @pl.when(j == pl.num_programs(1) - 1)
    def _():
        o_ref[...] = x_ref[...] + 0.5 * _rms(o_ref[...], g_ref[1:2, :])


def _ffn(h, norm, w13, w2, layer, tm, tf):
    M, D = h.shape
    F = w2.shape[1]
    nf = F // tf
    return pl.pallas_call(
        _ffn_kernel,
        grid=(M // tm, nf),
        in_specs=[
            pl.BlockSpec((tm, D), lambda i, j: (i, 0)),
            pl.BlockSpec((None, 2, D), lambda i, j: (layer, 0, 0)),
            pl.BlockSpec((None, D, tf), lambda i, j: (layer, 0, j)),
            pl.BlockSpec((None, D, tf), lambda i, j: (layer, 0, j + nf)),
            pl.BlockSpec((None, tf, D), lambda i, j: (layer, j, 0)),
        ],
        out_specs=pl.BlockSpec((tm, D), lambda i, j: (i, 0)),
        out_shape=jax.ShapeDtypeStruct((M, D), F32),
        scratch_shapes=[pltpu.VMEM((tm, D), BF16)],
        compiler_params=pltpu.CompilerParams(
            dimension_semantics=("parallel", "arbitrary"), vmem_limit_bytes=VMEM_LIMIT),
        name="ffn",
    )(h, norm, w13, w13, w2)


def _proj_kernel(x_ref, g_ref, w_ref, o32_ref, o16_ref, xn_ref, *, n32):
    j = pl.program_id(1)

    @pl.when(j == 0)
    def _():
        xn_ref[...] = _rms(x_ref[...], g_ref[0:1, :]).astype(BF16)

    @pl.when(j < n32)
    def _():
        o32_ref[...] = _dot(xn_ref[...], w_ref[...])

    @pl.when(j >= n32)
    def _():
        o16_ref[...] = _dot(xn_ref[...], w_ref[...]).astype(BF16)


def _proj(h, norm, w_in, layer, cols32, cols16, tm, tn):
    M, D = h.shape
    n32, n16 = cols32 // tn, cols16 // tn
    return pl.pallas_call(
        functools.partial(_proj_kernel, n32=n32),
        grid=(M // tm, n32 + n16),
        in_specs=[
            pl.BlockSpec((tm, D), lambda i, j: (i, 0)),
            pl.BlockSpec((None, 2, D), lambda i, j: (layer, 0, 0)),
            pl.BlockSpec((None, D, tn), lambda i, j: (layer, 0, j)),
        ],
        out_specs=[pl.BlockSpec((tm, tn), lambda i, j: (i, jnp.minimum(j, n32 - 1))),
                   pl.BlockSpec((tm, tn), lambda i, j: (i, jnp.maximum(j - n32, 0)))],
        out_shape=[jax.ShapeDtypeStruct((M, cols32), F32), jax.ShapeDtypeStruct((M, cols16), BF16)],
        scratch_shapes=[pltpu.VMEM((tm, D), BF16)],
        compiler_params=pltpu.CompilerParams(
            dimension_semantics=("parallel", "arbitrary"), vmem_limit_bytes=VMEM_LIMIT),
        name="proj",
    )(h, norm, w_in)


def _hgrn_kernel(q_ref, f_ref, i_ref, g_ref, lb_ref, ng_ref, o_ref, st_ref, kv_ref, sall_ref):
    tt = q_ref.shape[1]
    sub = A_SUB
    nb = tt // sub
    per = LANES // sub
    shape3 = (nb, sub, LANES)

    @pl.when(pl.program_id(2) == 0)
    def _():
        st_ref[...] = jnp.zeros_like(st_ref)

    lb = lb_ref[0]
    f = lb + (1.0 - lb) * _sigmoid(f_ref[0])
    k = 1.0 - f
    gl = jnp.log2(f)
    r = lax.broadcasted_iota(jnp.int32, (LANES, LANES), 0)
    c = lax.broadcasted_iota(jnp.int32, (LANES, LANES), 1)
    tril = ((r // sub == c // sub) & (c <= r)).astype(BF16)
    cums = []
    for a in range(tt // LANES):
        hi, lo = _split_bf16(gl[a * LANES:(a + 1) * LANES])
        cums.append(_dot(tril, hi) + _dot(tril, lo))
    b = jnp.concatenate(cums, axis=0)
    q = q_ref[0]
    v = i_ref[0]
    b3, q3, k3, v3 = (x.reshape(shape3) for x in (b, q, k, v))

    ones = jnp.ones((LANES, LANES), BF16)
    half = sub // 2
    rows = lax.broadcasted_iota(jnp.int32, (1, sub, LANES), 1)
    acc = [jnp.zeros(shape3, F32), jnp.zeros((nb, half, LANES), F32)]
    for s in range(sub):
        up = s >= half
        lo_row = half if up else 0
        qs, bs, rs = (x[:, lo_row:, :] for x in (q3, b3, rows))
        w = jnp.where(rs >= s, qs * jnp.exp2(bs - b3[:, s:s + 1, :]) * k3[:, s:s + 1, :], 0.0)
        w = w.reshape(-1, LANES).astype(BF16)
        acc[up] = acc[up] + _dot(w, ones).reshape(qs.shape) * v3[:, s:s + 1, :]
    o3 = acc[0] + jnp.concatenate([jnp.zeros_like(acc[1]), acc[1]], axis=1)

    b_last = b3[:, sub - 1:sub, :]
    kd = (k3 * jnp.exp2(b_last - b3)).reshape(tt, LANES)
    dec = jnp.exp2(b_last)
    for a in range(tt // LANES):
        v_t = v[a * LANES:(a + 1) * LANES].T.astype(BF16)
        kd_a = kd[a * LANES:(a + 1) * LANES]
        rhs = jnp.concatenate([jnp.where(r // sub == j, kd_a, 0.0).astype(BF16) for j in range(per)], axis=1)
        kv = _dot(v_t, rhs)
        for j in range(per):
            kv_ref[a * per + j] = kv[:, j * LANES:(j + 1) * LANES]

    st = st_ref[...]
    for n in range(nb):
        sall_ref[n] = st.astype(BF16)
        st = dec[n] * st + kv_ref[n]
    st_ref[...] = st

    qe = (q * jnp.exp2(b)).astype(BF16)
    inter = []
    for a in range(tt // LANES):
        stack = sall_ref[a * per:(a + 1) * per].reshape(per * LANES, LANES)
        p = _dot_nt(qe[a * LANES:(a + 1) * LANES], stack)
        inter += [p[j * sub:(j + 1) * sub, j * LANES:(j + 1) * LANES] for j in range(per)]
    o = o3.reshape(tt, LANES) + jnp.concatenate(inter, axis=0)

    g = g_ref[0]
    o_ref[0] = (_rms(o, ng_ref[0]) * (g * _sigmoid(g))).astype(o_ref.dtype)


def _hgrn(pa, lb, ng, tt):
    Bn, T, _ = pa.shape
    H = A_HEADS

    def col(off):
        return pl.BlockSpec((1, tt, A_DK), lambda b, h, t: (b, t, off * H + h))

    return pl.pallas_call(
        _hgrn_kernel,
        grid=(Bn, H, T // tt),
        in_specs=[col(0), col(1), col(2), col(3),
                  pl.BlockSpec((1, 1, A_DK), lambda b, h, t: (h, 0, 0)),
                  pl.BlockSpec((1, 1, A_DK), lambda b, h, t: (0, 0, 0))],
        out_specs=pl.BlockSpec((1, tt, A_DK), lambda b, h, t: (b, t, h)),
        out_shape=jax.ShapeDtypeStruct((Bn, T, H * A_DK), BF16),
        scratch_shapes=[pltpu.VMEM((A_DK, A_DK), F32), pltpu.VMEM((tt // A_SUB, A_DK, A_DK), F32),
                        pltpu.VMEM((tt // A_SUB, A_DK, A_DK), BF16)],
        compiler_params=pltpu.CompilerParams(
            dimension_semantics=("parallel", "parallel", "arbitrary"), vmem_limit_bytes=VMEM_LIMIT),
        name="hgrn2",
    )(pa, pa, pa, pa, lb, ng)


def _sb_kernel(q_ref, k_ref, v_ref, o_ref, suf_ref, acc_ref, run_ref):
    tq = q_ref.shape[1]
    i = pl.program_id(2)
    in_h0 = lax.broadcasted_iota(jnp.int32, (tq, LANES), 1) < HEAD_DIM
    r = lax.broadcasted_iota(jnp.int32, (tq, tq), 0)
    c = lax.broadcasted_iota(jnp.int32, (tq, tq), 1)
    suf_ref[...] = (r >= c).astype(BF16)
    strict = jnp.concatenate([c < r, c < r], axis=0)

    def lanes(n):
        return slice(n * LANES, (n + 1) * LANES)

    q2s = []
    for n in range(SB_CHAINS):
        q = q_ref[0, :, lanes(n)]
        zero = jnp.zeros_like(q)
        q2s.append(jnp.concatenate([jnp.where(in_h0, q, zero), jnp.where(in_h0, zero, q)], axis=0)
                   * (HEAD_DIM ** -0.5))

    def tile(q2, kb, vb, run, diag):
        z = _dot_nt(q2, kb)
        sp = jnp.maximum(z, 0.0) + jnp.log(1.0 + jnp.exp(-jnp.abs(z)))
        if diag:
            sp = jnp.where(strict, sp, 0.0)
        hi, lo = _split_bf16(sp)
        suffix = suf_ref[...]
        cs = _dot(hi, suffix) + _dot(lo, suffix) + run
        a = jnp.exp(z - cs)
        if diag:
            a = jnp.where(strict, a, 0.0)
        return _dot(a.astype(BF16), vb), cs[:, 0:1]

    def step(rows, first):
        lowest = None
        for n in range(SB_CHAINS):
            run = jnp.zeros((2 * tq, 1), F32) if first else run_ref[n]
            pv, run = tile(q2s[n], k_ref[0, rows, lanes(n)], v_ref[0, rows, lanes(n)], run, first)
            acc_ref[n] = pv if first else acc_ref[n] + pv
            run_ref[n] = run
            low = jnp.min(run)
            lowest = low if lowest is None else jnp.minimum(lowest, low)
        return lowest

    lowest = step(pl.ds(pl.multiple_of(i * tq, tq), tq), True)

    def cond(carry):
        s, lowest = carry
        return (s < i) & (lowest <= SB_EXIT)

    def body(carry):
        s, _ = carry
        return s + 1, step(pl.ds(pl.multiple_of((i - 1 - s) * tq, tq), tq), False)

    lax.while_loop(cond, body, (jnp.int32(0), lowest))
    for n in range(SB_CHAINS):
        acc = acc_ref[n]
        o_ref[0, :, lanes(n)] = jnp.where(in_h0, acc[:tq], acc[tq:]).astype(o_ref.dtype)


def _stick_breaking(pbd, tq):
    Bn, T, _ = pbd.shape
    ngrp = B_HEADS // 2 // SB_CHAINS
    w = SB_CHAINS * LANES
    return pl.pallas_call(
        _sb_kernel,
        grid=(Bn, ngrp, T // tq),
        in_specs=[pl.BlockSpec((1, tq, w), lambda b, p, i: (b, i, _BQ // SB_CHAINS + p)),
                  pl.BlockSpec((1, T, w), lambda b, p, i: (b, 0, _BK // SB_CHAINS + p)),
                  pl.BlockSpec((1, T, w), lambda b, p, i: (b, 0, _BV // SB_CHAINS + p))],
        out_specs=pl.BlockSpec((1, tq, w), lambda b, p, i: (b, i, p)),
        out_shape=jax.ShapeDtypeStruct((Bn, T, B_HEADS // 2 * LANES), BF16),
        scratch_shapes=[pltpu.VMEM((tq, tq), BF16), pltpu.VMEM((SB_CHAINS, 2 * tq, LANES), F32),
                        pltpu.VMEM((SB_CHAINS, 2 * tq, 1), F32)],
        compiler_params=pltpu.CompilerParams(
            dimension_semantics=("parallel", "parallel", "arbitrary"), vmem_limit_bytes=VMEM_LIMIT),
        name="stick_breaking",
    )(pbd, pbd, pbd)


def _win_kernel(q_ref, k_ref, kp_ref, v_ref, vp_ref, bias_ref, sink_ref, *rest, dil, nsub, nq, n_mix):
    mix_refs, out_refs = rest[:2 * n_mix], rest[2 * n_mix + (1 if n_mix else 0):]
    blk = WIN_BLOCK
    span = blk * dil
    nrow = 2 * nq * blk
    i = pl.program_id(2)
    lane = lax.broadcasted_iota(jnp.int32, (blk, LANES), 1)
    in_h0 = lane < HEAD_DIM
    prev_col = lax.broadcasted_iota(jnp.int32, (nrow, 2 * blk), 1) < blk
    ones = jnp.ones((2 * blk, LANES), BF16)
    sinks = jnp.concatenate([jnp.broadcast_to(sink_ref[hh:hh + 1, 0:1], (blk, 1)) for hh in range(2 * nq)],
                            axis=0)
    bias = bias_ref[...]

    def rows_of(start):
        return pl.ds(start, blk, stride=dil) if dil > 1 else pl.ds(start, blk)

    def residue_class(s, c):
        cur = rows_of(s * span + c)
        q = q_ref[0, cur, :].astype(BF16)
        if s == 0:
            kp, vp = kp_ref[0, rows_of(c), :], vp_ref[0, rows_of(c), :]
        else:
            prev = rows_of((s - 1) * span + c)
            kp, vp = k_ref[0, prev, :], v_ref[0, prev, :]
        k2 = jnp.concatenate([kp.astype(BF16), k_ref[0, cur, :].astype(BF16)], axis=0)
        v2 = jnp.concatenate([vp.astype(BF16), v_ref[0, cur, :].astype(BF16)], axis=0)
        qs = [q[:, p * LANES:(p + 1) * LANES] for p in range(nq)]
        zero = jnp.zeros_like(qs[0])
        q2 = jnp.concatenate([jnp.where(in_h0, x, zero) for x in qs]
                             + [jnp.where(in_h0, zero, x) for x in qs], axis=0)
        logits = _dot_nt(q2 * (HEAD_DIM ** -0.5), k2) + bias
        if s == 0:
            logits = jnp.where(jnp.logical_and(i == 0, prev_col), -jnp.inf, logits)
        m = jnp.max(jnp.maximum(logits[:, :blk], logits[:, blk:]), axis=-1, keepdims=True)
        m = jnp.maximum(m, sinks)
        p = jnp.exp(logits - m).astype(BF16)
        if nq > 1:
            pv = _dot(p, jnp.concatenate([v2, ones], axis=1))
            num, den = pv[:, :LANES], pv[:, LANES:]
        else:
            num, den = _dot(p, v2), _dot(p, ones)
        den = den + jnp.exp(sinks - m)
        o2 = num / den

        def lane_blocks(x):
            return jnp.concatenate([jnp.where(in_h0, x[b * blk:(b + 1) * blk], x[(nq + b) * blk:(nq + b + 1) * blk])
                                    for b in range(nq)], axis=1)

        out_refs[0][0, cur, :] = lane_blocks(o2).astype(out_refs[0].dtype)
        if len(out_refs) > 1:
            out_refs[1][0, cur, :] = lane_blocks(m + jnp.log(den))

    for s in range(nsub):
        if dil <= WIN_UNROLL:
            for c in range(dil):
                residue_class(s, c)
        else:
            def body(c, carry, s=s):
                residue_class(s, c)
                return carry
            lax.fori_loop(0, dil, body, 0, unroll=WIN_UNROLL)

    if n_mix:
        outs = [r[0] for r in mix_refs[0::2]] + [out_refs[0][0]]
        lses = [r[0] for r in mix_refs[1::2]] + [out_refs[1][0]]
        top = functools.reduce(jnp.maximum, lses)
        ws = [jnp.exp(l - top) for l in lses]
        y_ref = rest[2 * n_mix]
        y_ref[0] = (sum(w * o for w, o in zip(ws, outs)) / sum(ws)).astype(y_ref.dtype)


def _window_attn(slab, bias, sinks, dil, qcol, kcol, vcol, ngrp, nq, with_lse, out_dtype, mix=()):
    Bn, T, _ = slab.shape
    span = WIN_BLOCK * dil
    nsub = max(1, WIN_UNROLL // dil)
    tb = span * nsub
    assert qcol % nq == 0

    def kvmap(col, prev):
        def f(b, g, i):
            return (b, jnp.maximum(i * nsub - 1, 0) if prev else i, col + g)
        return f

    qblk = (1, tb, nq * LANES)
    blk = (1, tb, LANES)
    pblk = (1, span, LANES)
    out_spec = pl.BlockSpec(qblk, lambda b, g, i: (b, i, g))
    out_sds = jax.ShapeDtypeStruct((Bn, T, ngrp * nq * LANES), out_dtype)
    lse_sds = jax.ShapeDtypeStruct((Bn, T, ngrp * nq * LANES), F32)
    if mix:
        out_specs, out_shape = [out_spec], [out_sds]
        scratch = [pltpu.VMEM(qblk, F32), pltpu.VMEM(qblk, F32)]
    else:
        out_specs = [out_spec, out_spec] if with_lse else [out_spec]
        out_shape = [out_sds, lse_sds] if with_lse else [out_sds]
        scratch = []
    return pl.pallas_call(
        functools.partial(_win_kernel, dil=dil, nsub=nsub, nq=nq, n_mix=len(mix) // 2),
        grid=(Bn, ngrp, T // tb),
        in_specs=[pl.BlockSpec(qblk, lambda b, g, i: (b, i, qcol // nq + g)),
                  pl.BlockSpec(blk, kvmap(kcol, False)), pl.BlockSpec(pblk, kvmap(kcol, True)),
                  pl.BlockSpec(blk, kvmap(vcol, False)), pl.BlockSpec(pblk, kvmap(vcol, True)),
                  pl.BlockSpec((None, 2 * nq * WIN_BLOCK, 2 * WIN_BLOCK), lambda b, g, i: (g, 0, 0)),
                  pl.BlockSpec((None, 2 * nq, LANES), lambda b, g, i: (g, 0, 0))] + [out_spec] * len(mix),
        out_specs=out_specs,
        out_shape=out_shape,
        scratch_shapes=scratch,
        compiler_params=pltpu.CompilerParams(
            dimension_semantics=("parallel", "parallel", "arbitrary"), vmem_limit_bytes=VMEM_LIMIT),
        name="window_attn",
    )(slab, slab, slab, slab, slab, bias, sinks, *mix)


def _merge_kernel(x_ref, g_ref, ya_ref, yb_ref, yc_ref, yd_ref,
                  ga_ref, gb_ref, gc_ref, gd_ref, wa_ref, wb_ref, wc_ref, wd_ref, wo_ref,
                  o_ref, u_ref):
    j = pl.program_id(1)

    @pl.when(j == 0)
    def _():
        u_ref[...] = _rms(x_ref[...], g_ref[0:1, :]).astype(BF16)
        o_ref[...] = jnp.zeros_like(o_ref)

    u = u_ref[...]
    merged = None
    for y_ref, gate_ref, w_ref in ((ya_ref, ga_ref, wa_ref), (yb_ref, gb_ref, wb_ref),
                                   (yc_ref, gc_ref, wc_ref), (yd_ref, gd_ref, wd_ref)):
        term = _sigmoid(_dot(u, gate_ref[...])) * _dot(y_ref[...], w_ref[...])
        merged = term if merged is None else merged + term
    o_ref[...] += _dot(merged.astype(BF16), wo_ref[...])

    @pl.when(j == pl.num_programs(1) - 1)
    def _():
        o_ref[...] = x_ref[...] + _rms(o_ref[...], g_ref[1:2, :])


def _merge(h, norm, ys, w_in, gate_col0, w_branch, w_out, layer, tm, tn):
    M, D = h.shape
    gcb = gate_col0 // tn
    dcb = D // tn

    def yspec(w):
        return pl.BlockSpec((tm, w), lambda i, j: (i, 0))

    def gspec(br):
        return pl.BlockSpec((None, D, tn), lambda i, j: (layer, 0, gcb + br * dcb + j))

    def wspec(rows, rb):
        return pl.BlockSpec((None, rows, tn), lambda i, j: (layer, rb, j))

    wa, wb, wc, wd = (y.shape[1] for y in ys)
    return pl.pallas_call(
        _merge_kernel,
        grid=(M // tm, dcb),
        in_specs=[pl.BlockSpec((tm, D), lambda i, j: (i, 0)),
                  pl.BlockSpec((None, 2, D), lambda i, j: (layer, 0, 0)),
                  yspec(wa), yspec(wb), yspec(wc), yspec(wd),
                  gspec(0), gspec(1), gspec(2), gspec(3),
                  wspec(wa, 0), wspec(wb, wa // wb), wspec(wc, (wa + wb + wd) // wc),
                  wspec(wd, (wa + wb) // wd),
                  pl.BlockSpec((None, tn, D), lambda i, j: (layer, j, 0))],
        out_specs=pl.BlockSpec((tm, D), lambda i, j: (i, 0)),
        out_shape=jax.ShapeDtypeStruct((M, D), F32),
        scratch_shapes=[pltpu.VMEM((tm, D), BF16)],
        compiler_params=pltpu.CompilerParams(
            dimension_semantics=("parallel", "arbitrary"), vmem_limit_bytes=VMEM_LIMIT),
        name="gated_merge",
    )(h, norm, *ys, w_in, w_in, w_in, w_in, w_branch, w_branch, w_branch, w_branch, w_out)


def _rel_bucket(dist):
    max_exact = REL_BUCKETS // 2
    d = jnp.maximum(dist, 1).astype(F32)
    large = max_exact + (jnp.log(d / max_exact) / math.log(REL_MAX_DIST / max_exact)
                         * (REL_BUCKETS - max_exact)).astype(jnp.int32)
    large = jnp.minimum(large, REL_BUCKETS - 1)
    return jnp.where(dist < max_exact, dist, large)


def _window_bias(rel_bias, heads, dilation, max_dist, rows_heads=2):
    row = jnp.arange(WIN_BLOCK)[:, None]
    col = jnp.arange(2 * WIN_BLOCK)[None, :]
    dist = row + WIN_BLOCK - col
    valid = (dist >= 0) & (dist <= max_dist)
    assert list(heads) == list(range(heads[0], heads[-1] + 1))
    bucket = _rel_bucket(jnp.maximum(dist, 0) * dilation)
    per_head = rel_bias.astype(F32)[:, heads[0]:heads[-1] + 1]
    hit = bucket[None, None] == jnp.arange(REL_BUCKETS)[:, None, None, None]
    tab = jnp.sum(jnp.where(hit, per_head[:, :, None, None], 0.0), axis=0)
    tab = jnp.where(valid[None], tab, -jnp.inf)
    return tab.reshape(len(heads) // rows_heads, rows_heads * WIN_BLOCK, 2 * WIN_BLOCK)


def _trunk(x, ffn1_norm, ffn1_w13, ffn1_w2, mix_norm, w_in, hgrn_lb_logits, hgrn_out_norm,
           attn_sinks, w_branch, w_out, ffn2_norm, ffn2_w13, ffn2_w2, rel_bias, cfg):
    Bn, T, D = x.shape
    depth = w_in.shape[0]
    M = Bn * T
    a_w = A_HEADS * A_DK
    a_cols = 4 * a_w
    b_cols = 3 * B_HEADS * HEAD_DIM
    c_cols = 3 * C_HEADS * HEAD_DIM
    ac_cols = AC_BLOCKS * LANES
    bd_cols = BD_BLOCKS * LANES
    mix_in = ac_cols + bd_cols

    lb_sm = jax.nn.softmax(hgrn_lb_logits.astype(F32), axis=0)
    lb_all = jnp.concatenate([jnp.zeros_like(lb_sm[0:1]), jnp.cumsum(lb_sm[1:], axis=0)], axis=0)
    lb_all = lb_all.reshape(depth, A_HEADS, 1, A_DK)

    b0, c0 = a_cols, a_cols + b_cols
    dq0 = c0 + c_cols
    dk0 = dq0 + D_HEADS * HEAD_DIM
    c_w = C_HPG * HEAD_DIM

    perm = jnp.array([h * HEAD_DIM + d for h in D_HEAD_PERM for d in range(HEAD_DIM)])

    def c_part(part, g0, g1):
        base = c0 + part * C_HEADS * HEAD_DIM
        return w_in[:, :, base + g0 * c_w:base + g1 * c_w]

    n_grp = len(C_PATTERNS)
    w_in_b = jnp.concatenate(
        [w_in[:, :, :a_cols]] + [c_part(p, 1, n_grp) for p in range(3)]
        + [w_in[:, :, b0:c0], w_in[:, :, dq0:dk0][:, :, perm], w_in[:, :, dk0:mix_in]]
        + [c_part(p, 0, 1) for p in range(3)] + [w_in[:, :, mix_in:]], axis=-1).astype(BF16)
    d0 = 2 * a_w + c_w
    w_br_b = jnp.concatenate(
        [w_branch[:, :2 * a_w], w_branch[:, d0:][:, perm], w_branch[:, 2 * a_w:d0]], axis=1).astype(BF16)
    w_out_b = w_out.astype(BF16)
    f1_w13, f1_w2 = ffn1_w13.astype(BF16), ffn1_w2.astype(BF16)
    f2_w13, f2_w2 = ffn2_w13.astype(BF16), ffn2_w2.astype(BF16)

    c_bias = [_window_bias(rel_bias, list(range(g * C_HPG, (g + 1) * C_HPG)), dil, win // dil)
              for g, (win, dil) in enumerate(C_PATTERNS)]
    d_bias = _window_bias(rel_bias, [C_HEADS + h for h in range(D_HEADS)], 1, D_WINDOW - 1, D_HEADS)
    no_sink = jnp.full((C_HPG // 2, 2, LANES), -jnp.inf, F32)

    h = x.reshape(M, D)
    for l in range(depth):
        h = _ffn(h, ffn1_norm, f1_w13, f1_w2, l, cfg["ffn_tm"], cfg["ffn_tf"])

        pac, pbd = _proj(h, mix_norm, w_in_b, l, ac_cols, bd_cols, cfg["proj_tm"], cfg["proj_tn"])
        pac = pac.reshape(Bn, T, ac_cols)
        pbd = pbd.reshape(Bn, T, bd_cols)

        ya = _hgrn(pac, lb_all[l], hgrn_out_norm[l].reshape(1, 1, A_DK).astype(F32), cfg["hgrn_tt"])
        yb = _stick_breaking(pbd, cfg["sb_tq"])
        mix = ()
        for g, (win, dil) in enumerate(C_PATTERNS):
            if dil == 1:
                slab, cols = pbd, (_C0Q, _C0K, _C0V)
            else:
                slab, cols = pac, tuple(c + 2 * (g - 1) for c in (_CQ, _CK, _CV))
            if g < len(C_PATTERNS) - 1:
                mix += tuple(_window_attn(slab, c_bias[g], no_sink, dil, *cols, C_HPG // 2, 1, True, F32))
            else:
                (yc,) = _window_attn(slab, c_bias[g], no_sink, dil, *cols, C_HPG // 2, 1, True, BF16, mix)
        sinks = jnp.broadcast_to(attn_sinks[l].astype(F32).reshape(1, D_HEADS, 1), (1, D_HEADS, LANES))
        (yd,) = _window_attn(pbd, d_bias, sinks, 1, _DQ, _DK, _DV, 1, D_HEADS // 2, False, BF16)

        ys = (ya.reshape(M, a_w), yb.reshape(M, a_w), yc.reshape(M, c_w), yd.reshape(M, a_w))
        h = _merge(h, mix_norm, ys, w_in_b, mix_in, w_br_b, w_out_b, l, cfg["merge_tm"], cfg["merge_tn"])

        h = _ffn(h, ffn2_norm, f2_w13, f2_w2, l, cfg["ffn_tm"], cfg["ffn_tf"])
    return h.reshape(Bn, T, D)


_CFG = dict(ffn_tm=512, ffn_tf=512, proj_tm=1024, proj_tn=512, hgrn_tt=512, sb_tq=256,
            merge_tm=512, merge_tn=512)


def kernel(x, ffn1_norm, ffn1_w13, ffn1_w2, mix_norm, w_in, hgrn_lb_logits, hgrn_out_norm, attn_sinks,
           w_branch, w_out, ffn2_norm, ffn2_w13, ffn2_w2, rel_bias):
    return _trunk(x, ffn1_norm, ffn1_w13, ffn1_w2, mix_norm, w_in, hgrn_lb_logits, hgrn_out_norm,
                  attn_sinks, w_branch, w_out, ffn2_norm, ffn2_w13, ffn2_w2, rel_bias, _CFG)
```

```python
import functools
import math

import jax
import jax.numpy as jnp
from jax import lax
from jax.experimental import pallas as pl
from jax.experimental.pallas import tpu as pltpu

F32 = jnp.float32
BF16 = jnp.bfloat16

EPS = 1e-6
HEAD_DIM = 64
LANES = 128
A_HEADS = 4
A_DK = 128
B_HEADS = 8
C_PATTERNS = ((128, 1), (512, 4), (2048, 16))
C_HPG = 4
C_HEADS = len(C_PATTERNS) * C_HPG
D_HEADS = 8
D_WINDOW = 128
WIN_BLOCK = 128
REL_BUCKETS = 32
REL_MAX_DIST = 2048
A_SUB = 16
VMEM_LIMIT = 60 * 1024 * 1024

_CQ, _CK, _CV = 16, 20, 24
AC_BLOCKS = 28
_BQ, _BK, _BV = 0, 4, 8
_DQ, _DK, _DV = 12, 16, 17
_C0Q, _C0K, _C0V = 18, 20, 22
BD_BLOCKS = 24
WIN_UNROLL = 4
WIN_TALL_SPANS = 4
D_HEAD_PERM = (0, 4, 1, 5, 2, 6, 3, 7)
SB_CHAINS = 4
SB_EXIT = 100.0


def _rms(x, g):
    ms = jnp.mean(x * x, axis=-1, keepdims=True)
    return x * lax.rsqrt(ms + EPS) * g


def _sigmoid(x):
    return 1.0 / (1.0 + jnp.exp(-x))


def _dot(a, b):
    return jnp.dot(a, b, preferred_element_type=F32)


def _dot_nt(a, b):
    return lax.dot_general(a, b, (((1,), (1,)), ((), ())), preferred_element_type=F32)


def _split_bf16(x):
    hi = x.astype(BF16)
    lo = (x - hi.astype(F32)).astype(BF16)
    return hi, lo


def _ffn_kernel(x_ref, g_ref, w1_ref, w3_ref, w2_ref, o_ref, xn_ref):
    j = pl.program_id(1)

    @pl.when(j == 0)
    def _():
        xn_ref[...] = _rms(x_ref[...], g_ref[0:1, :]).astype(BF16)
        o_ref[...] = jnp.zeros_like(o_ref)

    xn = xn_ref[...]
    gate = _dot(xn, w1_ref[...])
    up = _dot(xn, w3_ref[...])
    act = (gate * _sigmoid(gate) * up).astype(BF16)
    o_ref[...] += _dot(act, w2_ref[...])

    @pl.when(j == pl.num_programs(1) - 1)
    def _():
        o_ref[...] = x_ref[...] + 0.5 * _rms(o_ref[...], g_ref[1:2, :])


def _ffn(h, norm, w13, w2, layer, tm, tf):
    M, D = h.shape
    F = w2.shape[1]
    nf = F // tf
    return pl.pallas_call(
        _ffn_kernel,
        grid=(M // tm, nf),
        in_specs=[
            pl.BlockSpec((tm, D), lambda i, j: (i, 0)),
            pl.BlockSpec((None, 2, D), lambda i, j: (layer, 0, 0)),
            pl.BlockSpec((None, D, tf), lambda i, j: (layer, 0, j)),
            pl.BlockSpec((None, D, tf), lambda i, j: (layer, 0, j + nf)),
            pl.BlockSpec((None, tf, D), lambda i, j: (layer, j, 0)),
        ],
        out_specs=pl.BlockSpec((tm, D), lambda i, j: (i, 0)),
        out_shape=jax.ShapeDtypeStruct((M, D), F32),
        scratch_shapes=[pltpu.VMEM((tm, D), BF16)],
        compiler_params=pltpu.CompilerParams(
            dimension_semantics=("parallel", "arbitrary"), vmem_limit_bytes=VMEM_LIMIT),
        name="ffn",
    )(h, norm, w13, w13, w2)


def _proj_kernel(x_ref, g_ref, w_ref, o32_ref, o16_ref, xn_ref, *, n32):
    j = pl.program_id(1)

    @pl.when(j == 0)
    def _():
        xn_ref[...] = _rms(x_ref[...], g_ref[0:1, :]).astype(BF16)

    @pl.when(j < n32)
    def _():
        o32_ref[...] = _dot(xn_ref[...], w_ref[...])

    @pl.when(j >= n32)
    def _():
        o16_ref[...] = _dot(xn_ref[...], w_ref[...]).astype(BF16)


def _proj(h, norm, w_in, layer, cols32, cols16, tm, tn):
    M, D = h.shape
    n32, n16 = cols32 // tn, cols16 // tn
    return pl.pallas_call(
        functools.partial(_proj_kernel, n32=n32),
        grid=(M // tm, n32 + n16),
        in_specs=[
            pl.BlockSpec((tm, D), lambda i, j: (i, 0)),
            pl.BlockSpec((None, 2, D), lambda i, j: (layer, 0, 0)),
            pl.BlockSpec((None, D, tn), lambda i, j: (layer, 0, j)),
        ],
        out_specs=[pl.BlockSpec((tm, tn), lambda i, j: (i, jnp.minimum(j, n32 - 1))),
                   pl.BlockSpec((tm, tn), lambda i, j: (i, jnp.maximum(j - n32, 0)))],
        out_shape=[jax.ShapeDtypeStruct((M, cols32), F32), jax.ShapeDtypeStruct((M, cols16), BF16)],
        scratch_shapes=[pltpu.VMEM((tm, D), BF16)],
        compiler_params=pltpu.CompilerParams(
            dimension_semantics=("parallel", "arbitrary"), vmem_limit_bytes=VMEM_LIMIT),
        name="proj",
    )(h, norm, w_in)


def _hgrn_kernel(q_ref, f_ref, i_ref, g_ref, lb_ref, ng_ref, o_ref, st_ref, kv_ref, sall_ref):
    tt = q_ref.shape[1]
    sub = A_SUB
    nb = tt // sub
    per = LANES // sub
    shape3 = (nb, sub, LANES)

    @pl.when(pl.program_id(2) == 0)
    def _():
        st_ref[...] = jnp.zeros_like(st_ref)

    lb = lb_ref[0]
    f = lb + (1.0 - lb) * _sigmoid(f_ref[0])
    k = 1.0 - f
    gl = jnp.log2(f)
    r = lax.broadcasted_iota(jnp.int32, (LANES, LANES), 0)
    c = lax.broadcasted_iota(jnp.int32, (LANES, LANES), 1)
    tril = ((r // sub == c // sub) & (c <= r)).astype(BF16)
    cums = []
    for a in range(tt // LANES):
        hi, lo = _split_bf16(gl[a * LANES:(a + 1) * LANES])
        cums.append(_dot(tril, hi) + _dot(tril, lo))
    b = jnp.concatenate(cums, axis=0)
    q = q_ref[0]
    v = i_ref[0]
    b3, q3, k3, v3 = (x.reshape(shape3) for x in (b, q, k, v))

    ones = jnp.ones((LANES, LANES), BF16)
    half = sub // 2
    rows = lax.broadcasted_iota(jnp.int32, (1, sub, LANES), 1)
    acc = [jnp.zeros(shape3, F32), jnp.zeros((nb, half, LANES), F32)]
    for s in range(sub):
        up = s >= half
        lo_row = half if up else 0
        qs, bs, rs = (x[:, lo_row:, :] for x in (q3, b3, rows))
        w = jnp.where(rs >= s, qs * jnp.exp2(bs - b3[:, s:s + 1, :]) * k3[:, s:s + 1, :], 0.0)
        w = w.reshape(-1, LANES).astype(BF16)
        acc[up] = acc[up] + _dot(w, ones).reshape(qs.shape) * v3[:, s:s + 1, :]
    o3 = acc[0] + jnp.concatenate([jnp.zeros_like(acc[1]), acc[1]], axis=1)

    b_last = b3[:, sub - 1:sub, :]
    kd = (k3 * jnp.exp2(b_last - b3)).reshape(tt, LANES)
    dec = jnp.exp2(b_last)
    for a in range(tt // LANES):
        v_t = v[a * LANES:(a + 1) * LANES].T.astype(BF16)
        kd_a = kd[a * LANES:(a + 1) * LANES]
        rhs = jnp.concatenate([jnp.where(r // sub == j, kd_a, 0.0).astype(BF16) for j in range(per)], axis=1)
        kv = _dot(v_t, rhs)
        for j in range(per):
            kv_ref[a * per + j] = kv[:, j * LANES:(j + 1) * LANES]

    st = st_ref[...]
    for n in range(nb):
        sall_ref[n] = st.astype(BF16)
        st = dec[n] * st + kv_ref[n]
    st_ref[...] = st

    qe = (q * jnp.exp2(b)).astype(BF16)
    inter = []
    for a in range(tt // LANES):
        stack = sall_ref[a * per:(a + 1) * per].reshape(per * LANES, LANES)
        p = _dot_nt(qe[a * LANES:(a + 1) * LANES], stack)
        inter += [p[j * sub:(j + 1) * sub, j * LANES:(j + 1) * LANES] for j in range(per)]
    o = o3.reshape(tt, LANES) + jnp.concatenate(inter, axis=0)

    g = g_ref[0]
    o_ref[0] = (_rms(o, ng_ref[0]) * (g * _sigmoid(g))).astype(o_ref.dtype)


def _hgrn(pa, lb, ng, tt):
    Bn, T, _ = pa.shape
    H = A_HEADS

    def col(off):
        return pl.BlockSpec((1, tt, A_DK), lambda b, h, t: (b, t, off * H + h))

    return pl.pallas_call(
        _hgrn_kernel,
        grid=(Bn, H, T // tt),
        in_specs=[col(0), col(1), col(2), col(3),
                  pl.BlockSpec((1, 1, A_DK), lambda b, h, t: (h, 0, 0)),
                  pl.BlockSpec((1, 1, A_DK), lambda b, h, t: (0, 0, 0))],
        out_specs=pl.BlockSpec((1, tt, A_DK), lambda b, h, t: (b, t, h)),
        out_shape=jax.ShapeDtypeStruct((Bn, T, H * A_DK), BF16),
        scratch_shapes=[pltpu.VMEM((A_DK, A_DK), F32), pltpu.VMEM((tt // A_SUB, A_DK, A_DK), F32),
                        pltpu.VMEM((tt // A_SUB, A_DK, A_DK), BF16)],
        compiler_params=pltpu.CompilerParams(
            dimension_semantics=("parallel", "parallel", "arbitrary"), vmem_limit_bytes=VMEM_LIMIT),
        name="hgrn2",
    )(pa, pa, pa, pa, lb, ng)


def _sb_kernel(q_ref, k_ref, v_ref, o_ref, suf_ref, acc_ref, run_ref):
    tq = q_ref.shape[1]
    i = pl.program_id(2)
    in_h0 = lax.broadcasted_iota(jnp.int32, (tq, LANES), 1) < HEAD_DIM
    r = lax.broadcasted_iota(jnp.int32, (tq, tq), 0)
    c = lax.broadcasted_iota(jnp.int32, (tq, tq), 1)
    suf_ref[...] = (r >= c).astype(BF16)
    strict = jnp.concatenate([c < r, c < r], axis=0)

    def lanes(n):
        return slice(n * LANES, (n + 1) * LANES)

    q2s = []
    for n in range(SB_CHAINS):
        q = q_ref[0, :, lanes(n)]
        zero = jnp.zeros_like(q)
        q2s.append(jnp.concatenate([jnp.where(in_h0, q, zero), jnp.where(in_h0, zero, q)], axis=0)
                   * (HEAD_DIM ** -0.5))

    def tile(q2, kb, vb, run, diag):
        z = _dot_nt(q2, kb)
        sp = jnp.maximum(z, 0.0) + jnp.log(1.0 + jnp.exp(-jnp.abs(z)))
        if diag:
            sp = jnp.where(strict, sp, 0.0)
        hi, lo = _split_bf16(sp)
        suffix = suf_ref[...]
        cs = _dot(hi, suffix) + _dot(lo, suffix) + run
        a = jnp.exp(z - cs)
        if diag:
            a = jnp.where(strict, a, 0.0)
        return _dot(a.astype(BF16), vb), cs[:, 0:1]

    def step(rows, first):
        lowest = None
        for n in range(SB_CHAINS):
            run = jnp.zeros((2 * tq, 1), F32) if first else run_ref[n]
            pv, run = tile(q2s[n], k_ref[0, rows, lanes(n)], v_ref[0, rows, lanes(n)], run, first)
            acc_ref[n] = pv if first else acc_ref[n] + pv
            run_ref[n] = run
            low = jnp.min(run)
            lowest = low if lowest is None else jnp.minimum(lowest, low)
        return lowest

    lowest = step(pl.ds(pl.multiple_of(i * tq, tq), tq), True)

    def cond(carry):
        s, lowest = carry
        return (s < i) & (lowest <= SB_EXIT)

    def body(carry):
        s, _ = carry
        return s + 1, step(pl.ds(pl.multiple_of((i - 1 - s) * tq, tq), tq), False)

    lax.while_loop(cond, body, (jnp.int32(0), lowest))
    for n in range(SB_CHAINS):
        acc = acc_ref[n]
        o_ref[0, :, lanes(n)] = jnp.where(in_h0, acc[:tq], acc[tq:]).astype(o_ref.dtype)


def _stick_breaking(pbd, tq):
    Bn, T, _ = pbd.shape
    ngrp = B_HEADS // 2 // SB_CHAINS
    w = SB_CHAINS * LANES
    return pl.pallas_call(
        _sb_kernel,
        grid=(Bn, ngrp, T // tq),
        in_specs=[pl.BlockSpec((1, tq, w), lambda b, p, i: (b, i, _BQ // SB_CHAINS + p)),
                  pl.BlockSpec((1, T, w), lambda b, p, i: (b, 0, _BK // SB_CHAINS + p)),
                  pl.BlockSpec((1, T, w), lambda b, p, i: (b, 0, _BV // SB_CHAINS + p))],
        out_specs=pl.BlockSpec((1, tq, w), lambda b, p, i: (b, i, p)),
        out_shape=jax.ShapeDtypeStruct((Bn, T, B_HEADS // 2 * LANES), BF16),
        scratch_shapes=[pltpu.VMEM((tq, tq), BF16), pltpu.VMEM((SB_CHAINS, 2 * tq, LANES), F32),
                        pltpu.VMEM((SB_CHAINS, 2 * tq, 1), F32)],
        compiler_params=pltpu.CompilerParams(
            dimension_semantics=("parallel", "parallel", "arbitrary"), vmem_limit_bytes=VMEM_LIMIT),
        name="stick_breaking",
    )(pbd, pbd, pbd)


def _win_kernel(q_ref, k_ref, kp_ref, v_ref, vp_ref, bias_ref, sink_ref, *rest, dil, nsub, nq, n_mix):
    mix_refs, out_refs = rest[:2 * n_mix], rest[2 * n_mix + (1 if n_mix else 0):]
    blk = WIN_BLOCK
    span = blk * dil
    nrow = 2 * nq * blk
    i = pl.program_id(2)
    lane = lax.broadcasted_iota(jnp.int32, (blk, LANES), 1)
    in_h0 = lane < HEAD_DIM
    prev_col = lax.broadcasted_iota(jnp.int32, (nrow, 2 * blk), 1) < blk
    ones = jnp.ones((2 * blk, LANES), BF16)
    sinks = jnp.concatenate([jnp.broadcast_to(sink_ref[hh:hh + 1, 0:1], (blk, 1)) for hh in range(2 * nq)],
                            axis=0)
    bias = bias_ref[...]

    def rows_of(start):
        return pl.ds(start, blk, stride=dil) if dil > 1 else pl.ds(start, blk)

    def residue_class(s, c):
        cur = rows_of(s * span + c)
        q = q_ref[0, cur, :].astype(BF16)
        if s == 0:
            kp, vp = kp_ref[0, rows_of(c), :], vp_ref[0, rows_of(c), :]
        else:
            prev = rows_of((s - 1) * span + c)
            kp, vp = k_ref[0, prev, :], v_ref[0, prev, :]
        k2 = jnp.concatenate([kp.astype(BF16), k_ref[0, cur, :].astype(BF16)], axis=0)
        v2 = jnp.concatenate([vp.astype(BF16), v_ref[0, cur, :].astype(BF16)], axis=0)
        qs = [q[:, p * LANES:(p + 1) * LANES] for p in range(nq)]
        zero = jnp.zeros_like(qs[0])
        q2 = jnp.concatenate([jnp.where(in_h0, x, zero) for x in qs]
                             + [jnp.where(in_h0, zero, x) for x in qs], axis=0)
        logits = _dot_nt(q2 * (HEAD_DIM ** -0.5), k2) + bias
        if s == 0:
            logits = jnp.where(jnp.logical_and(i == 0, prev_col), -jnp.inf, logits)
        m = jnp.max(jnp.maximum(logits[:, :blk], logits[:, blk:]), axis=-1, keepdims=True)
        m = jnp.maximum(m, sinks)
        p = jnp.exp(logits - m).astype(BF16)
        if nq > 1:
            pv = _dot(p, jnp.concatenate([v2, ones], axis=1))
            num, den = pv[:, :LANES], pv[:, LANES:]
        else:
            num, den = _dot(p, v2), _dot(p, ones)
        den = den + jnp.exp(sinks - m)
        o2 = num / den

        def lane_blocks(x):
            return jnp.concatenate([jnp.where(in_h0, x[b * blk:(b + 1) * blk], x[(nq + b) * blk:(nq + b + 1) * blk])
                                    for b in range(nq)], axis=1)

        out_refs[0][0, cur, :] = lane_blocks(o2).astype(out_refs[0].dtype)
        if len(out_refs) > 1:
            out_refs[1][0, cur, :] = lane_blocks(m + jnp.log(den))

    for s in range(nsub):
        if dil <= WIN_UNROLL:
            for c in range(dil):
                residue_class(s, c)
        else:
            def body(c, carry, s=s):
                residue_class(s, c)
                return carry
            lax.fori_loop(0, dil, body, 0, unroll=WIN_UNROLL)

    if n_mix:
        outs = [r[0] for r in mix_refs[0::2]] + [out_refs[0][0]]
        lses = [r[0] for r in mix_refs[1::2]] + [out_refs[1][0]]
        top = functools.reduce(jnp.maximum, lses)
        ws = [jnp.exp(l - top) for l in lses]
        y_ref = rest[2 * n_mix]
        y_ref[0] = (sum(w * o for w, o in zip(ws, outs)) / sum(ws)).astype(y_ref.dtype)


def _window_attn(slab, bias, sinks, dil, qcol, kcol, vcol, ngrp, nq, with_lse, out_dtype, mix=()):
    Bn, T, _ = slab.shape
    span = WIN_BLOCK * dil
    nsub = max(1, WIN_UNROLL // dil) * (WIN_TALL_SPANS if nq > 1 else 1)
    tb = span * nsub
    assert qcol % nq == 0

    def kvmap(col, prev):
        def f(b, g, i):
            return (b, jnp.maximum(i * nsub - 1, 0) if prev else i, col + g)
        return f

    qblk = (1, tb, nq * LANES)
    blk = (1, tb, LANES)
    pblk = (1, span, LANES)
    out_spec = pl.BlockSpec(qblk, lambda b, g, i: (b, i, g))
    out_sds = jax.ShapeDtypeStruct((Bn, T, ngrp * nq * LANES), out_dtype)
    lse_sds = jax.ShapeDtypeStruct((Bn, T, ngrp * nq * LANES), F32)
    if mix:
        out_specs, out_shape = [out_spec], [out_sds]
        scratch = [pltpu.VMEM(qblk, F32), pltpu.VMEM(qblk, F32)]
    else:
        out_specs = [out_spec, out_spec] if with_lse else [out_spec]
        out_shape = [out_sds, lse_sds] if with_lse else [out_sds]
        scratch = []
    return pl.pallas_call(
        functools.partial(_win_kernel, dil=dil, nsub=nsub, nq=nq, n_mix=len(mix) // 2),
        grid=(Bn, ngrp, T // tb),
        in_specs=[pl.BlockSpec(qblk, lambda b, g, i: (b, i, qcol // nq + g)),
                  pl.BlockSpec(blk, kvmap(kcol, False)), pl.BlockSpec(pblk, kvmap(kcol, True)),
                  pl.BlockSpec(blk, kvmap(vcol, False)), pl.BlockSpec(pblk, kvmap(vcol, True)),
                  pl.BlockSpec((None, 2 * nq * WIN_BLOCK, 2 * WIN_BLOCK), lambda b, g, i: (g, 0, 0)),
                  pl.BlockSpec((None, 2 * nq, LANES), lambda b, g, i: (g, 0, 0))] + [out_spec] * len(mix),
        out_specs=out_specs,
        out_shape=out_shape,
        scratch_shapes=scratch,
        compiler_params=pltpu.CompilerParams(
            dimension_semantics=("parallel", "parallel", "arbitrary"), vmem_limit_bytes=VMEM_LIMIT),
        name="window_attn",
    )(slab, slab, slab, slab, slab, bias, sinks, *mix)


def _merge_kernel(x_ref, g_ref, ya_ref, yb_ref, yc_ref, yd_ref,
                  ga_ref, gb_ref, gc_ref, gd_ref, wa_ref, wb_ref, wc_ref, wd_ref, wo_ref,
                  o_ref, u_ref):
    j = pl.program_id(1)

    @pl.when(j == 0)
    def _():
        u_ref[...] = _rms(x_ref[...], g_ref[0:1, :]).astype(BF16)
        o_ref[...] = jnp.zeros_like(o_ref)

    u = u_ref[...]
    merged = None
    for y_ref, gate_ref, w_ref in ((ya_ref, ga_ref, wa_ref), (yb_ref, gb_ref, wb_ref),
                                   (yc_ref, gc_ref, wc_ref), (yd_ref, gd_ref, wd_ref)):
        term = _sigmoid(_dot(u, gate_ref[...])) * _dot(y_ref[...], w_ref[...])
        merged = term if merged is None else merged + term
    o_ref[...] += _dot(merged.astype(BF16), wo_ref[...])

    @pl.when(j == pl.num_programs(1) - 1)
    def _():
        o_ref[...] = x_ref[...] + _rms(o_ref[...], g_ref[1:2, :])


def _merge(h, norm, ys, w_in, gate_col0, w_branch, w_out, layer, tm, tn):
    M, D = h.shape
    gcb = gate_col0 // tn
    dcb = D // tn

    def yspec(w):
        return pl.BlockSpec((tm, w), lambda i, j: (i, 0))

    def gspec(br):
        return pl.BlockSpec((None, D, tn), lambda i, j: (layer, 0, gcb + br * dcb + j))

    def wspec(rows, rb):
        return pl.BlockSpec((None, rows, tn), lambda i, j: (layer, rb, j))

    wa, wb, wc, wd = (y.shape[1] for y in ys)
    return pl.pallas_call(
        _merge_kernel,
        grid=(M // tm, dcb),
        in_specs=[pl.BlockSpec((tm, D), lambda i, j: (i, 0)),
                  pl.BlockSpec((None, 2, D), lambda i, j: (layer, 0, 0)),
                  yspec(wa), yspec(wb), yspec(wc), yspec(wd),
                  gspec(0), gspec(1), gspec(2), gspec(3),
                  wspec(wa, 0), wspec(wb, wa // wb), wspec(wc, (wa + wb + wd) // wc),
                  wspec(wd, (wa + wb) // wd),
                  pl.BlockSpec((None, tn, D), lambda i, j: (layer, j, 0))],
        out_specs=pl.BlockSpec((tm, D), lambda i, j: (i, 0)),
        out_shape=jax.ShapeDtypeStruct((M, D), F32),
        scratch_shapes=[pltpu.VMEM((tm, D), BF16)],
        compiler_params=pltpu.CompilerParams(
            dimension_semantics=("parallel", "arbitrary"), vmem_limit_bytes=VMEM_LIMIT),
        name="gated_merge",
    )(h, norm, *ys, w_in, w_in, w_in, w_in, w_branch, w_branch, w_branch, w_branch, w_out)


def _rel_bucket(dist):
    max_exact = REL_BUCKETS // 2
    d = jnp.maximum(dist, 1).astype(F32)
    large = max_exact + (jnp.log(d / max_exact) / math.log(REL_MAX_DIST / max_exact)
                         * (REL_BUCKETS - max_exact)).astype(jnp.int32)
    large = jnp.minimum(large, REL_BUCKETS - 1)
    return jnp.where(dist < max_exact, dist, large)


def _window_bias(rel_bias, heads, dilation, max_dist, rows_heads=2):
    row = jnp.arange(WIN_BLOCK)[:, None]
    col = jnp.arange(2 * WIN_BLOCK)[None, :]
    dist = row + WIN_BLOCK - col
    valid = (dist >= 0) & (dist <= max_dist)
    assert list(heads) == list(range(heads[0], heads[-1] + 1))
    bucket = _rel_bucket(jnp.maximum(dist, 0) * dilation)
    per_head = rel_bias.astype(F32)[:, heads[0]:heads[-1] + 1]
    hit = bucket[None, None] == jnp.arange(REL_BUCKETS)[:, None, None, None]
    tab = jnp.sum(jnp.where(hit, per_head[:, :, None, None], 0.0), axis=0)
    tab = jnp.where(valid[None], tab, -jnp.inf)
    return tab.reshape(len(heads) // rows_heads, rows_heads * WIN_BLOCK, 2 * WIN_BLOCK)


def _trunk(x, ffn1_norm, ffn1_w13, ffn1_w2, mix_norm, w_in, hgrn_lb_logits, hgrn_out_norm,
           attn_sinks, w_branch, w_out, ffn2_norm, ffn2_w13, ffn2_w2, rel_bias, cfg):
    Bn, T, D = x.shape
    depth = w_in.shape[0]
    M = Bn * T
    a_w = A_HEADS * A_DK
    a_cols = 4 * a_w
    b_cols = 3 * B_HEADS * HEAD_DIM
    c_cols = 3 * C_HEADS * HEAD_DIM
    ac_cols = AC_BLOCKS * LANES
    bd_cols = BD_BLOCKS * LANES
    mix_in = ac_cols + bd_cols

    lb_sm = jax.nn.softmax(hgrn_lb_logits.astype(F32), axis=0)
    lb_all = jnp.concatenate([jnp.zeros_like(lb_sm[0:1]), jnp.cumsum(lb_sm[1:], axis=0)], axis=0)
    lb_all = lb_all.reshape(depth, A_HEADS, 1, A_DK)

    b0, c0 = a_cols, a_cols + b_cols
    dq0 = c0 + c_cols
    dk0 = dq0 + D_HEADS * HEAD_DIM
    c_w = C_HPG * HEAD_DIM

    perm = jnp.array([h * HEAD_DIM + d for h in D_HEAD_PERM for d in range(HEAD_DIM)])

    def c_part(part, g0, g1):
        base = c0 + part * C_HEADS * HEAD_DIM
        return w_in[:, :, base + g0 * c_w:base + g1 * c_w]

    n_grp = len(C_PATTERNS)
    w_in_b = jnp.concatenate(
        [w_in[:, :, :a_cols]] + [c_part(p, 1, n_grp) for p in range(3)]
        + [w_in[:, :, b0:c0], w_in[:, :, dq0:dk0][:, :, perm], w_in[:, :, dk0:mix_in]]
        + [c_part(p, 0, 1) for p in range(3)] + [w_in[:, :, mix_in:]], axis=-1).astype(BF16)
    d0 = 2 * a_w + c_w
    w_br_b = jnp.concatenate(
        [w_branch[:, :2 * a_w], w_branch[:, d0:][:, perm], w_branch[:, 2 * a_w:d0]], axis=1).astype(BF16)
    w_out_b = w_out.astype(BF16)
    f1_w13, f1_w2 = ffn1_w13.astype(BF16), ffn1_w2.astype(BF16)
    f2_w13, f2_w2 = ffn2_w13.astype(BF16), ffn2_w2.astype(BF16)

    c_bias = [_window_bias(rel_bias, list(range(g * C_HPG, (g + 1) * C_HPG)), dil, win // dil)
              for g, (win, dil) in enumerate(C_PATTERNS)]
    d_bias = _window_bias(rel_bias, [C_HEADS + h for h in range(D_HEADS)], 1, D_WINDOW - 1, D_HEADS)
    no_sink = jnp.full((C_HPG // 2, 2, LANES), -jnp.inf, F32)

    h = x.reshape(M, D)
    for l in range(depth):
        h = _ffn(h, ffn1_norm, f1_w13, f1_w2, l, cfg["ffn_tm"], cfg["ffn_tf"])

        pac, pbd = _proj(h, mix_norm, w_in_b, l, ac_cols, bd_cols, cfg["proj_tm"], cfg["proj_tn"])
        pac = pac.reshape(Bn, T, ac_cols)
        pbd = pbd.reshape(Bn, T, bd_cols)

        ya = _hgrn(pac, lb_all[l], hgrn_out_norm[l].reshape(1, 1, A_DK).astype(F32), cfg["hgrn_tt"])
        yb = _stick_breaking(pbd, cfg["sb_tq"])
        mix = ()
        for g, (win, dil) in enumerate(C_PATTERNS):
            if dil == 1:
                slab, cols = pbd, (_C0Q, _C0K, _C0V)
            else:
                slab, cols = pac, tuple(c + 2 * (g - 1) for c in (_CQ, _CK, _CV))
            if g < len(C_PATTERNS) - 1:
                mix += tuple(_window_attn(slab, c_bias[g], no_sink, dil, *cols, C_HPG // 2, 1, True, F32))
            else:
                (yc,) = _window_attn(slab, c_bias[g], no_sink, dil, *cols, C_HPG // 2, 1, True, BF16, mix)
        sinks = jnp.broadcast_to(attn_sinks[l].astype(F32).reshape(1, D_HEADS, 1), (1, D_HEADS, LANES))
        (yd,) = _window_attn(pbd, d_bias, sinks, 1, _DQ, _DK, _DV, 1, D_HEADS // 2, False, BF16)

        ys = (ya.reshape(M, a_w), yb.reshape(M, a_w), yc.reshape(M, c_w), yd.reshape(M, a_w))
        h = _merge(h, mix_norm, ys, w_in_b, mix_in, w_br_b, w_out_b, l, cfg["merge_tm"], cfg["merge_tn"])

        h = _ffn(h, ffn2_norm, f2_w13, f2_w2, l, cfg["ffn_tm"], cfg["ffn_tf"])
    return h.reshape(Bn, T, D)


_CFG = dict(ffn_tm=512, ffn_tf=512, proj_tm=1024, proj_tn=512, hgrn_tt=512, sb_tq=256,
            merge_tm=512, merge_tn=512)


def kernel(x, ffn1_norm, ffn1_w13, ffn1_w2, mix_norm, w_in, hgrn_lb_logits, hgrn_out_norm, attn_sinks,
           w_branch, w_out, ffn2_norm, ffn2_w13, ffn2_w2, rel_bias):
    return _trunk(x, ffn1_norm, ffn1_w13, ffn1_w2, mix_norm, w_in, hgrn_lb_logits, hgrn_out_norm,
                  attn_sinks, w_branch, w_out, ffn2_norm, ffn2_w13, ffn2_w2, rel_bias, _CFG)
```

```python
import functools
import math

import jax
import jax.numpy as jnp
from jax import lax
from jax.experimental import pallas as pl
from jax.experimental.pallas import tpu as pltpu

F32 = jnp.float32
BF16 = jnp.bfloat16

EPS = 1e-6
HEAD_DIM = 64
LANES = 128
A_HEADS = 4
A_DK = 128
B_HEADS = 8
C_PATTERNS = ((128, 1), (512, 4), (2048, 16))
C_HPG = 4
C_HEADS = len(C_PATTERNS) * C_HPG
D_HEADS = 8
D_WINDOW = 128
WIN_BLOCK = 128
REL_BUCKETS = 32
REL_MAX_DIST = 2048
A_SUB = 16
VMEM_LIMIT = 60 * 1024 * 1024

_CQ, _CK, _CV = 16, 20, 24
AC_BLOCKS = 28
_BQ, _BK, _BV = 0, 4, 8
_DQ, _DK, _DV = 12, 16, 17
_C0Q, _C0K, _C0V = 18, 20, 22
BD_BLOCKS = 24
WIN_UNROLL = 8
WIN_TALL_SPANS = 4
D_HEAD_PERM = (0, 4, 1, 5, 2, 6, 3, 7)
SB_CHAINS = 4
SB_EXIT = 100.0


def _rms(x, g):
    ms = jnp.mean(x * x, axis=-1, keepdims=True)
    return x * lax.rsqrt(ms + EPS) * g


def _sigmoid(x):
    return 1.0 / (1.0 + jnp.exp(-x))


def _dot(a, b):
    return jnp.dot(a, b, preferred_element_type=F32)


def _dot_nt(a, b):
    return lax.dot_general(a, b, (((1,), (1,)), ((), ())), preferred_element_type=F32)


def _split_bf16(x):
    hi = x.astype(BF16)
    lo = (x - hi.astype(F32)).astype(BF16)
    return hi, lo


def _ffn_kernel(x_ref, g_ref, w1_ref, w3_ref, w2_ref, o_ref, xn_ref):
    j = pl.program_id(1)

    @pl.when(j == 0)
    def _():
        xn_ref[...] = _rms(x_ref[...], g_ref[0:1, :]).astype(BF16)
        o_ref[...] = jnp.zeros_like(o_ref)

    xn = xn_ref[...]
    gate = _dot(xn, w1_ref[...])
    up = _dot(xn, w3_ref[...])
    act = (gate * _sigmoid(gate) * up).astype(BF16)
    o_ref[...] += _dot(act, w2_ref[...])

    @pl.when(j == pl.num_programs(1) - 1)
    def _():
        o_ref[...] = x_ref[...] + 0.5 * _rms(o_ref[...], g_ref[1:2, :])


def _ffn(h, norm, w13, w2, layer, tm, tf):
    M, D = h.shape
    F = w2.shape[1]
    nf = F // tf
    return pl.pallas_call(
        _ffn_kernel,
        grid=(M // tm, nf),
        in_specs=[
            pl.BlockSpec((tm, D), lambda i, j: (i, 0)),
            pl.BlockSpec((None, 2, D), lambda i, j: (layer, 0, 0)),
            pl.BlockSpec((None, D, tf), lambda i, j: (layer, 0, j)),
            pl.BlockSpec((None, D, tf), lambda i, j: (layer, 0, j + nf)),
            pl.BlockSpec((None, tf, D), lambda i, j: (layer, j, 0)),
        ],
        out_specs=pl.BlockSpec((tm, D), lambda i, j: (i, 0)),
        out_shape=jax.ShapeDtypeStruct((M, D), F32),
        scratch_shapes=[pltpu.VMEM((tm, D), BF16)],
        compiler_params=pltpu.CompilerParams(
            dimension_semantics=("parallel", "arbitrary"), vmem_limit_bytes=VMEM_LIMIT),
        name="ffn",
    )(h, norm, w13, w13, w2)


def _proj_kernel(x_ref, g_ref, w_ref, o32_ref, o16_ref, xn_ref, *, n32):
    j = pl.program_id(1)

    @pl.when(j == 0)
    def _():
        xn_ref[...] = _rms(x_ref[...], g_ref[0:1, :]).astype(BF16)

    @pl.when(j < n32)
    def _():
        o32_ref[...] = _dot(xn_ref[...], w_ref[...])

    @pl.when(j >= n32)
    def _():
        o16_ref[...] = _dot(xn_ref[...], w_ref[...]).astype(BF16)


def _proj(h, norm, w_in, layer, cols32, cols16, tm, tn):
    M, D = h.shape
    n32, n16 = cols32 // tn, cols16 // tn
    return pl.pallas_call(
        functools.partial(_proj_kernel, n32=n32),
        grid=(M // tm, n32 + n16),
        in_specs=[
            pl.BlockSpec((tm, D), lambda i, j: (i, 0)),
            pl.BlockSpec((None, 2, D), lambda i, j: (layer, 0, 0)),
            pl.BlockSpec((None, D, tn), lambda i, j: (layer, 0, j)),
        ],
        out_specs=[pl.BlockSpec((tm, tn), lambda i, j: (i, jnp.minimum(j, n32 - 1))),
                   pl.BlockSpec((tm, tn), lambda i, j: (i, jnp.maximum(j - n32, 0)))],
        out_shape=[jax.ShapeDtypeStruct((M, cols32), F32), jax.ShapeDtypeStruct((M, cols16), BF16)],
        scratch_shapes=[pltpu.VMEM((tm, D), BF16)],
        compiler_params=pltpu.CompilerParams(
            dimension_semantics=("parallel", "arbitrary"), vmem_limit_bytes=VMEM_LIMIT),
        name="proj",
    )(h, norm, w_in)


def _hgrn_kernel(q_ref, f_ref, i_ref, g_ref, lb_ref, ng_ref, o_ref, st_ref, kv_ref, sall_ref):
    tt = q_ref.shape[1]
    sub = A_SUB
    nb = tt // sub
    per = LANES // sub
    shape3 = (nb, sub, LANES)

    @pl.when(pl.program_id(2) == 0)
    def _():
        st_ref[...] = jnp.zeros_like(st_ref)

    lb = lb_ref[0]
    f = lb + (1.0 - lb) * _sigmoid(f_ref[0])
    k = 1.0 - f
    gl = jnp.log2(f)
    r = lax.broadcasted_iota(jnp.int32, (LANES, LANES), 0)
    c = lax.broadcasted_iota(jnp.int32, (LANES, LANES), 1)
    tril = ((r // sub == c // sub) & (c <= r)).astype(BF16)
    cums = []
    for a in range(tt // LANES):
        hi, lo = _split_bf16(gl[a * LANES:(a + 1) * LANES])
        cums.append(_dot(tril, hi) + _dot(tril, lo))
    b = jnp.concatenate(cums, axis=0)
    q = q_ref[0]
    v = i_ref[0]
    b3, q3, k3, v3 = (x.reshape(shape3) for x in (b, q, k, v))

    ones = jnp.ones((LANES, LANES), BF16)
    half = sub // 2
    rows = lax.broadcasted_iota(jnp.int32, (1, sub, LANES), 1)
    acc = [jnp.zeros(shape3, F32), jnp.zeros((nb, half, LANES), F32)]
    for s in range(sub):
        up = s >= half
        lo_row = half if up else 0
        qs, bs, rs = (x[:, lo_row:, :] for x in (q3, b3, rows))
        w = jnp.where(rs >= s, qs * jnp.exp2(bs - b3[:, s:s + 1, :]) * k3[:, s:s + 1, :], 0.0)
        w = w.reshape(-1, LANES).astype(BF16)
        acc[up] = acc[up] + _dot(w, ones).reshape(qs.shape) * v3[:, s:s + 1, :]
    o3 = acc[0] + jnp.concatenate([jnp.zeros_like(acc[1]), acc[1]], axis=1)

    b_last = b3[:, sub - 1:sub, :]
    kd = (k3 * jnp.exp2(b_last - b3)).reshape(tt, LANES)
    dec = jnp.exp2(b_last)
    for a in range(tt // LANES):
        v_t = v[a * LANES:(a + 1) * LANES].T.astype(BF16)
        kd_a = kd[a * LANES:(a + 1) * LANES]
        rhs = jnp.concatenate([jnp.where(r // sub == j, kd_a, 0.0).astype(BF16) for j in range(per)], axis=1)
        kv = _dot(v_t, rhs)
        for j in range(per):
            kv_ref[a * per + j] = kv[:, j * LANES:(j + 1) * LANES]

    st = st_ref[...]
    for n in range(nb):
        sall_ref[n] = st.astype(BF16)
        st = dec[n] * st + kv_ref[n]
    st_ref[...] = st

    qe = (q * jnp.exp2(b)).astype(BF16)
    inter = []
    for a in range(tt // LANES):
        stack = sall_ref[a * per:(a + 1) * per].reshape(per * LANES, LANES)
        p = _dot_nt(qe[a * LANES:(a + 1) * LANES], stack)
        inter += [p[j * sub:(j + 1) * sub, j * LANES:(j + 1) * LANES] for j in range(per)]
    o = o3.reshape(tt, LANES) + jnp.concatenate(inter, axis=0)

    g = g_ref[0]
    o_ref[0] = (_rms(o, ng_ref[0]) * (g * _sigmoid(g))).astype(o_ref.dtype)


def _hgrn(pa, lb, ng, tt):
    Bn, T, _ = pa.shape
    H = A_HEADS

    def col(off):
        return pl.BlockSpec((1, tt, A_DK), lambda b, h, t: (b, t, off * H + h))

    return pl.pallas_call(
        _hgrn_kernel,
        grid=(Bn, H, T // tt),
        in_specs=[col(0), col(1), col(2), col(3),
                  pl.BlockSpec((1, 1, A_DK), lambda b, h, t: (h, 0, 0)),
                  pl.BlockSpec((1, 1, A_DK), lambda b, h, t: (0, 0, 0))],
        out_specs=pl.BlockSpec((1, tt, A_DK), lambda b, h, t: (b, t, h)),
        out_shape=jax.ShapeDtypeStruct((Bn, T, H * A_DK), BF16),
        scratch_shapes=[pltpu.VMEM((A_DK, A_DK), F32), pltpu.VMEM((tt // A_SUB, A_DK, A_DK), F32),
                        pltpu.VMEM((tt // A_SUB, A_DK, A_DK), BF16)],
        compiler_params=pltpu.CompilerParams(
            dimension_semantics=("parallel", "parallel", "arbitrary"), vmem_limit_bytes=VMEM_LIMIT),
        name="hgrn2",
    )(pa, pa, pa, pa, lb, ng)


def _sb_kernel(q_ref, k_ref, v_ref, o_ref, suf_ref, acc_ref, run_ref):
    tq = q_ref.shape[1]
    i = pl.program_id(2)
    in_h0 = lax.broadcasted_iota(jnp.int32, (tq, LANES), 1) < HEAD_DIM
    r = lax.broadcasted_iota(jnp.int32, (tq, tq), 0)
    c = lax.broadcasted_iota(jnp.int32, (tq, tq), 1)
    suf_ref[...] = (r >= c).astype(BF16)
    strict = jnp.concatenate([c < r, c < r], axis=0)

    def lanes(n):
        return slice(n * LANES, (n + 1) * LANES)

    q2s = []
    for n in range(SB_CHAINS):
        q = q_ref[0, :, lanes(n)]
        zero = jnp.zeros_like(q)
        q2s.append(jnp.concatenate([jnp.where(in_h0, q, zero), jnp.where(in_h0, zero, q)], axis=0)
                   * (HEAD_DIM ** -0.5))

    def tile(q2, kb, vb, run, diag):
        z = _dot_nt(q2, kb)
        sp = jnp.maximum(z, 0.0) + jnp.log(1.0 + jnp.exp(-jnp.abs(z)))
        if diag:
            sp = jnp.where(strict, sp, 0.0)
        hi, lo = _split_bf16(sp)
        suffix = suf_ref[...]
        cs = _dot(hi, suffix) + _dot(lo, suffix) + run
        a = jnp.exp(z - cs)
        if diag:
            a = jnp.where(strict, a, 0.0)
        return _dot(a.astype(BF16), vb), cs[:, 0:1]

    def step(rows, first):
        lowest = None
        for n in range(SB_CHAINS):
            run = jnp.zeros((2 * tq, 1), F32) if first else run_ref[n]
            pv, run = tile(q2s[n], k_ref[0, rows, lanes(n)], v_ref[0, rows, lanes(n)], run, first)
            acc_ref[n] = pv if first else acc_ref[n] + pv
            run_ref[n] = run
            low = jnp.min(run)
            lowest = low if lowest is None else jnp.minimum(lowest, low)
        return lowest

    lowest = step(pl.ds(pl.multiple_of(i * tq, tq), tq), True)

    def cond(carry):
        s, lowest = carry
        return (s < i) & (lowest <= SB_EXIT)

    def body(carry):
        s, _ = carry
        return s + 1, step(pl.ds(pl.multiple_of((i - 1 - s) * tq, tq), tq), False)

    lax.while_loop(cond, body, (jnp.int32(0), lowest))
    for n in range(SB_CHAINS):
        acc = acc_ref[n]
        o_ref[0, :, lanes(n)] = jnp.where(in_h0, acc[:tq], acc[tq:]).astype(o_ref.dtype)


def _stick_breaking(pbd, tq):
    Bn, T, _ = pbd.shape
    ngrp = B_HEADS // 2 // SB_CHAINS
    w = SB_CHAINS * LANES
    return pl.pallas_call(
        _sb_kernel,
        grid=(Bn, ngrp, T // tq),
        in_specs=[pl.BlockSpec((1, tq, w), lambda b, p, i: (b, i, _BQ // SB_CHAINS + p)),
                  pl.BlockSpec((1, T, w), lambda b, p, i: (b, 0, _BK // SB_CHAINS + p)),
                  pl.BlockSpec((1, T, w), lambda b, p, i: (b, 0, _BV // SB_CHAINS + p))],
        out_specs=pl.BlockSpec((1, tq, w), lambda b, p, i: (b, i, p)),
        out_shape=jax.ShapeDtypeStruct((Bn, T, B_HEADS // 2 * LANES), BF16),
        scratch_shapes=[pltpu.VMEM((tq, tq), BF16), pltpu.VMEM((SB_CHAINS, 2 * tq, LANES), F32),
                        pltpu.VMEM((SB_CHAINS, 2 * tq, 1), F32)],
        compiler_params=pltpu.CompilerParams(
            dimension_semantics=("parallel", "parallel", "arbitrary"), vmem_limit_bytes=VMEM_LIMIT),
        name="stick_breaking",
    )(pbd, pbd, pbd)


def _win_kernel(q_ref, k_ref, kp_ref, v_ref, vp_ref, bias_ref, sink_ref, *rest, dil, nsub, nq, n_mix):
    mix_refs, out_refs = rest[:2 * n_mix], rest[2 * n_mix + (1 if n_mix else 0):]
    blk = WIN_BLOCK
    span = blk * dil
    nrow = 2 * nq * blk
    i = pl.program_id(2)
    lane = lax.broadcasted_iota(jnp.int32, (blk, LANES), 1)
    in_h0 = lane < HEAD_DIM
    prev_col = lax.broadcasted_iota(jnp.int32, (nrow, 2 * blk), 1) < blk
    ones = jnp.ones((2 * blk, LANES), BF16)
    sinks = jnp.concatenate([jnp.broadcast_to(sink_ref[hh:hh + 1, 0:1], (blk, 1)) for hh in range(2 * nq)],
                            axis=0)
    bias = bias_ref[...]

    def rows_of(start):
        return pl.ds(start, blk, stride=dil) if dil > 1 else pl.ds(start, blk)

    def residue_class(s, c):
        cur = rows_of(s * span + c)
        q = q_ref[0, cur, :].astype(BF16)
        if s == 0:
            kp, vp = kp_ref[0, rows_of(c), :], vp_ref[0, rows_of(c), :]
        else:
            prev = rows_of((s - 1) * span + c)
            kp, vp = k_ref[0, prev, :], v_ref[0, prev, :]
        k2 = jnp.concatenate([kp.astype(BF16), k_ref[0, cur, :].astype(BF16)], axis=0)
        v2 = jnp.concatenate([vp.astype(BF16), v_ref[0, cur, :].astype(BF16)], axis=0)
        qs = [q[:, p * LANES:(p + 1) * LANES] for p in range(nq)]
        zero = jnp.zeros_like(qs[0])
        q2 = jnp.concatenate([jnp.where(in_h0, x, zero) for x in qs]
                             + [jnp.where(in_h0, zero, x) for x in qs], axis=0)
        logits = _dot_nt(q2 * (HEAD_DIM ** -0.5), k2) + bias
        if s == 0:
            logits = jnp.where(jnp.logical_and(i == 0, prev_col), -jnp.inf, logits)
        m = jnp.max(jnp.maximum(logits[:, :blk], logits[:, blk:]), axis=-1, keepdims=True)
        m = jnp.maximum(m, sinks)
        p = jnp.exp(logits - m).astype(BF16)
        if nq > 1:
            pv = _dot(p, jnp.concatenate([v2, ones], axis=1))
            num, den = pv[:, :LANES], pv[:, LANES:]
        else:
            num, den = _dot(p, v2), _dot(p, ones)
        den = den + jnp.exp(sinks - m)
        o2 = num / den

        def lane_blocks(x):
            return jnp.concatenate([jnp.where(in_h0, x[b * blk:(b + 1) * blk], x[(nq + b) * blk:(nq + b + 1) * blk])
                                    for b in range(nq)], axis=1)

        out_refs[0][0, cur, :] = lane_blocks(o2).astype(out_refs[0].dtype)
        if len(out_refs) > 1:
            out_refs[1][0, cur, :] = lane_blocks(m + jnp.log(den))

    for s in range(nsub):
        if dil <= WIN_UNROLL:
            for c in range(dil):
                residue_class(s, c)
        else:
            def body(c, carry, s=s):
                residue_class(s, c)
                return carry
            lax.fori_loop(0, dil, body, 0, unroll=WIN_UNROLL)

    if n_mix:
        outs = [r[0] for r in mix_refs[0::2]] + [out_refs[0][0]]
        lses = [r[0] for r in mix_refs[1::2]] + [out_refs[1][0]]
        top = functools.reduce(jnp.maximum, lses)
        ws = [jnp.exp(l - top) for l in lses]
        y_ref = rest[2 * n_mix]
        y_ref[0] = (sum(w * o for w, o in zip(ws, outs)) / sum(ws)).astype(y_ref.dtype)


def _window_attn(slab, bias, sinks, dil, qcol, kcol, vcol, ngrp, nq, with_lse, out_dtype, mix=()):
    Bn, T, _ = slab.shape
    span = WIN_BLOCK * dil
    nsub = max(1, WIN_UNROLL // dil) * (WIN_TALL_SPANS if nq > 1 else 1)
    tb = span * nsub
    assert qcol % nq == 0

    def kvmap(col, prev):
        def f(b, g, i):
            return (b, jnp.maximum(i * nsub - 1, 0) if prev else i, col + g)
        return f

    qblk = (1, tb, nq * LANES)
    blk = (1, tb, LANES)
    pblk = (1, span, LANES)
    out_spec = pl.BlockSpec(qblk, lambda b, g, i: (b, i, g))
    out_sds = jax.ShapeDtypeStruct((Bn, T, ngrp * nq * LANES), out_dtype)
    lse_sds = jax.ShapeDtypeStruct((Bn, T, ngrp * nq * LANES), F32)
    if mix:
        out_specs, out_shape = [out_spec], [out_sds]
        scratch = [pltpu.VMEM(qblk, F32), pltpu.VMEM(qblk, F32)]
    else:
        out_specs = [out_spec, out_spec] if with_lse else [out_spec]
        out_shape = [out_sds, lse_sds] if with_lse else [out_sds]
        scratch = []
    return pl.pallas_call(
        functools.partial(_win_kernel, dil=dil, nsub=nsub, nq=nq, n_mix=len(mix) // 2),
        grid=(Bn, ngrp, T // tb),
        in_specs=[pl.BlockSpec(qblk, lambda b, g, i: (b, i, qcol // nq + g)),
                  pl.BlockSpec(blk, kvmap(kcol, False)), pl.BlockSpec(pblk, kvmap(kcol, True)),
                  pl.BlockSpec(blk, kvmap(vcol, False)), pl.BlockSpec(pblk, kvmap(vcol, True)),
                  pl.BlockSpec((None, 2 * nq * WIN_BLOCK, 2 * WIN_BLOCK), lambda b, g, i: (g, 0, 0)),
                  pl.BlockSpec((None, 2 * nq, LANES), lambda b, g, i: (g, 0, 0))] + [out_spec] * len(mix),
        out_specs=out_specs,
        out_shape=out_shape,
        scratch_shapes=scratch,
        compiler_params=pltpu.CompilerParams(
            dimension_semantics=("parallel", "parallel", "arbitrary"), vmem_limit_bytes=VMEM_LIMIT),
        name="window_attn",
    )(slab, slab, slab, slab, slab, bias, sinks, *mix)


def _merge_kernel(x_ref, g_ref, ya_ref, yb_ref, yc_ref, yd_ref,
                  ga_ref, gb_ref, gc_ref, gd_ref, wa_ref, wb_ref, wc_ref, wd_ref, wo_ref,
                  o_ref, u_ref):
    j = pl.program_id(1)

    @pl.when(j == 0)
    def _():
        u_ref[...] = _rms(x_ref[...], g_ref[0:1, :]).astype(BF16)
        o_ref[...] = jnp.zeros_like(o_ref)

    u = u_ref[...]
    merged = None
    for y_ref, gate_ref, w_ref in ((ya_ref, ga_ref, wa_ref), (yb_ref, gb_ref, wb_ref),
                                   (yc_ref, gc_ref, wc_ref), (yd_ref, gd_ref, wd_ref)):
        term = _sigmoid(_dot(u, gate_ref[...])) * _dot(y_ref[...], w_ref[...])
        merged = term if merged is None else merged + term
    o_ref[...] += _dot(merged.astype(BF16), wo_ref[...])

    @pl.when(j == pl.num_programs(1) - 1)
    def _():
        o_ref[...] = x_ref[...] + _rms(o_ref[...], g_ref[1:2, :])


def _merge(h, norm, ys, w_in, gate_col0, w_branch, w_out, layer, tm, tn):
    M, D = h.shape
    gcb = gate_col0 // tn
    dcb = D // tn

    def yspec(w):
        return pl.BlockSpec((tm, w), lambda i, j: (i, 0))

    def gspec(br):
        return pl.BlockSpec((None, D, tn), lambda i, j: (layer, 0, gcb + br * dcb + j))

    def wspec(rows, rb):
        return pl.BlockSpec((None, rows, tn), lambda i, j: (layer, rb, j))

    wa, wb, wc, wd = (y.shape[1] for y in ys)
    return pl.pallas_call(
        _merge_kernel,
        grid=(M // tm, dcb),
        in_specs=[pl.BlockSpec((tm, D), lambda i, j: (i, 0)),
                  pl.BlockSpec((None, 2, D), lambda i, j: (layer, 0, 0)),
                  yspec(wa), yspec(wb), yspec(wc), yspec(wd),
                  gspec(0), gspec(1), gspec(2), gspec(3),
                  wspec(wa, 0), wspec(wb, wa // wb), wspec(wc, (wa + wb + wd) // wc),
                  wspec(wd, (wa + wb) // wd),
                  pl.BlockSpec((None, tn, D), lambda i, j: (layer, j, 0))],
        out_specs=pl.BlockSpec((tm, D), lambda i, j: (i, 0)),
        out_shape=jax.ShapeDtypeStruct((M, D), F32),
        scratch_shapes=[pltpu.VMEM((tm, D), BF16)],
        compiler_params=pltpu.CompilerParams(
            dimension_semantics=("parallel", "arbitrary"), vmem_limit_bytes=VMEM_LIMIT),
        name="gated_merge",
    )(h, norm, *ys, w_in, w_in, w_in, w_in, w_branch, w_branch, w_branch, w_branch, w_out)


def _rel_bucket(dist):
    max_exact = REL_BUCKETS // 2
    d = jnp.maximum(dist, 1).astype(F32)
    large = max_exact + (jnp.log(d / max_exact) / math.log(REL_MAX_DIST / max_exact)
                         * (REL_BUCKETS - max_exact)).astype(jnp.int32)
    large = jnp.minimum(large, REL_BUCKETS - 1)
    return jnp.where(dist < max_exact, dist, large)


def _window_bias(rel_bias, heads, dilation, max_dist, rows_heads=2):
    row = jnp.arange(WIN_BLOCK)[:, None]
    col = jnp.arange(2 * WIN_BLOCK)[None, :]
    dist = row + WIN_BLOCK - col
    valid = (dist >= 0) & (dist <= max_dist)
    assert list(heads) == list(range(heads[0], heads[-1] + 1))
    bucket = _rel_bucket(jnp.maximum(dist, 0) * dilation)
    per_head = rel_bias.astype(F32)[:, heads[0]:heads[-1] + 1]
    hit = bucket[None, None] == jnp.arange(REL_BUCKETS)[:, None, None, None]
    tab = jnp.sum(jnp.where(hit, per_head[:, :, None, None], 0.0), axis=0)
    tab = jnp.where(valid[None], tab, -jnp.inf)
    return tab.reshape(len(heads) // rows_heads, rows_heads * WIN_BLOCK, 2 * WIN_BLOCK)


def _trunk(x, ffn1_norm, ffn1_w13, ffn1_w2, mix_norm, w_in, hgrn_lb_logits, hgrn_out_norm,
           attn_sinks, w_branch, w_out, ffn2_norm, ffn2_w13, ffn2_w2, rel_bias, cfg):
    Bn, T, D = x.shape
    depth = w_in.shape[0]
    M = Bn * T
    a_w = A_HEADS * A_DK
    a_cols = 4 * a_w
    b_cols = 3 * B_HEADS * HEAD_DIM
    c_cols = 3 * C_HEADS * HEAD_DIM
    ac_cols = AC_BLOCKS * LANES
    bd_cols = BD_BLOCKS * LANES
    mix_in = ac_cols + bd_cols

    lb_sm = jax.nn.softmax(hgrn_lb_logits.astype(F32), axis=0)
    lb_all = jnp.concatenate([jnp.zeros_like(lb_sm[0:1]), jnp.cumsum(lb_sm[1:], axis=0)], axis=0)
    lb_all = lb_all.reshape(depth, A_HEADS, 1, A_DK)

    b0, c0 = a_cols, a_cols + b_cols
    dq0 = c0 + c_cols
    dk0 = dq0 + D_HEADS * HEAD_DIM
    c_w = C_HPG * HEAD_DIM

    perm = jnp.array([h * HEAD_DIM + d for h in D_HEAD_PERM for d in range(HEAD_DIM)])

    def c_part(part, g0, g1):
        base = c0 + part * C_HEADS * HEAD_DIM
        return w_in[:, :, base + g0 * c_w:base + g1 * c_w]

    n_grp = len(C_PATTERNS)
    w_in_b = jnp.concatenate(
        [w_in[:, :, :a_cols]] + [c_part(p, 1, n_grp) for p in range(3)]
        + [w_in[:, :, b0:c0], w_in[:, :, dq0:dk0][:, :, perm], w_in[:, :, dk0:mix_in]]
        + [c_part(p, 0, 1) for p in range(3)] + [w_in[:, :, mix_in:]], axis=-1).astype(BF16)
    d0 = 2 * a_w + c_w
    w_br_b = jnp.concatenate(
        [w_branch[:, :2 * a_w], w_branch[:, d0:][:, perm], w_branch[:, 2 * a_w:d0]], axis=1).astype(BF16)
    w_out_b = w_out.astype(BF16)
    f1_w13, f1_w2 = ffn1_w13.astype(BF16), ffn1_w2.astype(BF16)
    f2_w13, f2_w2 = ffn2_w13.astype(BF16), ffn2_w2.astype(BF16)

    c_bias = [_window_bias(rel_bias, list(range(g * C_HPG, (g + 1) * C_HPG)), dil, win // dil)
              for g, (win, dil) in enumerate(C_PATTERNS)]
    d_bias = _window_bias(rel_bias, [C_HEADS + h for h in range(D_HEADS)], 1, D_WINDOW - 1, D_HEADS)
    no_sink = jnp.full((C_HPG // 2, 2, LANES), -jnp.inf, F32)

    h = x.reshape(M, D)
    for l in range(depth):
        h = _ffn(h, ffn1_norm, f1_w13, f1_w2, l, cfg["ffn_tm"], cfg["ffn_tf"])

        pac, pbd = _proj(h, mix_norm, w_in_b, l, ac_cols, bd_cols, cfg["proj_tm"], cfg["proj_tn"])
        pac = pac.reshape(Bn, T, ac_cols)
        pbd = pbd.reshape(Bn, T, bd_cols)

        ya = _hgrn(pac, lb_all[l], hgrn_out_norm[l].reshape(1, 1, A_DK).astype(F32), cfg["hgrn_tt"])
        yb = _stick_breaking(pbd, cfg["sb_tq"])
        mix = ()
        for g, (win, dil) in enumerate(C_PATTERNS):
            if dil == 1:
                slab, cols = pbd, (_C0Q, _C0K, _C0V)
            else:
                slab, cols = pac, tuple(c + 2 * (g - 1) for c in (_CQ, _CK, _CV))
            if g < len(C_PATTERNS) - 1:
                mix += tuple(_window_attn(slab, c_bias[g], no_sink, dil, *cols, C_HPG // 2, 1, True, F32))
            else:
                (yc,) = _window_attn(slab, c_bias[g], no_sink, dil, *cols, C_HPG // 2, 1, True, BF16, mix)
        sinks = jnp.broadcast_to(attn_sinks[l].astype(F32).reshape(1, D_HEADS, 1), (1, D_HEADS, LANES))
        (yd,) = _window_attn(pbd, d_bias, sinks, 1, _DQ, _DK, _DV, 1, D_HEADS // 2, False, BF16)

        ys = (ya.reshape(M, a_w), yb.reshape(M, a_w), yc.reshape(M, c_w), yd.reshape(M, a_w))
        h = _merge(h, mix_norm, ys, w_in_b, mix_in, w_br_b, w_out_b, l, cfg["merge_tm"], cfg["merge_tn"])

        h = _ffn(h, ffn2_norm, f2_w13, f2_w2, l, cfg["ffn_tm"], cfg["ffn_tf"])
    return h.reshape(Bn, T, D)


_CFG = dict(ffn_tm=512, ffn_tf=512, proj_tm=1024, proj_tn=512, hgrn_tt=512, sb_tq=256,
            merge_tm=512, merge_tn=512)


def kernel(x, ffn1_norm, ffn1_w13, ffn1_w2, mix_norm, w_in, hgrn_lb_logits, hgrn_out_norm, attn_sinks,
           w_branch, w_out, ffn2_norm, ffn2_w13, ffn2_w2, rel_bias):
    return _trunk(x, ffn1_norm, ffn1_w13, ffn1_w2, mix_norm, w_in, hgrn_lb_logits, hgrn_out_norm,
                  attn_sinks, w_branch, w_out, ffn2_norm, ffn2_w13, ffn2_w2, rel_bias, _CFG)
```
